```python
import jax, jax.numpy as jnp
from jax import lax
import numpy as np

D_MODEL = 2048
BATCH = 4
SEQ = 8192
DEPTH = 4

RMS_EPS = 1e-5

HEAD_DIM = 64
SWA_Q_HEADS = (D_MODEL // 2) // HEAD_DIM
SWA_GQA_RATIO = 4
SWA_KV_HEADS = SWA_Q_HEADS // SWA_GQA_RATIO
SWA_GROUP = SWA_Q_HEADS // SWA_KV_HEADS
WINDOW = 128
BLOCK = 128

MLSTM_HEADS = 4
MLSTM_V_DIM = (D_MODEL // 2) // MLSTM_HEADS
MLSTM_QK_DIM = MLSTM_V_DIM // 2
MLSTM_CHUNK = 128
GATE_SOFTCAP = 15.0

FOX_HEAD_DIM = 128
FOX_HEADS = D_MODEL // FOX_HEAD_DIM
FOX_BLOCK = 128

FFN_HIDDEN = -(-(8 * D_MODEL) // (3 * 256)) * 256

EVEN_SIZES = (SWA_Q_HEADS * HEAD_DIM, SWA_KV_HEADS * HEAD_DIM, SWA_KV_HEADS * HEAD_DIM,
              MLSTM_HEADS * MLSTM_QK_DIM, MLSTM_HEADS * MLSTM_QK_DIM,
              MLSTM_HEADS * MLSTM_V_DIM, MLSTM_HEADS * MLSTM_V_DIM,
              MLSTM_HEADS, MLSTM_HEADS)
EVEN_IN = sum(EVEN_SIZES)
EVEN_MIX = SWA_Q_HEADS * HEAD_DIM + MLSTM_HEADS * MLSTM_V_DIM
FOX_SIZES = (FOX_HEADS * FOX_HEAD_DIM, FOX_HEADS * FOX_HEAD_DIM, FOX_HEADS * FOX_HEAD_DIM, FOX_HEADS)
FOX_IN = sum(FOX_SIZES)
FOX_MIX = FOX_HEADS * FOX_HEAD_DIM

kernel_name = "hybrid_swa_mlstm_fox_trunk"


def rms_norm(x, g):
    xf = x.astype(jnp.float32)
    y = xf * lax.rsqrt(jnp.mean(xf * xf, axis=-1, keepdims=True) + RMS_EPS)
    return (y * g.astype(jnp.float32)).astype(x.dtype)


def split_cols(t, sizes):
    out, start = [], 0
    for s in sizes:
        out.append(t[..., start:start + s])
        start += s
    return out


def softcap(t):
    return GATE_SOFTCAP * jnp.tanh(t / GATE_SOFTCAP)


def alibi_slopes(n):
    return 2.0 ** (-8.0 * jnp.arange(1, n + 1, dtype=jnp.float32) / n)


def sliding_window_attention(q, k, v, sinks):
    B, S = q.shape[0], q.shape[1]
    nb = S // BLOCK
    qb = q.reshape(B, nb, BLOCK, SWA_KV_HEADS, SWA_GROUP, HEAD_DIM)

    def band(t):
        tb = t.reshape(B, nb, BLOCK, SWA_KV_HEADS, HEAD_DIM)
        prev = jnp.concatenate([jnp.zeros_like(tb[:, :1]), tb[:, :-1]], axis=1)
        return jnp.concatenate([prev, tb], axis=2)

    kb, vb = band(k), band(v)
    qi = jnp.arange(BLOCK)[:, None] + BLOCK
    kj = jnp.arange(2 * BLOCK)[None, :]
    dist = qi - kj
    key_abs = (jnp.arange(nb)[:, None, None] - 1) * BLOCK + kj
    valid = (dist >= 0) & (dist < WINDOW) & (key_abs >= 0)
    slopes = alibi_slopes(SWA_Q_HEADS).reshape(SWA_KV_HEADS, SWA_GROUP, 1, 1)
    scores = jnp.einsum('bnqhgd,bnkhd->bnhgqk', qb, kb,
                        preferred_element_type=jnp.float32) * (HEAD_DIM ** -0.5)
    scores = scores - slopes * dist.astype(jnp.float32)
    scores = jnp.where(valid[:, None, None], scores, -jnp.inf)
    sink = sinks.astype(jnp.float32).reshape(SWA_KV_HEADS, SWA_GROUP, 1, 1)
    m = jnp.maximum(jnp.max(scores, axis=-1, keepdims=True), sink)
    p = jnp.exp(scores - m)
    probs = p / (jnp.sum(p, axis=-1, keepdims=True) + jnp.exp(sink - m))
    out = jnp.einsum('bnhgqk,bnkhd->bnqhgd', probs.astype(v.dtype), vb)
    return out.reshape(B, S, SWA_Q_HEADS * HEAD_DIM)


def mlstm_chunkwise(q, k, v, i_pre, f_pre):
    B, S = q.shape[0], q.shape[1]
    L = MLSTM_CHUNK
    nc = S // L
    H = MLSTM_HEADS
    f32 = jnp.float32

    def chunks(t):
        return jnp.transpose(t.reshape(B, nc, L, H, -1), (0, 3, 1, 2, 4)).astype(f32)

    def gchunks(t):
        return jnp.transpose(t.reshape(B, nc, L, H), (0, 3, 1, 2))

    qc = chunks(q) * (MLSTM_QK_DIM ** -0.5)
    kc, vc = chunks(k), chunks(v)
    log_i = gchunks(i_pre)
    log_f = jax.nn.log_sigmoid(gchunks(f_pre))
    b = jnp.cumsum(log_f, axis=-1)
    b_last = b[..., -1]

    a = b_last[..., None] - b + log_i
    a_max = jnp.max(a, axis=-1)
    w = jnp.exp(a - a_max[..., None])
    dC = jnp.einsum('bhcl,bhcld,bhcle->bhcde', w, kc, vc)
    dn = jnp.einsum('bhcl,bhcld->bhcd', w, kc)

    def step(carry, xs):
        C, n, m = carry
        dC_c, dn_c, bl, am = xs
        m_new = jnp.maximum(bl + m, am)
        decay = jnp.exp(bl + m - m_new)
        scale = jnp.exp(am - m_new)
        C_new = decay[..., None, None] * C + scale[..., None, None] * dC_c
        n_new = decay[..., None] * n + scale[..., None] * dn_c
        return (C_new, n_new, m_new), (C, n, m)

    init = (jnp.zeros((B, H, MLSTM_QK_DIM, MLSTM_V_DIM), f32),
            jnp.zeros((B, H, MLSTM_QK_DIM), f32),
            jnp.zeros((B, H), f32))
    xs = (jnp.moveaxis(dC, 2, 0), jnp.moveaxis(dn, 2, 0),
          jnp.moveaxis(b_last, 2, 0), jnp.moveaxis(a_max, 2, 0))
    _, (C_prev, n_prev, m_prev) = lax.scan(step, init, xs)
    C_prev = jnp.moveaxis(C_prev, 0, 2)
    n_prev = jnp.moveaxis(n_prev, 0, 2)
    m_prev = jnp.moveaxis(m_prev, 0, 2)

    causal = jnp.tril(jnp.ones((L, L), dtype=bool))
    log_D = b[..., :, None] - b[..., None, :] + log_i[..., None, :]
    log_D = jnp.where(causal, log_D, -jnp.inf)
    g = b + m_prev[..., None]
    m_t = jnp.maximum(g, jnp.max(log_D, axis=-1))
    D = jnp.exp(log_D - m_t[..., None])
    inter = jnp.exp(g - m_t)
    qk = jnp.einsum('bhctd,bhcsd->bhcts', qc, kc) * D
    num = jnp.einsum('bhcts,bhcse->bhcte', qk, vc) + \
        inter[..., None] * jnp.einsum('bhctd,bhcde->bhcte', qc, C_prev)
    den = jnp.sum(qk, axis=-1) + inter * jnp.einsum('bhctd,bhcd->bhct', qc, n_prev)
    h = num / jnp.maximum(jnp.abs(den), jnp.exp(-m_t))[..., None]
    return jnp.transpose(h, (0, 2, 3, 1, 4)).reshape(B, S, H, MLSTM_V_DIM)


def swa_mlstm_mixer(xn, w_in, w_out, sinks, b_i, b_f, head_gain):
    B, S = xn.shape[0], xn.shape[1]
    proj = xn @ w_in
    q_s, k_s, v_s, q_m, k_m, v_m, o_m, i_m, f_m = split_cols(proj, EVEN_SIZES)
    attn = sliding_window_attention(q_s.reshape(B, S, SWA_Q_HEADS, HEAD_DIM),
                                    k_s.reshape(B, S, SWA_KV_HEADS, HEAD_DIM),
                                    v_s.reshape(B, S, SWA_KV_HEADS, HEAD_DIM), sinks)
    i_pre = softcap(i_m.astype(jnp.float32) + b_i.astype(jnp.float32))
    f_pre = softcap(f_m.astype(jnp.float32) + b_f.astype(jnp.float32))
    h = mlstm_chunkwise(q_m.reshape(B, S, MLSTM_HEADS, MLSTM_QK_DIM),
                        k_m.reshape(B, S, MLSTM_HEADS, MLSTM_QK_DIM),
                        v_m.reshape(B, S, MLSTM_HEADS, MLSTM_V_DIM), i_pre, f_pre)
    h = rms_norm(h, head_gain.reshape(MLSTM_HEADS, MLSTM_V_DIM))
    o = jax.nn.sigmoid(o_m.astype(jnp.float32)).reshape(B, S, MLSTM_HEADS, MLSTM_V_DIM)
    mem = (o * h).astype(xn.dtype).reshape(B, S, MLSTM_HEADS * MLSTM_V_DIM)
    return jnp.concatenate([attn.astype(xn.dtype), mem], axis=-1) @ w_out


def forgetting_attention_mixer(xn, w_in, b_f, w_out):
    B, S = xn.shape[0], xn.shape[1]
    nb = S // FOX_BLOCK
    proj = xn @ w_in
    q, k, v, f = split_cols(proj, FOX_SIZES)
    to_heads = lambda t: jnp.transpose(t.reshape(B, S, FOX_HEADS, FOX_HEAD_DIM), (0, 2, 1, 3))
    q, k, v = to_heads(q), to_heads(k), to_heads(v)
    log_f = jax.nn.log_sigmoid(f.astype(jnp.float32) + b_f.astype(jnp.float32))
    c = jnp.transpose(jnp.cumsum(log_f, axis=1), (0, 2, 1))
    q_blocks = jnp.moveaxis(q.reshape(B, FOX_HEADS, nb, FOX_BLOCK, FOX_HEAD_DIM), 2, 0)
    c_blocks = jnp.moveaxis(c.reshape(B, FOX_HEADS, nb, FOX_BLOCK), 2, 0)
    pos_blocks = jnp.arange(S).reshape(nb, FOX_BLOCK)
    k_pos = jnp.arange(S)
    scale = FOX_HEAD_DIM ** -0.5

    def one_block(args):
        qb, cb, pb = args
        s = jnp.einsum('bhqd,bhkd->bhqk', qb, k, preferred_element_type=jnp.float32) * scale
        s = s + cb[..., None] - c[:, :, None, :]
        s = jnp.where(pb[:, None] >= k_pos[None, :], s, -jnp.inf)
        p = jax.nn.softmax(s, axis=-1)
        return jnp.einsum('bhqk,bhkd->bhqd', p.astype(v.dtype), v)

    out = lax.map(one_block, (q_blocks, c_blocks, pos_blocks))
    out = jnp.transpose(out, (1, 0, 3, 2, 4)).reshape(B, S, FOX_MIX)
    return out @ w_out


def swiglu(xn, w_gate, w_up, w_down):
    return (jax.nn.silu(xn @ w_gate) * (xn @ w_up)) @ w_down


def setup_inputs(seed: int = 0) -> dict:
    key = jax.random.key(seed)
    ks = jax.random.split(key, 18)
    f32 = jnp.float32
    n_even = (DEPTH + 1) // 2
    n_odd = DEPTH // 2
    nrm = lambda k, shape, s: s * jax.random.normal(k, shape, f32)
    return {
        'x': nrm(ks[0], (BATCH, SEQ, D_MODEL), 1.0),
        'norm_mix': 1.0 + nrm(ks[1], (DEPTH, D_MODEL), 0.02),
        'norm_ffn': 1.0 + nrm(ks[2], (DEPTH, D_MODEL), 0.02),
        'norm_final': 1.0 + nrm(ks[3], (D_MODEL,), 0.02),
        'w_in_even': nrm(ks[4], (n_even, D_MODEL, EVEN_IN), D_MODEL ** -0.5),
        'w_out_even': nrm(ks[5], (n_even, EVEN_MIX, D_MODEL), EVEN_MIX ** -0.5),
        'swa_sinks': nrm(ks[6], (n_even, SWA_Q_HEADS), 0.5),
        'mlstm_b_i': nrm(ks[7], (n_even, MLSTM_HEADS), 0.1),
        'mlstm_b_f': jax.random.uniform(ks[8], (n_even, MLSTM_HEADS), f32, 3.0, 6.0),
        'mlstm_head_gain': 1.0 + nrm(ks[9], (n_even, MLSTM_HEADS * MLSTM_V_DIM), 0.02),
        'w_in_odd': nrm(ks[10], (n_odd, D_MODEL, FOX_IN), D_MODEL ** -0.5),
        'fox_b_f': jax.random.uniform(ks[11], (n_odd, FOX_HEADS), f32, 1.0, 4.0),
        'w_out_odd': nrm(ks[12], (n_odd, FOX_MIX, D_MODEL), FOX_MIX ** -0.5),
        'w_gate': nrm(ks[13], (DEPTH, D_MODEL, FFN_HIDDEN), D_MODEL ** -0.5),
        'w_up': nrm(ks[14], (DEPTH, D_MODEL, FFN_HIDDEN), D_MODEL ** -0.5),
        'w_down': nrm(ks[15], (DEPTH, FFN_HIDDEN, D_MODEL), FFN_HIDDEN ** -0.5),
    }


def reference(x, norm_mix, norm_ffn, norm_final, w_in_even, w_out_even, swa_sinks,
              mlstm_b_i, mlstm_b_f, mlstm_head_gain, w_in_odd, fox_b_f, w_out_odd,
              w_gate, w_up, w_down):
    h = x
    for layer in range(DEPTH):
        j = layer // 2
        xn = rms_norm(h, norm_mix[layer])
        if layer % 2 == 0:
            mix = swa_mlstm_mixer(xn, w_in_even[j], w_out_even[j], swa_sinks[j],
                                  mlstm_b_i[j], mlstm_b_f[j], mlstm_head_gain[j])
        else:
            mix = forgetting_attention_mixer(xn, w_in_odd[j], fox_b_f[j], w_out_odd[j])
        h = h + mix.astype(h.dtype)
        h = h + swiglu(rms_norm(h, norm_ffn[layer]), w_gate[layer], w_up[layer], w_down[layer])
    return rms_norm(h, norm_final)
```

```python
import functools

import jax
import jax.numpy as jnp
from jax import lax
from jax.experimental import pallas as pl
from jax.experimental.pallas import tpu as pltpu

F32 = jnp.float32
BF16 = jnp.bfloat16

D_MODEL = 2048
RMS_EPS = 1e-5

HEAD_DIM = 64
SWA_Q_HEADS = 16
SWA_KV_HEADS = 4
SWA_GROUP = SWA_Q_HEADS // SWA_KV_HEADS
WINDOW = 128
SWA_BLOCK = 128

MLSTM_HEADS = 4
MLSTM_V_DIM = 256
MLSTM_QK_DIM = 128
MLSTM_CHUNK = 128
GATE_SOFTCAP = 15.0

FOX_HEAD_DIM = 128
FOX_HEADS = 16

FFN_HIDDEN = 5632

EVEN_MAIN = 4608
EVEN_QS, EVEN_VM, EVEN_OM = 0, 1024, 2048
EVEN_QM, EVEN_KM, EVEN_KS, EVEN_VS = 3072, 3584, 4096, 4352
FOX_MAIN = 3 * FOX_HEADS * FOX_HEAD_DIM

LANES = 128
GATE_PAD = LANES
VMEM_LIMIT = 56 * 1024 * 1024


def _cparams(*semantics):
    return pltpu.CompilerParams(dimension_semantics=semantics,
                                vmem_limit_bytes=VMEM_LIMIT)


def _rms_norm_rows(x, gain):
    ms = jnp.mean(x * x, axis=-1, keepdims=True)
    return x * lax.rsqrt(ms + RMS_EPS) * gain


def _log_sigmoid(x):
    return jnp.minimum(x, 0.0) - jnp.log(1.0 + jnp.exp(-jnp.abs(x)))


def _sigmoid(x):
    return 1.0 / (1.0 + jnp.exp(-x))


def _split3_bf16(x):
    hi = x.astype(BF16)
    r1 = x - hi.astype(F32)
    mid = r1.astype(BF16)
    lo = (r1 - mid.astype(F32)).astype(BF16)
    return hi, mid, lo


def _dot(a, b):
    return jnp.dot(a, b, preferred_element_type=F32)


def _dot_nt(a, b):
    return lax.dot_general(a, b, (((1,), (1,)), ((), ())), preferred_element_type=F32)


def _norm_proj_kernel(x_ref, g_ref, w_ref, cs_ref, wg_ref, o_ref, og_ref, xn_ref):
    @pl.when(pl.program_id(1) == 0)
    def _():
        xn = _rms_norm_rows(x_ref[...], g_ref[...]).astype(BF16)
        xn_ref[...] = xn
        og_ref[...] = _dot(xn, wg_ref[...])

    o_ref[...] = (_dot(xn_ref[...], w_ref[...]) * cs_ref[...]).astype(o_ref.dtype)


def norm_proj(x, gain, w, col_scale, w_gate, *, tm=1024, tn=768):
    m, d = x.shape
    n = w.shape[1]
    tm = min(tm, m)
    return pl.pallas_call(
        _norm_proj_kernel,
        grid=(m // tm, n // tn),
        in_specs=[
            pl.BlockSpec((tm, d), lambda i, j: (i, 0)),
            pl.BlockSpec((1, d), lambda i, j: (0, 0)),
            pl.BlockSpec((d, tn), lambda i, j: (0, j)),
            pl.BlockSpec((1, tn), lambda i, j: (0, j)),
            pl.BlockSpec((d, GATE_PAD), lambda i, j: (0, 0)),
        ],
        out_specs=[
            pl.BlockSpec((tm, tn), lambda i, j: (i, j)),
            pl.BlockSpec((tm, GATE_PAD), lambda i, j: (i, 0)),
        ],
        out_shape=[
            jax.ShapeDtypeStruct((m, n), BF16),
            jax.ShapeDtypeStruct((m, GATE_PAD), F32),
        ],
        scratch_shapes=[pltpu.VMEM((tm, d), BF16)],
        compiler_params=_cparams("parallel", "arbitrary"),
        name="norm_proj",
    )(x, gain.reshape(1, d), w, col_scale, w_gate)


def _matmul_res_kernel(*refs, n_lhs):
    a_refs = refs[:n_lhs]
    w_refs = refs[n_lhs:2 * n_lhs]
    r_ref, o_ref = refs[2 * n_lhs], refs[2 * n_lhs + 1]
    acc = r_ref[...]
    for a_ref, w_ref in zip(a_refs, w_refs):
        acc = acc + _dot(a_ref[...], w_ref[...])
    o_ref[...] = acc


def matmul_residual(lhs_list, w_list, res, *, tm=1024, tn=512):
    m, n = res.shape
    tm = min(tm, m)
    n_lhs = len(lhs_list)
    in_specs = []
    for a in lhs_list:
        in_specs.append(pl.BlockSpec((tm, a.shape[1]), lambda i, j: (i, 0)))
    for w in w_list:
        in_specs.append(pl.BlockSpec((w.shape[0], tn), lambda i, j: (0, j)))
    in_specs.append(pl.BlockSpec((tm, tn), lambda i, j: (i, j)))
    return pl.pallas_call(
        functools.partial(_matmul_res_kernel, n_lhs=n_lhs),
        grid=(m // tm, n // tn),
        in_specs=in_specs,
        out_specs=pl.BlockSpec((tm, tn), lambda i, j: (i, j)),
        out_shape=jax.ShapeDtypeStruct((m, n), F32),
        compiler_params=_cparams("parallel", "arbitrary"),
        name="matmul_residual",
    )(*lhs_list, *w_list, res)


def _ffn_up_kernel(x_ref, g_ref, wg_ref, wu_ref, o_ref, xn_ref):
    @pl.when(pl.program_id(1) == 0)
    def _():
        xn_ref[...] = _rms_norm_rows(x_ref[...], g_ref[...]).astype(BF16)

    xn = xn_ref[...]
    gate = _dot(xn, wg_ref[...])
    up = _dot(xn, wu_ref[...])
    o_ref[...] = (gate * _sigmoid(gate) * up).astype(o_ref.dtype)


def ffn_up(x, gain, w_gate, w_up, *, tm=1024, tn=512):
    m, d = x.shape
    n = w_gate.shape[1]
    tm = min(tm, m)
    return pl.pallas_call(
        _ffn_up_kernel,
        grid=(m // tm, n // tn),
        in_specs=[
            pl.BlockSpec((tm, d), lambda i, j: (i, 0)),
            pl.BlockSpec((1, d), lambda i, j: (0, 0)),
            pl.BlockSpec((d, tn), lambda i, j: (0, j)),
            pl.BlockSpec((d, tn), lambda i, j: (0, j)),
        ],
        out_specs=pl.BlockSpec((tm, tn), lambda i, j: (i, j)),
        out_shape=jax.ShapeDtypeStruct((m, n), BF16),
        scratch_shapes=[pltpu.VMEM((tm, d), BF16)],
        compiler_params=_cparams("parallel", "arbitrary"),
        name="ffn_up",
    )(x, gain.reshape(1, d), w_gate, w_up)


def _final_norm_kernel(x_ref, g_ref, o_ref):
    o_ref[...] = _rms_norm_rows(x_ref[...], g_ref[...])


def final_norm(x, gain, *, tm=1024):
    m, d = x.shape
    tm = min(tm, m)
    return pl.pallas_call(
        _final_norm_kernel,
        grid=(m // tm,),
        in_specs=[pl.BlockSpec((tm, d), lambda i: (i, 0)),
                  pl.BlockSpec((1, d), lambda i: (0, 0))],
        out_specs=pl.BlockSpec((tm, d), lambda i: (i, 0)),
        out_shape=jax.ShapeDtypeStruct((m, d), F32),
        compiler_params=_cparams("parallel"),
        name="final_norm",
    )(x, gain.reshape(1, d))


def _swa_kernel(sinks_ref, q_ref, kc_ref, kp_ref, vc_ref, vp_ref, o_ref):
    blk = SWA_BLOCK
    k = jnp.concatenate([kp_ref[...], kc_ref[...]], axis=0)
    v = jnp.concatenate([vp_ref[...], vc_ref[...]], axis=0)
    qi = lax.broadcasted_iota(jnp.int32, (blk, 2 * blk), 0) + blk
    kj = lax.broadcasted_iota(jnp.int32, (blk, 2 * blk), 1)
    dist = qi - kj
    first_key = jnp.where(pl.program_id(1) > 0, 0, blk)
    valid = (dist >= 0) & (dist < WINDOW) & (kj >= first_key)
    distf = dist.astype(F32)
    for h in range(SWA_Q_HEADS):
        hk = h // SWA_GROUP
        slope = 2.0 ** (-8.0 * (h + 1) / SWA_Q_HEADS)
        qh = q_ref[:, h * HEAD_DIM:(h + 1) * HEAD_DIM]
        kh = k[:, hk * HEAD_DIM:(hk + 1) * HEAD_DIM]
        vh = v[:, hk * HEAD_DIM:(hk + 1) * HEAD_DIM]
        s = _dot_nt(qh, kh) - slope * distf
        s = jnp.where(valid, s, -jnp.inf)
        sink = sinks_ref[h]
        m = jnp.maximum(jnp.max(s, axis=-1, keepdims=True), sink)
        p = jnp.exp(s - m)
        den = jnp.sum(p, axis=-1, keepdims=True) + jnp.exp(sink - m)
        o = _dot(p.astype(BF16), vh) / den
        o_ref[:, h * HEAD_DIM:(h + 1) * HEAD_DIM] = o.astype(o_ref.dtype)


def swa_attention(proj, sinks, batch, seq):
    blk = SWA_BLOCK
    nb = seq // blk
    m = batch * seq
    qw = SWA_Q_HEADS * HEAD_DIM
    kw = SWA_KV_HEADS * HEAD_DIM
    cur = lambda b, n: b * nb + n
    prev = lambda b, n: b * nb + jnp.maximum(n - 1, 0)
    return pl.pallas_call(
        _swa_kernel,
        grid=(batch, nb),
        in_specs=[
            pl.BlockSpec(memory_space=pltpu.SMEM),
            pl.BlockSpec((blk, qw), lambda b, n: (cur(b, n), EVEN_QS // qw)),
            pl.BlockSpec((blk, kw), lambda b, n: (cur(b, n), EVEN_KS // kw)),
            pl.BlockSpec((blk, kw), lambda b, n: (prev(b, n), EVEN_KS // kw)),
            pl.BlockSpec((blk, kw), lambda b, n: (cur(b, n), EVEN_VS // kw)),
            pl.BlockSpec((blk, kw), lambda b, n: (prev(b, n), EVEN_VS // kw)),
        ],
        out_specs=pl.BlockSpec((blk, qw), lambda b, n: (cur(b, n), 0)),
        out_shape=jax.ShapeDtypeStruct((m, qw), BF16),
        compiler_params=_cparams("parallel", "arbitrary"),
        name="swa_attention",
    )(sinks, proj, proj, proj, proj, proj)


def _mlstm_kernel(g_ref, bias_ref, gain_ref, q_ref, k_ref, v_ref, og_ref, o_ref,
                  c_ref, m_ref):
    L = MLSTM_CHUNK
    dk, dv = MLSTM_QK_DIM, MLSTM_V_DIM
    nh = MLSTM_HEADS

    @pl.when(pl.program_id(1) == 0)
    def _():
        c_ref[...] = jnp.zeros_like(c_ref)
        m_ref[...] = jnp.zeros_like(m_ref)

    a_all = GATE_SOFTCAP * jnp.tanh((g_ref[...] + bias_ref[...]) / GATE_SOFTCAP)
    lf_all = _log_sigmoid(a_all)
    row = lax.broadcasted_iota(jnp.int32, (L, L), 0)
    col = lax.broadcasted_iota(jnp.int32, (L, L), 1)
    causal = row >= col
    tri = jnp.where(causal, 1.0, 0.0).astype(BF16)
    hi, mid, lo = _split3_bf16(lf_all)
    b_all = _dot(tri, hi) + _dot(tri, mid) + _dot(tri, lo)
    a_all_t = a_all.T
    b_all_t = b_all.T
    ones = jnp.ones((L, LANES), BF16)

    for h in range(nh):
        li_c = a_all[:, h:h + 1]
        li_r = a_all_t[h:h + 1, :]
        b_c = b_all[:, nh + h:nh + h + 1]
        b_r = b_all_t[nh + h:nh + h + 1, :]
        b_last = b_r[:, L - 1:L]
        m_prev = m_ref[h:h + 1, 0:1]

        a_c = b_last - b_c + li_c
        a_max = jnp.max(a_c, axis=0, keepdims=True)
        w_c = jnp.exp(a_c - a_max)

        log_d = jnp.where(causal, b_c - b_r + li_r, -jnp.inf)
        g_c = b_c + m_prev
        m_t = jnp.maximum(g_c, jnp.max(log_d, axis=-1, keepdims=True))
        d_mat = jnp.exp(log_d - m_t)
        inter = jnp.exp(g_c - m_t)

        q = q_ref[:, h * dk:(h + 1) * dk]
        k = k_ref[:, h * dk:(h + 1) * dk]
        v_aug = jnp.concatenate([v_ref[:, h * dv:(h + 1) * dv], ones], axis=1)
        p = (_dot_nt(q, k) * d_mat).astype(BF16)
        c_prev = c_ref[h]
        num = _dot(p, v_aug) + inter * _dot(q, c_prev.astype(BF16))
        den = num[:, dv:dv + 1]
        hh = num[:, :dv] / jnp.maximum(jnp.abs(den), jnp.exp(-m_t))
        hn = _rms_norm_rows(hh, gain_ref[:, h * dv:(h + 1) * dv])
        gate = _sigmoid(og_ref[:, h * dv:(h + 1) * dv].astype(F32))
        o_ref[:, h * dv:(h + 1) * dv] = (gate * hn).astype(o_ref.dtype)

        m_new = jnp.maximum(b_last + m_prev, a_max)
        decay = jnp.exp(b_last + m_prev - m_new)
        scl = jnp.exp(a_max - m_new)
        kw_t = (k.astype(F32) * w_c).T.astype(BF16)
        c_ref[h] = decay * c_prev + scl * _dot(kw_t, v_aug)
        m_ref[h:h + 1, :] = jnp.broadcast_to(m_new, (1, LANES))


def mlstm(proj, gates, bias_row, head_gain, batch, seq):
    L = MLSTM_CHUNK
    nc = seq // L
    m = batch * seq
    nh, dk, dv = MLSTM_HEADS, MLSTM_QK_DIM, MLSTM_V_DIM
    rows = lambda b, c: b * nc + c
    return pl.pallas_call(
        _mlstm_kernel,
        grid=(batch, nc),
        in_specs=[
            pl.BlockSpec((L, GATE_PAD), lambda b, c: (rows(b, c), 0)),
            pl.BlockSpec((1, GATE_PAD), lambda b, c: (0, 0)),
            pl.BlockSpec((1, nh * dv), lambda b, c: (0, 0)),
            pl.BlockSpec((L, nh * dk), lambda b, c: (rows(b, c), EVEN_QM // (nh * dk))),
            pl.BlockSpec((L, nh * dk), lambda b, c: (rows(b, c), EVEN_KM // (nh * dk))),
            pl.BlockSpec((L, nh * dv), lambda b, c: (rows(b, c), EVEN_VM // (nh * dv))),
            pl.BlockSpec((L, nh * dv), lambda b, c: (rows(b, c), EVEN_OM // (nh * dv))),
        ],
        out_specs=pl.BlockSpec((L, nh * dv), lambda b, c: (rows(b, c), 0)),
        out_shape=jax.ShapeDtypeStruct((m, nh * dv), BF16),
        scratch_shapes=[pltpu.VMEM((nh, dk, dv + LANES), F32),
                        pltpu.VMEM((8, LANES), F32)],
        compiler_params=_cparams("parallel", "arbitrary"),
        name="mlstm",
    )(gates, bias_row, head_gain.reshape(1, nh * dv), proj, proj, proj, proj)


def _fox_cumsum_kernel(g_ref, bias_ref, c_ref, carry_ref, *, ts):
    @pl.when(pl.program_id(1) == 0)
    def _():
        carry_ref[...] = jnp.zeros_like(carry_ref)

    lf_t = _log_sigmoid(g_ref[...] + bias_ref[...]).T
    row = lax.broadcasted_iota(jnp.int32, (ts, ts), 0)
    col = lax.broadcasted_iota(jnp.int32, (ts, ts), 1)
    upper = jnp.where(row <= col, 1.0, 0.0).astype(BF16)
    hi, mid, lo = _split3_bf16(lf_t)
    c = _dot(hi, upper) + _dot(mid, upper) + _dot(lo, upper) + carry_ref[:, 0:1]
    c_ref[0] = c
    carry_ref[...] = jnp.broadcast_to(c[:, ts - 1:ts], carry_ref.shape)


def fox_cumsum(gates, bias_row, batch, seq, *, ts=512):
    ns = seq // ts
    return pl.pallas_call(
        functools.partial(_fox_cumsum_kernel, ts=ts),
        grid=(batch, ns),
        in_specs=[pl.BlockSpec((ts, GATE_PAD), lambda b, s: (b * ns + s, 0)),
                  pl.BlockSpec((1, GATE_PAD), lambda b, s: (0, 0))],
        out_specs=pl.BlockSpec((1, GATE_PAD, ts), lambda b, s: (b, 0, s)),
        out_shape=jax.ShapeDtypeStruct((batch, GATE_PAD, seq), F32),
        scratch_shapes=[pltpu.VMEM((GATE_PAD, LANES), F32)],
        compiler_params=_cparams("parallel", "arbitrary"),
        name="fox_cumsum",
    )(gates, bias_row)


def _fox_attn_kernel(q_ref, k_ref, v_ref, c_ref, o_ref, m_ref, l_ref, acc_ref, *, blk):
    qi = pl.program_id(2)
    q = q_ref[...]
    cq_row = c_ref[0, pl.ds(qi, 1), :]
    cq = jnp.broadcast_to(cq_row, (LANES, blk)).T[:, 0:1]

    m_ref[...] = jnp.full_like(m_ref, -jnp.inf)
    l_ref[...] = jnp.zeros_like(l_ref)
    acc_ref[...] = jnp.zeros_like(acc_ref)

    def tile(kb, masked):
        start = pl.multiple_of(kb * blk, blk)
        k = k_ref[pl.ds(start, blk), :]
        v = v_ref[pl.ds(start, blk), :]
        s = _dot_nt(q, k) + cq - c_ref[0, pl.ds(kb, 1), :]
        if masked:
            row = lax.broadcasted_iota(jnp.int32, (blk, blk), 0)
            col = lax.broadcasted_iota(jnp.int32, (blk, blk), 1)
            s = jnp.where(row >= col, s, -jnp.inf)
        m_old = m_ref[...]
        m_new = jnp.maximum(m_old, jnp.max(s, axis=-1, keepdims=True))
        alpha = jnp.exp(m_old - m_new)
        p = jnp.exp(s - m_new)
        l_ref[...] = alpha * l_ref[...] + jnp.sum(p, axis=-1, keepdims=True)
        acc_ref[...] = alpha * acc_ref[...] + _dot(p.astype(BF16), v)
        m_ref[...] = m_new

    def body(kb, carry):
        tile(kb, False)
        return carry

    lax.fori_loop(0, qi, body, 0)
    tile(qi, True)
    o_ref[...] = (acc_ref[...] / l_ref[...]).astype(o_ref.dtype)


def fox_attention(proj, c_rows, batch, seq, *, blk=512):
    nq = seq // blk
    m = batch * seq
    nh, dh = FOX_HEADS, FOX_HEAD_DIM
    c_blocks = c_rows.reshape(batch * GATE_PAD, nq, blk)
    return pl.pallas_call(
        functools.partial(_fox_attn_kernel, blk=blk),
        grid=(batch, nh, nq),
        in_specs=[
            pl.BlockSpec((blk, dh), lambda b, h, i: (b * nq + i, h)),
            pl.BlockSpec((seq, dh), lambda b, h, i: (b, nh + h)),
            pl.BlockSpec((seq, dh), lambda b, h, i: (b, 2 * nh + h)),
            pl.BlockSpec((1, nq, blk), lambda b, h, i: (b * GATE_PAD + h, 0, 0)),
        ],
        out_specs=pl.BlockSpec((blk, dh), lambda b, h, i: (b * nq + i, h)),
        out_shape=jax.ShapeDtypeStruct((m, nh * dh), BF16),
        scratch_shapes=[pltpu.VMEM((blk, 1), F32), pltpu.VMEM((blk, 1), F32),
                        pltpu.VMEM((blk, dh), F32)],
        compiler_params=_cparams("parallel", "parallel", "arbitrary"),
        name="fox_attention",
    )(proj, proj, proj, c_blocks)


def _pad_cols(t, width):
    return jnp.pad(t, ((0, 0), (0, width - t.shape[1])))


def _even_col_scale():
    cs = jnp.ones((EVEN_MAIN,), F32)
    cs = cs.at[EVEN_QS:EVEN_QS + SWA_Q_HEADS * HEAD_DIM].set(HEAD_DIM ** -0.5)
    cs = cs.at[EVEN_QM:EVEN_QM + MLSTM_HEADS * MLSTM_QK_DIM].set(MLSTM_QK_DIM ** -0.5)
    return cs.reshape(1, EVEN_MAIN)


def _fox_col_scale():
    cs = jnp.ones((FOX_MAIN,), F32)
    cs = cs.at[:FOX_HEADS * FOX_HEAD_DIM].set(FOX_HEAD_DIM ** -0.5)
    return cs.reshape(1, FOX_MAIN)


def _even_mixer(h, gain, w_in, w_out, sinks, b_i, b_f, head_gain, batch, seq):
    q_s, k_s, v_s, q_m, k_m, v_m, o_m, gate_cols = jnp.split(
        w_in, [1024, 1280, 1536, 2048, 2560, 3584, 4608], axis=1)
    w_main = jnp.concatenate([q_s, v_m, o_m, q_m, k_m, k_s, v_s], axis=1).astype(BF16)
    w_gate = _pad_cols(gate_cols, GATE_PAD).astype(BF16)
    proj, gates = norm_proj(h, gain, w_main, _even_col_scale(), w_gate, tn=768)
    attn = swa_attention(proj, sinks, batch, seq)
    bias_row = _pad_cols(jnp.concatenate([b_i, b_f]).reshape(1, -1), GATE_PAD)
    mem = mlstm(proj, gates, bias_row, head_gain, batch, seq)
    w_out = w_out.astype(BF16)
    split = SWA_Q_HEADS * HEAD_DIM
    return matmul_residual([attn, mem], [w_out[:split], w_out[split:]], h)


def _fox_mixer(h, gain, w_in, b_f, w_out, batch, seq):
    w_main = w_in[:, :FOX_MAIN].astype(BF16)
    w_gate = _pad_cols(w_in[:, FOX_MAIN:], GATE_PAD).astype(BF16)
    proj, gates = norm_proj(h, gain, w_main, _fox_col_scale(), w_gate, tn=768)
    c_rows = fox_cumsum(gates, _pad_cols(b_f.reshape(1, -1), GATE_PAD), batch, seq)
    attn = fox_attention(proj, c_rows, batch, seq)
    return matmul_residual([attn], [w_out.astype(BF16)], h)


def kernel(x, norm_mix, norm_ffn, norm_final, w_in_even, w_out_even, swa_sinks, mlstm_b_i, mlstm_b_f, mlstm_head_gain, w_in_odd, fox_b_f, w_out_odd, w_gate, w_up, w_down):
    batch, seq, d = x.shape
    depth = norm_mix.shape[0]
    h = x.reshape(batch * seq, d)
    for layer in range(depth):
        j = layer // 2
        if layer % 2 == 0:
            h = _even_mixer(h, norm_mix[layer], w_in_even[j], w_out_even[j], swa_sinks[j],
                            mlstm_b_i[j], mlstm_b_f[j], mlstm_head_gain[j], batch, seq)
        else:
            h = _fox_mixer(h, norm_mix[layer], w_in_odd[j], fox_b_f[j], w_out_odd[j], batch, seq)
        act = ffn_up(h, norm_ffn[layer], w_gate[layer].astype(BF16), w_up[layer].astype(BF16))
        h = matmul_residual([act], [w_down[layer].astype(BF16)], h)
    return final_norm(h, norm_final).reshape(batch, seq, d)
```

```python
import functools

import jax
import jax.numpy as jnp
import numpy as np
from jax import lax
from jax.experimental import pallas as pl
from jax.experimental.pallas import tpu as pltpu

F32 = jnp.float32
BF16 = jnp.bfloat16

D_MODEL = 2048
RMS_EPS = 1e-5

HEAD_DIM = 64
SWA_Q_HEADS = 16
SWA_KV_HEADS = 4
SWA_GROUP = SWA_Q_HEADS // SWA_KV_HEADS
WINDOW = 128
SWA_BLOCK = 128

MLSTM_HEADS = 4
MLSTM_V_DIM = 256
MLSTM_QK_DIM = 128
MLSTM_CHUNK = 128
GATE_SOFTCAP = 15.0

FOX_HEAD_DIM = 128
FOX_HEADS = 16

FFN_HIDDEN = 5632

EVEN_MAIN = 4608
EVEN_QS, EVEN_VM, EVEN_OM = 0, 1024, 2048
EVEN_QM, EVEN_KM, EVEN_KS, EVEN_VS = 3072, 3584, 4096, 4352
FOX_MAIN = 3 * FOX_HEADS * FOX_HEAD_DIM

LANES = 128
GATE_PAD = LANES
VMEM_LIMIT = 56 * 1024 * 1024


def _cparams(*semantics):
    return pltpu.CompilerParams(dimension_semantics=semantics,
                                vmem_limit_bytes=VMEM_LIMIT)


def _rms_norm_rows(x, gain):
    ms = jnp.mean(x * x, axis=-1, keepdims=True)
    return x * lax.rsqrt(ms + RMS_EPS) * gain


def _log_sigmoid(x):
    return jnp.minimum(x, 0.0) - jnp.log(1.0 + jnp.exp(-jnp.abs(x)))


def _sigmoid(x):
    return 1.0 / (1.0 + jnp.exp(-x))


def _split3_bf16(x):
    hi = x.astype(BF16)
    r1 = x - hi.astype(F32)
    mid = r1.astype(BF16)
    lo = (r1 - mid.astype(F32)).astype(BF16)
    return hi, mid, lo


def _dot(a, b):
    return jnp.dot(a, b, preferred_element_type=F32)


def _dot_nt(a, b):
    return lax.dot_general(a, b, (((1,), (1,)), ((), ())), preferred_element_type=F32)


def _norm_proj_kernel(x_ref, g_ref, w_ref, cs_ref, wg_ref, o_ref, og_ref, xn_ref):
    @pl.when(pl.program_id(1) == 0)
    def _():
        xn = _rms_norm_rows(x_ref[...], g_ref[...]).astype(BF16)
        xn_ref[...] = xn
        og_ref[...] = _dot(xn, wg_ref[...])

    o_ref[...] = (_dot(xn_ref[...], w_ref[...]) * cs_ref[...]).astype(o_ref.dtype)


def norm_proj(x, gain, w, col_scale, w_gate, *, tm=1024, tn=768):
    m, d = x.shape
    n = w.shape[1]
    tm = min(tm, m)
    return pl.pallas_call(
        _norm_proj_kernel,
        grid=(m // tm, n // tn),
        in_specs=[
            pl.BlockSpec((tm, d), lambda i, j: (i, 0)),
            pl.BlockSpec((1, d), lambda i, j: (0, 0)),
            pl.BlockSpec((d, tn), lambda i, j: (0, j)),
            pl.BlockSpec((1, tn), lambda i, j: (0, j)),
            pl.BlockSpec((d, GATE_PAD), lambda i, j: (0, 0)),
        ],
        out_specs=[
            pl.BlockSpec((tm, tn), lambda i, j: (i, j)),
            pl.BlockSpec((tm, GATE_PAD), lambda i, j: (i, 0)),
        ],
        out_shape=[
            jax.ShapeDtypeStruct((m, n), BF16),
            jax.ShapeDtypeStruct((m, GATE_PAD), F32),
        ],
        scratch_shapes=[pltpu.VMEM((tm, d), BF16)],
        compiler_params=_cparams("parallel", "arbitrary"),
        name="norm_proj",
    )(x, gain.reshape(1, d), w, col_scale, w_gate)


def _matmul_res_kernel(*refs, n_lhs):
    a_refs = refs[:n_lhs]
    w_refs = refs[n_lhs:2 * n_lhs]
    r_ref, o_ref = refs[2 * n_lhs], refs[2 * n_lhs + 1]
    acc = r_ref[...]
    for a_ref, w_ref in zip(a_refs, w_refs):
        acc = acc + _dot(a_ref[...], w_ref[...])
    o_ref[...] = acc


def matmul_residual(lhs_list, w_list, res, *, tm=1024, tn=512):
    m, n = res.shape
    tm = min(tm, m)
    n_lhs = len(lhs_list)
    in_specs = []
    for a in lhs_list:
        in_specs.append(pl.BlockSpec((tm, a.shape[1]), lambda i, j: (i, 0)))
    for w in w_list:
        in_specs.append(pl.BlockSpec((w.shape[0], tn), lambda i, j: (0, j)))
    in_specs.append(pl.BlockSpec((tm, tn), lambda i, j: (i, j)))
    return pl.pallas_call(
        functools.partial(_matmul_res_kernel, n_lhs=n_lhs),
        grid=(m // tm, n // tn),
        in_specs=in_specs,
        out_specs=pl.BlockSpec((tm, tn), lambda i, j: (i, j)),
        out_shape=jax.ShapeDtypeStruct((m, n), F32),
        compiler_params=_cparams("parallel", "arbitrary"),
        name="matmul_residual",
    )(*lhs_list, *w_list, res)


def _ffn_up_kernel(x_ref, g_ref, wg_ref, wu_ref, o_ref, xn_ref):
    @pl.when(pl.program_id(1) == 0)
    def _():
        xn_ref[...] = _rms_norm_rows(x_ref[...], g_ref[...]).astype(BF16)

    xn = xn_ref[...]
    gate = _dot(xn, wg_ref[...])
    up = _dot(xn, wu_ref[...])
    o_ref[...] = (gate * _sigmoid(gate) * up).astype(o_ref.dtype)


def ffn_up(x, gain, w_gate, w_up, *, tm=1024, tn=512):
    m, d = x.shape
    n = w_gate.shape[1]
    tm = min(tm, m)
    return pl.pallas_call(
        _ffn_up_kernel,
        grid=(m // tm, n // tn),
        in_specs=[
            pl.BlockSpec((tm, d), lambda i, j: (i, 0)),
            pl.BlockSpec((1, d), lambda i, j: (0, 0)),
            pl.BlockSpec((d, tn), lambda i, j: (0, j)),
            pl.BlockSpec((d, tn), lambda i, j: (0, j)),
        ],
        out_specs=pl.BlockSpec((tm, tn), lambda i, j: (i, j)),
        out_shape=jax.ShapeDtypeStruct((m, n), BF16),
        scratch_shapes=[pltpu.VMEM((tm, d), BF16)],
        compiler_params=_cparams("parallel", "arbitrary"),
        name="ffn_up",
    )(x, gain.reshape(1, d), w_gate, w_up)


def _final_norm_kernel(x_ref, g_ref, o_ref):
    o_ref[...] = _rms_norm_rows(x_ref[...], g_ref[...])


def final_norm(x, gain, *, tm=1024):
    m, d = x.shape
    tm = min(tm, m)
    return pl.pallas_call(
        _final_norm_kernel,
        grid=(m // tm,),
        in_specs=[pl.BlockSpec((tm, d), lambda i: (i, 0)),
                  pl.BlockSpec((1, d), lambda i: (0, 0))],
        out_specs=pl.BlockSpec((tm, d), lambda i: (i, 0)),
        out_shape=jax.ShapeDtypeStruct((m, d), F32),
        compiler_params=_cparams("parallel"),
        name="final_norm",
    )(x, gain.reshape(1, d))


def _swa_kernel(sinks_ref, q_ref, kc_ref, kp_ref, vc_ref, vp_ref, o_ref):
    blk = SWA_BLOCK
    k = jnp.concatenate([kp_ref[...], kc_ref[...]], axis=0)
    v = jnp.concatenate([vp_ref[...], vc_ref[...]], axis=0)
    qi = lax.broadcasted_iota(jnp.int32, (blk, 2 * blk), 0) + blk
    kj = lax.broadcasted_iota(jnp.int32, (blk, 2 * blk), 1)
    dist = qi - kj
    first_key = jnp.where(pl.program_id(1) > 0, 0, blk)
    valid = (dist >= 0) & (dist < WINDOW) & (kj >= first_key)
    distf = dist.astype(F32)
    for h in range(SWA_Q_HEADS):
        hk = h // SWA_GROUP
        slope = 2.0 ** (-8.0 * (h + 1) / SWA_Q_HEADS)
        qh = q_ref[:, h * HEAD_DIM:(h + 1) * HEAD_DIM]
        kh = k[:, hk * HEAD_DIM:(hk + 1) * HEAD_DIM]
        vh = v[:, hk * HEAD_DIM:(hk + 1) * HEAD_DIM]
        s = _dot_nt(qh, kh) - slope * distf
        s = jnp.where(valid, s, -jnp.inf)
        sink = sinks_ref[h]
        m = jnp.maximum(jnp.max(s, axis=-1, keepdims=True), sink)
        p = jnp.exp(s - m)
        den = jnp.sum(p, axis=-1, keepdims=True) + jnp.exp(sink - m)
        o = _dot(p.astype(BF16), vh) / den
        o_ref[:, h * HEAD_DIM:(h + 1) * HEAD_DIM] = o.astype(o_ref.dtype)


def swa_attention(proj, sinks, batch, seq):
    blk = SWA_BLOCK
    nb = seq // blk
    m = batch * seq
    qw = SWA_Q_HEADS * HEAD_DIM
    kw = SWA_KV_HEADS * HEAD_DIM
    cur = lambda b, n: b * nb + n
    prev = lambda b, n: b * nb + jnp.maximum(n - 1, 0)
    return pl.pallas_call(
        _swa_kernel,
        grid=(batch, nb),
        in_specs=[
            pl.BlockSpec(memory_space=pltpu.SMEM),
            pl.BlockSpec((blk, qw), lambda b, n: (cur(b, n), EVEN_QS // qw)),
            pl.BlockSpec((blk, kw), lambda b, n: (cur(b, n), EVEN_KS // kw)),
            pl.BlockSpec((blk, kw), lambda b, n: (prev(b, n), EVEN_KS // kw)),
            pl.BlockSpec((blk, kw), lambda b, n: (cur(b, n), EVEN_VS // kw)),
            pl.BlockSpec((blk, kw), lambda b, n: (prev(b, n), EVEN_VS // kw)),
        ],
        out_specs=pl.BlockSpec((blk, qw), lambda b, n: (cur(b, n), 0)),
        out_shape=jax.ShapeDtypeStruct((m, qw), BF16),
        compiler_params=_cparams("parallel", "arbitrary"),
        name="swa_attention",
    )(sinks, proj, proj, proj, proj, proj)


def _mlstm_kernel(g_ref, bias_ref, gain_ref, q_ref, k_ref, v_ref, og_ref, o_ref,
                  c_ref, m_ref):
    L = MLSTM_CHUNK
    dk, dv = MLSTM_QK_DIM, MLSTM_V_DIM
    nh = MLSTM_HEADS

    @pl.when(pl.program_id(1) == 0)
    def _():
        c_ref[...] = jnp.zeros_like(c_ref)
        m_ref[...] = jnp.zeros_like(m_ref)

    a_all = GATE_SOFTCAP * jnp.tanh((g_ref[...] + bias_ref[...]) / GATE_SOFTCAP)
    lf_all = _log_sigmoid(a_all)
    row = lax.broadcasted_iota(jnp.int32, (L, L), 0)
    col = lax.broadcasted_iota(jnp.int32, (L, L), 1)
    causal = row >= col
    tri = jnp.where(causal, 1.0, 0.0).astype(BF16)
    hi, mid, lo = _split3_bf16(lf_all)
    b_all = _dot(tri, hi) + _dot(tri, mid) + _dot(tri, lo)
    a_all_t = a_all.T
    b_all_t = b_all.T
    ones = jnp.ones((L, LANES), BF16)

    for h in range(nh):
        li_c = a_all[:, h:h + 1]
        li_r = a_all_t[h:h + 1, :]
        b_c = b_all[:, nh + h:nh + h + 1]
        b_r = b_all_t[nh + h:nh + h + 1, :]
        b_last = b_r[:, L - 1:L]
        m_prev = m_ref[h:h + 1, 0:1]

        a_c = b_last - b_c + li_c
        a_max = jnp.max(a_c, axis=0, keepdims=True)
        w_c = jnp.exp(a_c - a_max)

        log_d = jnp.where(causal, b_c - b_r + li_r, -jnp.inf)
        g_c = b_c + m_prev
        m_t = jnp.maximum(g_c, jnp.max(log_d, axis=-1, keepdims=True))
        d_mat = jnp.exp(log_d - m_t)
        inter = jnp.exp(g_c - m_t)

        q = q_ref[:, h * dk:(h + 1) * dk]
        k = k_ref[:, h * dk:(h + 1) * dk]
        v_aug = jnp.concatenate([v_ref[:, h * dv:(h + 1) * dv], ones], axis=1)
        p = (_dot_nt(q, k) * d_mat).astype(BF16)
        c_prev = c_ref[h]
        num = _dot(p, v_aug) + inter * _dot(q, c_prev.astype(BF16))
        den = num[:, dv:dv + 1]
        hh = num[:, :dv] / jnp.maximum(jnp.abs(den), jnp.exp(-m_t))
        hn = _rms_norm_rows(hh, gain_ref[:, h * dv:(h + 1) * dv])
        gate = _sigmoid(og_ref[:, h * dv:(h + 1) * dv].astype(F32))
        o_ref[:, h * dv:(h + 1) * dv] = (gate * hn).astype(o_ref.dtype)

        m_new = jnp.maximum(b_last + m_prev, a_max)
        decay = jnp.exp(b_last + m_prev - m_new)
        scl = jnp.exp(a_max - m_new)
        kw_t = (k.astype(F32) * w_c).T.astype(BF16)
        c_ref[h] = decay * c_prev + scl * _dot(kw_t, v_aug)
        m_ref[h:h + 1, :] = jnp.broadcast_to(m_new, (1, LANES))


def mlstm(proj, gates, bias_row, head_gain, batch, seq):
    L = MLSTM_CHUNK
    nc = seq // L
    m = batch * seq
    nh, dk, dv = MLSTM_HEADS, MLSTM_QK_DIM, MLSTM_V_DIM
    rows = lambda b, c: b * nc + c
    return pl.pallas_call(
        _mlstm_kernel,
        grid=(batch, nc),
        in_specs=[
            pl.BlockSpec((L, GATE_PAD), lambda b, c: (rows(b, c), 0)),
            pl.BlockSpec((1, GATE_PAD), lambda b, c: (0, 0)),
            pl.BlockSpec((1, nh * dv), lambda b, c: (0, 0)),
            pl.BlockSpec((L, nh * dk), lambda b, c: (rows(b, c), EVEN_QM // (nh * dk))),
            pl.BlockSpec((L, nh * dk), lambda b, c: (rows(b, c), EVEN_KM // (nh * dk))),
            pl.BlockSpec((L, nh * dv), lambda b, c: (rows(b, c), EVEN_VM // (nh * dv))),
            pl.BlockSpec((L, nh * dv), lambda b, c: (rows(b, c), EVEN_OM // (nh * dv))),
        ],
        out_specs=pl.BlockSpec((L, nh * dv), lambda b, c: (rows(b, c), 0)),
        out_shape=jax.ShapeDtypeStruct((m, nh * dv), BF16),
        scratch_shapes=[pltpu.VMEM((nh, dk, dv + LANES), F32),
                        pltpu.VMEM((8, LANES), F32)],
        compiler_params=_cparams("parallel", "arbitrary"),
        name="mlstm",
    )(gates, bias_row, head_gain.reshape(1, nh * dv), proj, proj, proj, proj)


LOG2E = 1.4426950408889634
BIAS_COLS = 3


def _fox_selectors():
    width = FOX_HEADS * LANES
    sel_k = np.zeros((BIAS_COLS * GATE_PAD, width), np.float32)
    sel_q = np.zeros((BIAS_COLS * GATE_PAD, width), np.float32)
    const_k = np.zeros((1, width), np.float32)
    const_q = np.zeros((1, width), np.float32)
    for h in range(FOX_HEADS):
        for part in range(BIAS_COLS):
            sel_k[part * GATE_PAD + h, h * LANES + part] = 1.0
            sel_q[part * GATE_PAD + h, h * LANES + BIAS_COLS + part] = 1.0
            const_k[0, h * LANES + BIAS_COLS + part] = 1.0
            const_q[0, h * LANES + part] = -1.0
    return (jnp.asarray(sel_k, BF16), jnp.asarray(sel_q, BF16),
            jnp.asarray(const_k), jnp.asarray(const_q))


def _fox_bias_kernel(g_ref, bias_ref, selk_ref, selq_ref, ck_ref, cq_ref, kb_ref, qb_ref,
                     carry_ref, *, ts):
    @pl.when(pl.program_id(1) == 0)
    def _():
        carry_ref[...] = jnp.zeros_like(carry_ref)

    lf = _log_sigmoid(g_ref[...] + bias_ref[...])
    row = lax.broadcasted_iota(jnp.int32, (ts, ts), 0)
    col = lax.broadcasted_iota(jnp.int32, (ts, ts), 1)
    tri = jnp.where(row >= col, 1.0, 0.0).astype(BF16)
    hi, mid, lo = _split3_bf16(lf)
    c = _dot(tri, hi) + _dot(tri, mid) + _dot(tri, lo) + carry_ref[0:1, :]
    carry_ref[...] = jnp.broadcast_to(c[ts - 1:ts, :], carry_ref.shape)
    parts = jnp.concatenate(_split3_bf16(c * LOG2E), axis=1)
    kb_ref[...] = (_dot(parts, selk_ref[...]) + ck_ref[...]).astype(BF16)
    qb_ref[...] = (_dot(parts, selq_ref[...]) + cq_ref[...]).astype(BF16)


def fox_bias_columns(gates, bias_row, batch, seq, *, ts=512):
    ns = seq // ts
    m = batch * seq
    width = FOX_HEADS * LANES
    sel_k, sel_q, const_k, const_q = _fox_selectors()
    whole = lambda shape: pl.BlockSpec(shape, lambda b, s: (0, 0))
    return pl.pallas_call(
        functools.partial(_fox_bias_kernel, ts=ts),
        grid=(batch, ns),
        in_specs=[pl.BlockSpec((ts, GATE_PAD), lambda b, s: (b * ns + s, 0)),
                  whole((1, GATE_PAD)),
                  whole(sel_k.shape), whole(sel_q.shape),
                  whole((1, width)), whole((1, width))],
        out_specs=[pl.BlockSpec((ts, width), lambda b, s: (b * ns + s, 0)),
                   pl.BlockSpec((ts, width), lambda b, s: (b * ns + s, 0))],
        out_shape=[jax.ShapeDtypeStruct((m, width), BF16),
                   jax.ShapeDtypeStruct((m, width), BF16)],
        scratch_shapes=[pltpu.VMEM((8, GATE_PAD), F32)],
        compiler_params=_cparams("parallel", "arbitrary"),
        name="fox_bias_columns",
    )(gates, bias_row, sel_k, sel_q, const_k, const_q)


def _fox_attn_kernel(q_ref, qb_ref, k_ref, kb_ref, v_ref, o_ref, vt_ref, m_ref, l_ref, acc_ref,
                     *, blk, nblk):
    qi = pl.program_id(2)

    @pl.when(qi == 0)
    def _():
        for c in range(nblk):
            vt_ref[c] = v_ref[c * blk:(c + 1) * blk, :].astype(F32).T.astype(BF16)

    q_aug = jnp.concatenate([q_ref[...], qb_ref[...]], axis=1)
    m_ref[...] = jnp.full_like(m_ref, -jnp.inf)
    l_ref[...] = jnp.zeros_like(l_ref)
    acc_ref[...] = jnp.zeros_like(acc_ref)

    def tile(kb, masked):
        start = pl.multiple_of(kb * blk, blk)
        k_aug = jnp.concatenate([k_ref[pl.ds(start, blk), :], kb_ref[pl.ds(start, blk), :]], axis=1)
        s = _dot_nt(k_aug, q_aug)
        if masked:
            key = lax.broadcasted_iota(jnp.int32, (blk, blk), 0)
            qry = lax.broadcasted_iota(jnp.int32, (blk, blk), 1)
            s = jnp.where(key <= qry, s, -jnp.inf)
        m_old = m_ref[...]
        m_new = jnp.maximum(m_old, jnp.max(s, axis=0, keepdims=True))
        alpha = jnp.exp2(m_old - m_new)
        p = jnp.exp2(s - m_new)
        l_ref[...] = alpha * l_ref[...] + jnp.sum(p, axis=0, keepdims=True)
        acc_ref[...] = alpha * acc_ref[...] + _dot(vt_ref[kb], p.astype(BF16))
        m_ref[...] = m_new

    def body(kb, carry):
        tile(kb, False)
        return carry

    lax.fori_loop(0, qi, body, 0)
    tile(qi, True)
    o_ref[...] = (acc_ref[...] / l_ref[...]).T.astype(o_ref.dtype)


def fox_attention(proj, key_bias, query_bias, batch, seq, *, blk=512):
    nq = seq // blk
    m = batch * seq
    nh, dh = FOX_HEADS, FOX_HEAD_DIM
    return pl.pallas_call(
        functools.partial(_fox_attn_kernel, blk=blk, nblk=nq),
        grid=(batch, nh, nq),
        in_specs=[
            pl.BlockSpec((blk, dh), lambda b, h, i: (b * nq + i, h)),
            pl.BlockSpec((blk, LANES), lambda b, h, i: (b * nq + i, h)),
            pl.BlockSpec((seq, dh), lambda b, h, i: (b, nh + h)),
            pl.BlockSpec((seq, LANES), lambda b, h, i: (b, h)),
            pl.BlockSpec((seq, dh), lambda b, h, i: (b, 2 * nh + h)),
        ],
        out_specs=pl.BlockSpec((blk, dh), lambda b, h, i: (b * nq + i, h)),
        out_shape=jax.ShapeDtypeStruct((m, nh * dh), BF16),
        scratch_shapes=[pltpu.VMEM((nq, dh, blk), BF16),
                        pltpu.VMEM((1, blk), F32), pltpu.VMEM((1, blk), F32),
                        pltpu.VMEM((dh, blk), F32)],
        compiler_params=_cparams("parallel", "parallel", "arbitrary"),
        name="fox_attention",
    )(proj, query_bias, proj, key_bias, proj)


def _pad_cols(t, width):
    return jnp.pad(t, ((0, 0), (0, width - t.shape[1])))


def _even_col_scale():
    cs = jnp.ones((EVEN_MAIN,), F32)
    cs = cs.at[EVEN_QS:EVEN_QS + SWA_Q_HEADS * HEAD_DIM].set(HEAD_DIM ** -0.5)
    cs = cs.at[EVEN_QM:EVEN_QM + MLSTM_HEADS * MLSTM_QK_DIM].set(MLSTM_QK_DIM ** -0.5)
    return cs.reshape(1, EVEN_MAIN)


def _fox_col_scale():
    cs = jnp.ones((FOX_MAIN,), F32)
    cs = cs.at[:FOX_HEADS * FOX_HEAD_DIM].set(FOX_HEAD_DIM ** -0.5 * LOG2E)
    return cs.reshape(1, FOX_MAIN)


def _even_mixer(h, gain, w_in, w_out, sinks, b_i, b_f, head_gain, batch, seq):
    q_s, k_s, v_s, q_m, k_m, v_m, o_m, gate_cols = jnp.split(
        w_in, [1024, 1280, 1536, 2048, 2560, 3584, 4608], axis=1)
    w_main = jnp.concatenate([q_s, v_m, o_m, q_m, k_m, k_s, v_s], axis=1).astype(BF16)
    w_gate = _pad_cols(gate_cols, GATE_PAD).astype(BF16)
    proj, gates = norm_proj(h, gain, w_main, _even_col_scale(), w_gate, tn=768)
    attn = swa_attention(proj, sinks, batch, seq)
    bias_row = _pad_cols(jnp.concatenate([b_i, b_f]).reshape(1, -1), GATE_PAD)
    mem = mlstm(proj, gates, bias_row, head_gain, batch, seq)
    w_out = w_out.astype(BF16)
    split = SWA_Q_HEADS * HEAD_DIM
    return matmul_residual([attn, mem], [w_out[:split], w_out[split:]], h)


def _fox_mixer(h, gain, w_in, b_f, w_out, batch, seq):
    w_main = w_in[:, :FOX_MAIN].astype(BF16)
    w_gate = _pad_cols(w_in[:, FOX_MAIN:], GATE_PAD).astype(BF16)
    proj, gates = norm_proj(h, gain, w_main, _fox_col_scale(), w_gate, tn=768)
    key_bias, query_bias = fox_bias_columns(
        gates, _pad_cols(b_f.reshape(1, -1), GATE_PAD), batch, seq)
    attn = fox_attention(proj, key_bias, query_bias, batch, seq)
    return matmul_residual([attn], [w_out.astype(BF16)], h)


def kernel(x, norm_mix, norm_ffn, norm_final, w_in_even, w_out_even, swa_sinks, mlstm_b_i, mlstm_b_f, mlstm_head_gain, w_in_odd, fox_b_f, w_out_odd, w_gate, w_up, w_down):
    batch, seq, d = x.shape
    depth = norm_mix.shape[0]
    h = x.reshape(batch * seq, d)
    for layer in range(depth):
        j = layer // 2
        if layer % 2 == 0:
            h = _even_mixer(h, norm_mix[layer], w_in_even[j], w_out_even[j], swa_sinks[j],
                            mlstm_b_i[j], mlstm_b_f[j], mlstm_head_gain[j], batch, seq)
        else:
            h = _fox_mixer(h, norm_mix[layer], w_in_odd[j], fox_b_f[j], w_out_odd[j], batch, seq)
        act = ffn_up(h, norm_ffn[layer], w_gate[layer].astype(BF16), w_up[layer].astype(BF16))
        h = matmul_residual([act], [w_down[layer].astype(BF16)], h)
    return final_norm(h, norm_final).reshape(batch, seq, d)
```

```python
import functools

import jax
import jax.numpy as jnp
import numpy as np
from jax import lax
from jax.experimental import pallas as pl
from jax.experimental.pallas import tpu as pltpu

F32 = jnp.float32
BF16 = jnp.bfloat16

D_MODEL = 2048
RMS_EPS = 1e-5

HEAD_DIM = 64
SWA_Q_HEADS = 16
SWA_KV_HEADS = 4
SWA_GROUP = SWA_Q_HEADS // SWA_KV_HEADS
WINDOW = 128
SWA_BLOCK = 128

MLSTM_HEADS = 4
MLSTM_V_DIM = 256
MLSTM_QK_DIM = 128
MLSTM_CHUNK = 128
GATE_SOFTCAP = 15.0

FOX_HEAD_DIM = 128
FOX_HEADS = 16

FFN_HIDDEN = 5632

EVEN_MAIN = 4608
EVEN_QS, EVEN_VM, EVEN_OM = 0, 1024, 2048
EVEN_QM, EVEN_KM, EVEN_KS, EVEN_VS = 3072, 3584, 4096, 4352
FOX_MAIN = 3 * FOX_HEADS * FOX_HEAD_DIM

LANES = 128
GATE_PAD = LANES
VMEM_LIMIT = 56 * 1024 * 1024


def _cparams(*semantics, flags=None):
    return pltpu.CompilerParams(dimension_semantics=semantics,
                                vmem_limit_bytes=VMEM_LIMIT, flags=flags)


def _rms_norm_rows(x, gain):
    ms = jnp.mean(x * x, axis=-1, keepdims=True)
    return x * lax.rsqrt(ms + RMS_EPS) * gain


def _log_sigmoid(x):
    return jnp.minimum(x, 0.0) - jnp.log(1.0 + jnp.exp(-jnp.abs(x)))


def _sigmoid(x):
    return 1.0 / (1.0 + jnp.exp(-x))


def _split3_bf16(x):
    hi = x.astype(BF16)
    r1 = x - hi.astype(F32)
    mid = r1.astype(BF16)
    lo = (r1 - mid.astype(F32)).astype(BF16)
    return hi, mid, lo


def _dot(a, b):
    return jnp.dot(a, b, preferred_element_type=F32)


def _dot_nt(a, b):
    return lax.dot_general(a, b, (((1,), (1,)), ((), ())), preferred_element_type=F32)


def _norm_proj_kernel(x_ref, g_ref, w_ref, cs_ref, wg_ref, o_ref, og_ref, xn_ref):
    @pl.when(pl.program_id(1) == 0)
    def _():
        xn = _rms_norm_rows(x_ref[...], g_ref[...]).astype(BF16)
        xn_ref[...] = xn
        og_ref[...] = _dot(xn, wg_ref[...])

    o_ref[...] = (_dot(xn_ref[...], w_ref[...]) * cs_ref[...]).astype(o_ref.dtype)


def norm_proj(x, gain, w, col_scale, w_gate, *, tm=1024, tn=768):
    m, d = x.shape
    n = w.shape[1]
    tm = min(tm, m)
    return pl.pallas_call(
        _norm_proj_kernel,
        grid=(m // tm, n // tn),
        in_specs=[
            pl.BlockSpec((tm, d), lambda i, j: (i, 0)),
            pl.BlockSpec((1, d), lambda i, j: (0, 0)),
            pl.BlockSpec((d, tn), lambda i, j: (0, j)),
            pl.BlockSpec((1, tn), lambda i, j: (0, j)),
            pl.BlockSpec((d, GATE_PAD), lambda i, j: (0, 0)),
        ],
        out_specs=[
            pl.BlockSpec((tm, tn), lambda i, j: (i, j)),
            pl.BlockSpec((tm, GATE_PAD), lambda i, j: (i, 0)),
        ],
        out_shape=[
            jax.ShapeDtypeStruct((m, n), BF16),
            jax.ShapeDtypeStruct((m, GATE_PAD), F32),
        ],
        scratch_shapes=[pltpu.VMEM((tm, d), BF16)],
        compiler_params=_cparams("parallel", "arbitrary"),
        name="norm_proj",
    )(x, gain.reshape(1, d), w, col_scale, w_gate)


def _matmul_res_kernel(*refs, n_lhs):
    a_refs = refs[:n_lhs]
    w_refs = refs[n_lhs:2 * n_lhs]
    r_ref, o_ref = refs[2 * n_lhs], refs[2 * n_lhs + 1]
    acc = r_ref[...]
    for a_ref, w_ref in zip(a_refs, w_refs):
        acc = acc + _dot(a_ref[...], w_ref[...])
    o_ref[...] = acc


def matmul_residual(lhs_list, w_list, res, *, tm=1024, tn=512):
    m, n = res.shape
    tm = min(tm, m)
    n_lhs = len(lhs_list)
    in_specs = []
    for a in lhs_list:
        in_specs.append(pl.BlockSpec((tm, a.shape[1]), lambda i, j: (i, 0)))
    for w in w_list:
        in_specs.append(pl.BlockSpec((w.shape[0], tn), lambda i, j: (0, j)))
    in_specs.append(pl.BlockSpec((tm, tn), lambda i, j: (i, j)))
    return pl.pallas_call(
        functools.partial(_matmul_res_kernel, n_lhs=n_lhs),
        grid=(m // tm, n // tn),
        in_specs=in_specs,
        out_specs=pl.BlockSpec((tm, tn), lambda i, j: (i, j)),
        out_shape=jax.ShapeDtypeStruct((m, n), F32),
        compiler_params=_cparams("parallel", "arbitrary"),
        name="matmul_residual",
    )(*lhs_list, *w_list, res)


def _ffn_up_kernel(x_ref, g_ref, wg_ref, wu_ref, o_ref, xn_ref):
    @pl.when(pl.program_id(1) == 0)
    def _():
        xn_ref[...] = _rms_norm_rows(x_ref[...], g_ref[...]).astype(BF16)

    xn = xn_ref[...]
    gate = _dot(xn, wg_ref[...])
    up = _dot(xn, wu_ref[...])
    o_ref[...] = (gate * _sigmoid(gate) * up).astype(o_ref.dtype)


def ffn_up(x, gain, w_gate, w_up, *, tm=1024, tn=512):
    m, d = x.shape
    n = w_gate.shape[1]
    tm = min(tm, m)
    return pl.pallas_call(
        _ffn_up_kernel,
        grid=(m // tm, n // tn),
        in_specs=[
            pl.BlockSpec((tm, d), lambda i, j: (i, 0)),
            pl.BlockSpec((1, d), lambda i, j: (0, 0)),
            pl.BlockSpec((d, tn), lambda i, j: (0, j)),
            pl.BlockSpec((d, tn), lambda i, j: (0, j)),
        ],
        out_specs=pl.BlockSpec((tm, tn), lambda i, j: (i, j)),
        out_shape=jax.ShapeDtypeStruct((m, n), BF16),
        scratch_shapes=[pltpu.VMEM((tm, d), BF16)],
        compiler_params=_cparams("parallel", "arbitrary"),
        name="ffn_up",
    )(x, gain.reshape(1, d), w_gate, w_up)


def _final_norm_kernel(x_ref, g_ref, o_ref):
    o_ref[...] = _rms_norm_rows(x_ref[...], g_ref[...])


def final_norm(x, gain, *, tm=1024):
    m, d = x.shape
    tm = min(tm, m)
    return pl.pallas_call(
        _final_norm_kernel,
        grid=(m // tm,),
        in_specs=[pl.BlockSpec((tm, d), lambda i: (i, 0)),
                  pl.BlockSpec((1, d), lambda i: (0, 0))],
        out_specs=pl.BlockSpec((tm, d), lambda i: (i, 0)),
        out_shape=jax.ShapeDtypeStruct((m, d), F32),
        compiler_params=_cparams("parallel"),
        name="final_norm",
    )(x, gain.reshape(1, d))


def _swa_kernel(sinks_ref, q_ref, kc_ref, kp_ref, vc_ref, vp_ref, o_ref):
    blk = SWA_BLOCK
    k = jnp.concatenate([kp_ref[...], kc_ref[...]], axis=0)
    v = jnp.concatenate([vp_ref[...], vc_ref[...]], axis=0)
    qi = lax.broadcasted_iota(jnp.int32, (blk, 2 * blk), 0) + blk
    kj = lax.broadcasted_iota(jnp.int32, (blk, 2 * blk), 1)
    dist = qi - kj
    first_key = jnp.where(pl.program_id(1) > 0, 0, blk)
    valid = (dist >= 0) & (dist < WINDOW) & (kj >= first_key)
    distf = dist.astype(F32)
    for h in range(SWA_Q_HEADS):
        hk = h // SWA_GROUP
        slope = 2.0 ** (-8.0 * (h + 1) / SWA_Q_HEADS)
        qh = q_ref[:, h * HEAD_DIM:(h + 1) * HEAD_DIM]
        kh = k[:, hk * HEAD_DIM:(hk + 1) * HEAD_DIM]
        vh = v[:, hk * HEAD_DIM:(hk + 1) * HEAD_DIM]
        s = _dot_nt(qh, kh) - slope * distf
        s = jnp.where(valid, s, -jnp.inf)
        sink = sinks_ref[h]
        m = jnp.maximum(jnp.max(s, axis=-1, keepdims=True), sink)
        p = jnp.exp(s - m)
        den = jnp.sum(p, axis=-1, keepdims=True) + jnp.exp(sink - m)
        o = _dot(p.astype(BF16), vh) / den
        o_ref[:, h * HEAD_DIM:(h + 1) * HEAD_DIM] = o.astype(o_ref.dtype)


def swa_attention(proj, sinks, batch, seq):
    blk = SWA_BLOCK
    nb = seq // blk
    m = batch * seq
    qw = SWA_Q_HEADS * HEAD_DIM
    kw = SWA_KV_HEADS * HEAD_DIM
    cur = lambda b, n: b * nb + n
    prev = lambda b, n: b * nb + jnp.maximum(n - 1, 0)
    return pl.pallas_call(
        _swa_kernel,
        grid=(batch, nb),
        in_specs=[
            pl.BlockSpec(memory_space=pltpu.SMEM),
            pl.BlockSpec((blk, qw), lambda b, n: (cur(b, n), EVEN_QS // qw)),
            pl.BlockSpec((blk, kw), lambda b, n: (cur(b, n), EVEN_KS // kw)),
            pl.BlockSpec((blk, kw), lambda b, n: (prev(b, n), EVEN_KS // kw)),
            pl.BlockSpec((blk, kw), lambda b, n: (cur(b, n), EVEN_VS // kw)),
            pl.BlockSpec((blk, kw), lambda b, n: (prev(b, n), EVEN_VS // kw)),
        ],
        out_specs=pl.BlockSpec((blk, qw), lambda b, n: (cur(b, n), 0)),
        out_shape=jax.ShapeDtypeStruct((m, qw), BF16),
        compiler_params=_cparams("parallel", "arbitrary"),
        name="swa_attention",
    )(sinks, proj, proj, proj, proj, proj)


def _mlstm_kernel(g_ref, bias_ref, gain_ref, q_ref, k_ref, v_ref, og_ref, o_ref,
                  c_ref, m_ref):
    L = MLSTM_CHUNK
    dk, dv = MLSTM_QK_DIM, MLSTM_V_DIM
    nh = MLSTM_HEADS

    @pl.when(pl.program_id(1) == 0)
    def _():
        c_ref[...] = jnp.zeros_like(c_ref)
        m_ref[...] = jnp.zeros_like(m_ref)

    a_all = GATE_SOFTCAP * jnp.tanh((g_ref[...] + bias_ref[...]) / GATE_SOFTCAP)
    lf_all = _log_sigmoid(a_all)
    row = lax.broadcasted_iota(jnp.int32, (L, L), 0)
    col = lax.broadcasted_iota(jnp.int32, (L, L), 1)
    causal = row >= col
    tri = jnp.where(causal, 1.0, 0.0).astype(BF16)
    hi, mid, lo = _split3_bf16(lf_all)
    b_all = _dot(tri, hi) + _dot(tri, mid) + _dot(tri, lo)
    a_all_t = a_all.T
    b_all_t = b_all.T
    ones = jnp.ones((L, LANES), BF16)

    for h in range(nh):
        li_c = a_all[:, h:h + 1]
        li_r = a_all_t[h:h + 1, :]
        b_c = b_all[:, nh + h:nh + h + 1]
        b_r = b_all_t[nh + h:nh + h + 1, :]
        b_last = b_r[:, L - 1:L]
        m_prev = m_ref[h:h + 1, 0:1]

        a_c = b_last - b_c + li_c
        a_max = jnp.max(a_c, axis=0, keepdims=True)
        w_c = jnp.exp(a_c - a_max)

        log_d = jnp.where(causal, b_c - b_r + li_r, -jnp.inf)
        g_c = b_c + m_prev
        m_t = jnp.maximum(g_c, jnp.max(log_d, axis=-1, keepdims=True))
        d_mat = jnp.exp(log_d - m_t)
        inter = jnp.exp(g_c - m_t)

        q = q_ref[:, h * dk:(h + 1) * dk]
        k = k_ref[:, h * dk:(h + 1) * dk]
        v_aug = jnp.concatenate([v_ref[:, h * dv:(h + 1) * dv], ones], axis=1)
        p = (_dot_nt(q, k) * d_mat).astype(BF16)
        c_prev = c_ref[h]
        num = _dot(p, v_aug) + inter * _dot(q, c_prev.astype(BF16))
        den = num[:, dv:dv + 1]
        hh = num[:, :dv] / jnp.maximum(jnp.abs(den), jnp.exp(-m_t))
        hn = _rms_norm_rows(hh, gain_ref[:, h * dv:(h + 1) * dv])
        gate = _sigmoid(og_ref[:, h * dv:(h + 1) * dv].astype(F32))
        o_ref[:, h * dv:(h + 1) * dv] = (gate * hn).astype(o_ref.dtype)

        m_new = jnp.maximum(b_last + m_prev, a_max)
        decay = jnp.exp(b_last + m_prev - m_new)
        scl = jnp.exp(a_max - m_new)
        kw_t = (k.astype(F32) * w_c).T.astype(BF16)
        c_ref[h] = decay * c_prev + scl * _dot(kw_t, v_aug)
        m_ref[h:h + 1, :] = jnp.broadcast_to(m_new, (1, LANES))


def mlstm(proj, gates, bias_row, head_gain, batch, seq):
    L = MLSTM_CHUNK
    nc = seq // L
    m = batch * seq
    nh, dk, dv = MLSTM_HEADS, MLSTM_QK_DIM, MLSTM_V_DIM
    rows = lambda b, c: b * nc + c
    return pl.pallas_call(
        _mlstm_kernel,
        grid=(batch, nc),
        in_specs=[
            pl.BlockSpec((L, GATE_PAD), lambda b, c: (rows(b, c), 0)),
            pl.BlockSpec((1, GATE_PAD), lambda b, c: (0, 0)),
            pl.BlockSpec((1, nh * dv), lambda b, c: (0, 0)),
            pl.BlockSpec((L, nh * dk), lambda b, c: (rows(b, c), EVEN_QM // (nh * dk))),
            pl.BlockSpec((L, nh * dk), lambda b, c: (rows(b, c), EVEN_KM // (nh * dk))),
            pl.BlockSpec((L, nh * dv), lambda b, c: (rows(b, c), EVEN_VM // (nh * dv))),
            pl.BlockSpec((L, nh * dv), lambda b, c: (rows(b, c), EVEN_OM // (nh * dv))),
        ],
        out_specs=pl.BlockSpec((L, nh * dv), lambda b, c: (rows(b, c), 0)),
        out_shape=jax.ShapeDtypeStruct((m, nh * dv), BF16),
        scratch_shapes=[pltpu.VMEM((nh, dk, dv + LANES), F32),
                        pltpu.VMEM((8, LANES), F32)],
        compiler_params=_cparams("parallel", "arbitrary"),
        name="mlstm",
    )(gates, bias_row, head_gain.reshape(1, nh * dv), proj, proj, proj, proj)


LOG2E = 1.4426950408889634
BIAS_COLS = 3


def _fox_selectors():
    width = FOX_HEADS * LANES
    sel_k = np.zeros((BIAS_COLS * GATE_PAD, width), np.float32)
    sel_q = np.zeros((BIAS_COLS * GATE_PAD, width), np.float32)
    const_k = np.zeros((1, width), np.float32)
    const_q = np.zeros((1, width), np.float32)
    for h in range(FOX_HEADS):
        for part in range(BIAS_COLS):
            sel_k[part * GATE_PAD + h, h * LANES + part] = 1.0
            sel_q[part * GATE_PAD + h, h * LANES + BIAS_COLS + part] = 1.0
            const_k[0, h * LANES + BIAS_COLS + part] = 1.0
            const_q[0, h * LANES + part] = -1.0
    return (jnp.asarray(sel_k, BF16), jnp.asarray(sel_q, BF16),
            jnp.asarray(const_k), jnp.asarray(const_q))


def _fox_bias_kernel(g_ref, bias_ref, selk_ref, selq_ref, ck_ref, cq_ref, kb_ref, qb_ref,
                     carry_ref, *, ts):
    @pl.when(pl.program_id(1) == 0)
    def _():
        carry_ref[...] = jnp.zeros_like(carry_ref)

    lf = _log_sigmoid(g_ref[...] + bias_ref[...])
    row = lax.broadcasted_iota(jnp.int32, (ts, ts), 0)
    col = lax.broadcasted_iota(jnp.int32, (ts, ts), 1)
    tri = jnp.where(row >= col, 1.0, 0.0).astype(BF16)
    hi, mid, lo = _split3_bf16(lf)
    c = _dot(tri, hi) + _dot(tri, mid) + _dot(tri, lo) + carry_ref[0:1, :]
    carry_ref[...] = jnp.broadcast_to(c[ts - 1:ts, :], carry_ref.shape)
    parts = jnp.concatenate(_split3_bf16(c * LOG2E), axis=1)
    kb_ref[...] = (_dot(parts, selk_ref[...]) + ck_ref[...]).astype(BF16)
    qb_ref[...] = (_dot(parts, selq_ref[...]) + cq_ref[...]).astype(BF16)


def fox_bias_columns(gates, bias_row, batch, seq, *, ts=512):
    ns = seq // ts
    m = batch * seq
    width = FOX_HEADS * LANES
    sel_k, sel_q, const_k, const_q = _fox_selectors()
    whole = lambda shape: pl.BlockSpec(shape, lambda b, s: (0, 0))
    return pl.pallas_call(
        functools.partial(_fox_bias_kernel, ts=ts),
        grid=(batch, ns),
        in_specs=[pl.BlockSpec((ts, GATE_PAD), lambda b, s: (b * ns + s, 0)),
                  whole((1, GATE_PAD)),
                  whole(sel_k.shape), whole(sel_q.shape),
                  whole((1, width)), whole((1, width))],
        out_specs=[pl.BlockSpec((ts, width), lambda b, s: (b * ns + s, 0)),
                   pl.BlockSpec((ts, width), lambda b, s: (b * ns + s, 0))],
        out_shape=[jax.ShapeDtypeStruct((m, width), BF16),
                   jax.ShapeDtypeStruct((m, width), BF16)],
        scratch_shapes=[pltpu.VMEM((8, GATE_PAD), F32)],
        compiler_params=_cparams("parallel", "arbitrary"),
        name="fox_bias_columns",
    )(gates, bias_row, sel_k, sel_q, const_k, const_q)


def _fox_attn_kernel(q_ref, qb_ref, k_ref, kb_ref, v_ref, o_ref, vt_ref, m_ref, l_ref, acc_ref,
                     *, blk, nblk):
    qi = pl.program_id(2)
    group = 4

    @pl.when(qi == 0)
    def _():
        for c in range(nblk):
            vt_ref[c] = v_ref[c * blk:(c + 1) * blk, :].astype(F32).T.astype(BF16)

    q_aug_t = jnp.concatenate([q_ref[...].astype(F32).T, qb_ref[...].astype(F32).T],
                              axis=0).astype(BF16)
    m_ref[...] = jnp.full_like(m_ref, -jnp.inf)
    l_ref[...] = jnp.zeros_like(l_ref)
    acc_ref[...] = jnp.zeros_like(acc_ref)

    def tiles(kbs, masks):
        scores = []
        for kb, masked in zip(kbs, masks):
            start = pl.multiple_of(kb * blk, blk)
            k_aug = jnp.concatenate([k_ref[pl.ds(start, blk), :], kb_ref[pl.ds(start, blk), :]],
                                    axis=1)
            s = _dot(k_aug, q_aug_t)
            if masked:
                key = lax.broadcasted_iota(jnp.int32, (blk, blk), 0)
                qry = lax.broadcasted_iota(jnp.int32, (blk, blk), 1)
                s = jnp.where(key <= qry, s, -jnp.inf)
            scores.append(s)
        m_run = m_ref[...]
        l_run = l_ref[...]
        alphas, probs = [], []
        for s in scores:
            m_new = jnp.maximum(m_run, jnp.max(s, axis=0, keepdims=True))
            alpha = jnp.exp2(m_run - m_new)
            p = jnp.exp2(s - m_new)
            l_run = alpha * l_run + jnp.sum(p, axis=0, keepdims=True)
            alphas.append(alpha)
            probs.append(p.astype(BF16))
            m_run = m_new
        acc = acc_ref[...]
        for kb, alpha, p in zip(kbs, alphas, probs):
            acc = alpha * acc + _dot(vt_ref[kb], p)
        acc_ref[...] = acc
        l_ref[...] = l_run
        m_ref[...] = m_run

    def body(j, carry):
        tiles([group * j + t for t in range(group)], [False] * group)
        return carry

    lax.fori_loop(0, qi // group, body, 0)

    for r in range(group):
        @pl.when(qi % group == r)
        def _(r=r):
            tiles([qi - r + t for t in range(r + 1)], [False] * r + [True])

    o_ref[...] = (acc_ref[...] / l_ref[...]).T.astype(o_ref.dtype)


def fox_attention(proj, key_bias, query_bias, batch, seq, *, blk=512):
    nq = seq // blk
    m = batch * seq
    nh, dh = FOX_HEADS, FOX_HEAD_DIM
    return pl.pallas_call(
        functools.partial(_fox_attn_kernel, blk=blk, nblk=nq),
        grid=(batch, nh, nq),
        in_specs=[
            pl.BlockSpec((blk, dh), lambda b, h, i: (b * nq + i, h)),
            pl.BlockSpec((blk, LANES), lambda b, h, i: (b * nq + i, h)),
            pl.BlockSpec((seq, dh), lambda b, h, i: (b, nh + h)),
            pl.BlockSpec((seq, LANES), lambda b, h, i: (b, h)),
            pl.BlockSpec((seq, dh), lambda b, h, i: (b, 2 * nh + h)),
        ],
        out_specs=pl.BlockSpec((blk, dh), lambda b, h, i: (b * nq + i, h)),
        out_shape=jax.ShapeDtypeStruct((m, nh * dh), BF16),
        scratch_shapes=[pltpu.VMEM((nq, dh, blk), BF16),
                        pltpu.VMEM((1, blk), F32), pltpu.VMEM((1, blk), F32),
                        pltpu.VMEM((dh, blk), F32)],
        compiler_params=_cparams("parallel", "parallel", "arbitrary"),
        name="fox_attention",
    )(proj, query_bias, proj, key_bias, proj)


def _pad_cols(t, width):
    return jnp.pad(t, ((0, 0), (0, width - t.shape[1])))


def _even_col_scale():
    cs = jnp.ones((EVEN_MAIN,), F32)
    cs = cs.at[EVEN_QS:EVEN_QS + SWA_Q_HEADS * HEAD_DIM].set(HEAD_DIM ** -0.5)
    cs = cs.at[EVEN_QM:EVEN_QM + MLSTM_HEADS * MLSTM_QK_DIM].set(MLSTM_QK_DIM ** -0.5)
    return cs.reshape(1, EVEN_MAIN)


def _fox_col_scale():
    cs = jnp.ones((FOX_MAIN,), F32)
    cs = cs.at[:FOX_HEADS * FOX_HEAD_DIM].set(FOX_HEAD_DIM ** -0.5 * LOG2E)
    return cs.reshape(1, FOX_MAIN)


def _even_mixer(h, gain, w_in, w_out, sinks, b_i, b_f, head_gain, batch, seq):
    q_s, k_s, v_s, q_m, k_m, v_m, o_m, gate_cols = jnp.split(
        w_in, [1024, 1280, 1536, 2048, 2560, 3584, 4608], axis=1)
    w_main = jnp.concatenate([q_s, v_m, o_m, q_m, k_m, k_s, v_s], axis=1).astype(BF16)
    w_gate = _pad_cols(gate_cols, GATE_PAD).astype(BF16)
    proj, gates = norm_proj(h, gain, w_main, _even_col_scale(), w_gate, tn=768)
    attn = swa_attention(proj, sinks, batch, seq)
    bias_row = _pad_cols(jnp.concatenate([b_i, b_f]).reshape(1, -1), GATE_PAD)
    mem = mlstm(proj, gates, bias_row, head_gain, batch, seq)
    w_out = w_out.astype(BF16)
    split = SWA_Q_HEADS * HEAD_DIM
    return matmul_residual([attn, mem], [w_out[:split], w_out[split:]], h)


def _fox_mixer(h, gain, w_in, b_f, w_out, batch, seq):
    w_main = w_in[:, :FOX_MAIN].astype(BF16)
    w_gate = _pad_cols(w_in[:, FOX_MAIN:], GATE_PAD).astype(BF16)
    proj, gates = norm_proj(h, gain, w_main, _fox_col_scale(), w_gate, tn=768)
    key_bias, query_bias = fox_bias_columns(
        gates, _pad_cols(b_f.reshape(1, -1), GATE_PAD), batch, seq)
    attn = fox_attention(proj, key_bias, query_bias, batch, seq)
    return matmul_residual([attn], [w_out.astype(BF16)], h)


def kernel(x, norm_mix, norm_ffn, norm_final, w_in_even, w_out_even, swa_sinks, mlstm_b_i, mlstm_b_f, mlstm_head_gain, w_in_odd, fox_b_f, w_out_odd, w_gate, w_up, w_down):
    batch, seq, d = x.shape
    depth = norm_mix.shape[0]
    h = x.reshape(batch * seq, d)
    for layer in range(depth):
        j = layer // 2
        if layer % 2 == 0:
            h = _even_mixer(h, norm_mix[layer], w_in_even[j], w_out_even[j], swa_sinks[j],
                            mlstm_b_i[j], mlstm_b_f[j], mlstm_head_gain[j], batch, seq)
        else:
            h = _fox_mixer(h, norm_mix[layer], w_in_odd[j], fox_b_f[j], w_out_odd[j], batch, seq)
        act = ffn_up(h, norm_ffn[layer], w_gate[layer].astype(BF16), w_up[layer].astype(BF16))
        h = matmul_residual([act], [w_down[layer].astype(BF16)], h)
    return final_norm(h, norm_final).reshape(batch, seq, d)
```

```python
import functools

import jax
import jax.numpy as jnp
import numpy as np
from jax import lax
from jax.experimental import pallas as pl
from jax.experimental.pallas import tpu as pltpu

F32 = jnp.float32
BF16 = jnp.bfloat16

D_MODEL = 2048
RMS_EPS = 1e-5

HEAD_DIM = 64
SWA_Q_HEADS = 16
SWA_KV_HEADS = 4
SWA_GROUP = SWA_Q_HEADS // SWA_KV_HEADS
WINDOW = 128
SWA_BLOCK = 128

MLSTM_HEADS = 4
MLSTM_V_DIM = 256
MLSTM_QK_DIM = 128
MLSTM_CHUNK = 128
GATE_SOFTCAP = 15.0

FOX_HEAD_DIM = 128
FOX_HEADS = 16

FFN_HIDDEN = 5632

EVEN_MAIN = 4608
EVEN_QS, EVEN_VM, EVEN_OM = 0, 1024, 2048
EVEN_QM, EVEN_KM, EVEN_KS, EVEN_VS = 3072, 3584, 4096, 4352
FOX_MAIN = 3 * FOX_HEADS * FOX_HEAD_DIM

LANES = 128
GATE_PAD = LANES
VMEM_LIMIT = 56 * 1024 * 1024


def _cparams(*semantics, flags=None):
    return pltpu.CompilerParams(dimension_semantics=semantics,
                                vmem_limit_bytes=VMEM_LIMIT, flags=flags)


def _rms_norm_rows(x, gain):
    ms = jnp.mean(x * x, axis=-1, keepdims=True)
    return x * lax.rsqrt(ms + RMS_EPS) * gain


def _log_sigmoid(x):
    return jnp.minimum(x, 0.0) - jnp.log(1.0 + jnp.exp(-jnp.abs(x)))


def _sigmoid(x):
    return 1.0 / (1.0 + jnp.exp(-x))


def _split3_bf16(x):
    hi = x.astype(BF16)
    r1 = x - hi.astype(F32)
    mid = r1.astype(BF16)
    lo = (r1 - mid.astype(F32)).astype(BF16)
    return hi, mid, lo


def _dot(a, b):
    return jnp.dot(a, b, preferred_element_type=F32)


def _dot_nt(a, b):
    return lax.dot_general(a, b, (((1,), (1,)), ((), ())), preferred_element_type=F32)


def _norm_proj_kernel(x_ref, g_ref, w_ref, cs_ref, wg_ref, o_ref, og_ref, xn_ref):
    @pl.when(pl.program_id(1) == 0)
    def _():
        xn = _rms_norm_rows(x_ref[...], g_ref[...]).astype(BF16)
        xn_ref[...] = xn
        og_ref[...] = _dot(xn, wg_ref[...])

    o_ref[...] = (_dot(xn_ref[...], w_ref[...]) * cs_ref[...]).astype(o_ref.dtype)


def norm_proj(x, gain, w, col_scale, w_gate, *, tm=1024, tn=768):
    m, d = x.shape
    n = w.shape[1]
    tm = min(tm, m)
    return pl.pallas_call(
        _norm_proj_kernel,
        grid=(m // tm, n // tn),
        in_specs=[
            pl.BlockSpec((tm, d), lambda i, j: (i, 0)),
            pl.BlockSpec((1, d), lambda i, j: (0, 0)),
            pl.BlockSpec((d, tn), lambda i, j: (0, j)),
            pl.BlockSpec((1, tn), lambda i, j: (0, j)),
            pl.BlockSpec((d, GATE_PAD), lambda i, j: (0, 0)),
        ],
        out_specs=[
            pl.BlockSpec((tm, tn), lambda i, j: (i, j)),
            pl.BlockSpec((tm, GATE_PAD), lambda i, j: (i, 0)),
        ],
        out_shape=[
            jax.ShapeDtypeStruct((m, n), BF16),
            jax.ShapeDtypeStruct((m, GATE_PAD), F32),
        ],
        scratch_shapes=[pltpu.VMEM((tm, d), BF16)],
        compiler_params=_cparams("parallel", "arbitrary"),
        name="norm_proj",
    )(x, gain.reshape(1, d), w, col_scale, w_gate)


def _matmul_res_kernel(*refs, n_lhs):
    a_refs = refs[:n_lhs]
    w_refs = refs[n_lhs:2 * n_lhs]
    r_ref, o_ref = refs[2 * n_lhs], refs[2 * n_lhs + 1]
    acc = r_ref[...]
    for a_ref, w_ref in zip(a_refs, w_refs):
        acc = acc + _dot(a_ref[...], w_ref[...])
    o_ref[...] = acc


def matmul_residual(lhs_list, w_list, res, *, tm=1024, tn=512):
    m, n = res.shape
    tm = min(tm, m)
    n_lhs = len(lhs_list)
    in_specs = []
    for a in lhs_list:
        in_specs.append(pl.BlockSpec((tm, a.shape[1]), lambda i, j: (i, 0)))
    for w in w_list:
        in_specs.append(pl.BlockSpec((w.shape[0], tn), lambda i, j: (0, j)))
    in_specs.append(pl.BlockSpec((tm, tn), lambda i, j: (i, j)))
    return pl.pallas_call(
        functools.partial(_matmul_res_kernel, n_lhs=n_lhs),
        grid=(m // tm, n // tn),
        in_specs=in_specs,
        out_specs=pl.BlockSpec((tm, tn), lambda i, j: (i, j)),
        out_shape=jax.ShapeDtypeStruct((m, n), F32),
        compiler_params=_cparams("parallel", "arbitrary"),
        name="matmul_residual",
    )(*lhs_list, *w_list, res)


def _ffn_up_kernel(x_ref, g_ref, wg_ref, wu_ref, o_ref, xn_ref):
    @pl.when(pl.program_id(1) == 0)
    def _():
        xn_ref[...] = _rms_norm_rows(x_ref[...], g_ref[...]).astype(BF16)

    xn = xn_ref[...]
    gate = _dot(xn, wg_ref[...])
    up = _dot(xn, wu_ref[...])
    o_ref[...] = (gate * _sigmoid(gate) * up).astype(o_ref.dtype)


def ffn_up(x, gain, w_gate, w_up, *, tm=1024, tn=512):
    m, d = x.shape
    n = w_gate.shape[1]
    tm = min(tm, m)
    return pl.pallas_call(
        _ffn_up_kernel,
        grid=(m // tm, n // tn),
        in_specs=[
            pl.BlockSpec((tm, d), lambda i, j: (i, 0)),
            pl.BlockSpec((1, d), lambda i, j: (0, 0)),
            pl.BlockSpec((d, tn), lambda i, j: (0, j)),
            pl.BlockSpec((d, tn), lambda i, j: (0, j)),
        ],
        out_specs=pl.BlockSpec((tm, tn), lambda i, j: (i, j)),
        out_shape=jax.ShapeDtypeStruct((m, n), BF16),
        scratch_shapes=[pltpu.VMEM((tm, d), BF16)],
        compiler_params=_cparams("parallel", "arbitrary"),
        name="ffn_up",
    )(x, gain.reshape(1, d), w_gate, w_up)


def _final_norm_kernel(x_ref, g_ref, o_ref):
    o_ref[...] = _rms_norm_rows(x_ref[...], g_ref[...])


def final_norm(x, gain, *, tm=1024):
    m, d = x.shape
    tm = min(tm, m)
    return pl.pallas_call(
        _final_norm_kernel,
        grid=(m // tm,),
        in_specs=[pl.BlockSpec((tm, d), lambda i: (i, 0)),
                  pl.BlockSpec((1, d), lambda i: (0, 0))],
        out_specs=pl.BlockSpec((tm, d), lambda i: (i, 0)),
        out_shape=jax.ShapeDtypeStruct((m, d), F32),
        compiler_params=_cparams("parallel"),
        name="final_norm",
    )(x, gain.reshape(1, d))


def _swa_kernel(sinks_ref, q_ref, kc_ref, kp_ref, vc_ref, vp_ref, o_ref):
    blk = SWA_BLOCK
    k = jnp.concatenate([kp_ref[...], kc_ref[...]], axis=0)
    v = jnp.concatenate([vp_ref[...], vc_ref[...]], axis=0)
    qi = lax.broadcasted_iota(jnp.int32, (blk, 2 * blk), 0) + blk
    kj = lax.broadcasted_iota(jnp.int32, (blk, 2 * blk), 1)
    dist = qi - kj
    first_key = jnp.where(pl.program_id(1) > 0, 0, blk)
    valid = (dist >= 0) & (dist < WINDOW) & (kj >= first_key)
    distf = dist.astype(F32)
    heads = range(SWA_Q_HEADS)
    kv = lambda t, h: t[:, (h // SWA_GROUP) * HEAD_DIM:(h // SWA_GROUP + 1) * HEAD_DIM]
    scores = [_dot_nt(q_ref[:, h * HEAD_DIM:(h + 1) * HEAD_DIM], kv(k, h)) for h in heads]
    probs, dens = [], []
    for h in heads:
        slope = 2.0 ** (-8.0 * (h + 1) / SWA_Q_HEADS)
        s = jnp.where(valid, scores[h] - slope * distf, -jnp.inf)
        sink = sinks_ref[h]
        m = jnp.maximum(jnp.max(s, axis=-1, keepdims=True), sink)
        p = jnp.exp(s - m)
        dens.append(jnp.sum(p, axis=-1, keepdims=True) + jnp.exp(sink - m))
        probs.append(p.astype(BF16))
    for h in heads:
        o = _dot(probs[h], kv(v, h)) / dens[h]
        o_ref[:, h * HEAD_DIM:(h + 1) * HEAD_DIM] = o.astype(o_ref.dtype)


def swa_attention(proj, sinks, batch, seq):
    blk = SWA_BLOCK
    nb = seq // blk
    m = batch * seq
    qw = SWA_Q_HEADS * HEAD_DIM
    kw = SWA_KV_HEADS * HEAD_DIM
    cur = lambda b, n: b * nb + n
    prev = lambda b, n: b * nb + jnp.maximum(n - 1, 0)
    return pl.pallas_call(
        _swa_kernel,
        grid=(batch, nb),
        in_specs=[
            pl.BlockSpec(memory_space=pltpu.SMEM),
            pl.BlockSpec((blk, qw), lambda b, n: (cur(b, n), EVEN_QS // qw)),
            pl.BlockSpec((blk, kw), lambda b, n: (cur(b, n), EVEN_KS // kw)),
            pl.BlockSpec((blk, kw), lambda b, n: (prev(b, n), EVEN_KS // kw)),
            pl.BlockSpec((blk, kw), lambda b, n: (cur(b, n), EVEN_VS // kw)),
            pl.BlockSpec((blk, kw), lambda b, n: (prev(b, n), EVEN_VS // kw)),
        ],
        out_specs=pl.BlockSpec((blk, qw), lambda b, n: (cur(b, n), 0)),
        out_shape=jax.ShapeDtypeStruct((m, qw), BF16),
        compiler_params=_cparams("parallel", "arbitrary"),
        name="swa_attention",
    )(sinks, proj, proj, proj, proj, proj)


def _mlstm_kernel(g_ref, bias_ref, gain_ref, q_ref, k_ref, v_ref, og_ref, o_ref,
                  c_ref, m_ref):
    L = MLSTM_CHUNK
    dk, dv = MLSTM_QK_DIM, MLSTM_V_DIM
    nh = MLSTM_HEADS

    @pl.when(pl.program_id(1) == 0)
    def _():
        c_ref[...] = jnp.zeros_like(c_ref)
        m_ref[...] = jnp.zeros_like(m_ref)

    a_all = GATE_SOFTCAP * jnp.tanh((g_ref[...] + bias_ref[...]) / GATE_SOFTCAP)
    lf_all = _log_sigmoid(a_all)
    row = lax.broadcasted_iota(jnp.int32, (L, L), 0)
    col = lax.broadcasted_iota(jnp.int32, (L, L), 1)
    causal = row >= col
    tri = jnp.where(causal, 1.0, 0.0).astype(BF16)
    hi, mid, lo = _split3_bf16(lf_all)
    b_all = _dot(tri, hi) + _dot(tri, mid) + _dot(tri, lo)
    a_all_t = a_all.T
    b_all_t = b_all.T
    ones = jnp.ones((L, LANES), BF16)

    for h in range(nh):
        li_c = a_all[:, h:h + 1]
        li_r = a_all_t[h:h + 1, :]
        b_c = b_all[:, nh + h:nh + h + 1]
        b_r = b_all_t[nh + h:nh + h + 1, :]
        b_last = b_r[:, L - 1:L]
        m_prev = m_ref[h:h + 1, 0:1]

        a_c = b_last - b_c + li_c
        a_max = jnp.max(a_c, axis=0, keepdims=True)
        w_c = jnp.exp(a_c - a_max)

        log_d = jnp.where(causal, b_c - b_r + li_r, -jnp.inf)
        g_c = b_c + m_prev
        m_t = jnp.maximum(g_c, jnp.max(log_d, axis=-1, keepdims=True))
        d_mat = jnp.exp(log_d - m_t)
        inter = jnp.exp(g_c - m_t)

        q = q_ref[:, h * dk:(h + 1) * dk]
        k = k_ref[:, h * dk:(h + 1) * dk]
        v_aug = jnp.concatenate([v_ref[:, h * dv:(h + 1) * dv], ones], axis=1)
        p = (_dot_nt(q, k) * d_mat).astype(BF16)
        c_prev = c_ref[h]
        num = _dot(p, v_aug) + inter * _dot(q, c_prev.astype(BF16))
        den = num[:, dv:dv + 1]
        hh = num[:, :dv] / jnp.maximum(jnp.abs(den), jnp.exp(-m_t))
        hn = _rms_norm_rows(hh, gain_ref[:, h * dv:(h + 1) * dv])
        gate = _sigmoid(og_ref[:, h * dv:(h + 1) * dv].astype(F32))
        o_ref[:, h * dv:(h + 1) * dv] = (gate * hn).astype(o_ref.dtype)

        m_new = jnp.maximum(b_last + m_prev, a_max)
        decay = jnp.exp(b_last + m_prev - m_new)
        scl = jnp.exp(a_max - m_new)
        kw_t = (k.astype(F32) * w_c).T.astype(BF16)
        c_ref[h] = decay * c_prev + scl * _dot(kw_t, v_aug)
        m_ref[h:h + 1, :] = jnp.broadcast_to(m_new, (1, LANES))


def mlstm(proj, gates, bias_row, head_gain, batch, seq):
    L = MLSTM_CHUNK
    nc = seq // L
    m = batch * seq
    nh, dk, dv = MLSTM_HEADS, MLSTM_QK_DIM, MLSTM_V_DIM
    rows = lambda b, c: b * nc + c
    return pl.pallas_call(
        _mlstm_kernel,
        grid=(batch, nc),
        in_specs=[
            pl.BlockSpec((L, GATE_PAD), lambda b, c: (rows(b, c), 0)),
            pl.BlockSpec((1, GATE_PAD), lambda b, c: (0, 0)),
            pl.BlockSpec((1, nh * dv), lambda b, c: (0, 0)),
            pl.BlockSpec((L, nh * dk), lambda b, c: (rows(b, c), EVEN_QM // (nh * dk))),
            pl.BlockSpec((L, nh * dk), lambda b, c: (rows(b, c), EVEN_KM // (nh * dk))),
            pl.BlockSpec((L, nh * dv), lambda b, c: (rows(b, c), EVEN_VM // (nh * dv))),
            pl.BlockSpec((L, nh * dv), lambda b, c: (rows(b, c), EVEN_OM // (nh * dv))),
        ],
        out_specs=pl.BlockSpec((L, nh * dv), lambda b, c: (rows(b, c), 0)),
        out_shape=jax.ShapeDtypeStruct((m, nh * dv), BF16),
        scratch_shapes=[pltpu.VMEM((nh, dk, dv + LANES), F32),
                        pltpu.VMEM((8, LANES), F32)],
        compiler_params=_cparams("parallel", "arbitrary"),
        name="mlstm",
    )(gates, bias_row, head_gain.reshape(1, nh * dv), proj, proj, proj, proj)


LOG2E = 1.4426950408889634
BIAS_COLS = 3


def _fox_selectors():
    width = FOX_HEADS * LANES
    sel_k = np.zeros((BIAS_COLS * GATE_PAD, width), np.float32)
    sel_q = np.zeros((BIAS_COLS * GATE_PAD, width), np.float32)
    const_k = np.zeros((1, width), np.float32)
    const_q = np.zeros((1, width), np.float32)
    for h in range(FOX_HEADS):
        for part in range(BIAS_COLS):
            sel_k[part * GATE_PAD + h, h * LANES + part] = 1.0
            sel_q[part * GATE_PAD + h, h * LANES + BIAS_COLS + part] = 1.0
            const_k[0, h * LANES + BIAS_COLS + part] = 1.0
            const_q[0, h * LANES + part] = -1.0
    return (jnp.asarray(sel_k, BF16), jnp.asarray(sel_q, BF16),
            jnp.asarray(const_k), jnp.asarray(const_q))


FOX_BLOCK = 512
STAT_ROWS = 8


def _head_norms_sq(x_ref, gsum):
    sq = jnp.square(x_ref[...].astype(F32))
    hi = sq.astype(BF16)
    lo = (sq - hi.astype(F32)).astype(BF16)
    return jnp.max(_dot(hi, gsum) + _dot(lo, gsum), axis=0, keepdims=True)


def _fox_bias_kernel(g_ref, bias_ref, selk_ref, selq_ref, ck_ref, cq_ref, gsum_ref, q_ref, k_ref,
                     kb_ref, qb_ref, stat_ref, carry_ref, *, ts):
    @pl.when(pl.program_id(1) == 0)
    def _():
        carry_ref[...] = jnp.zeros_like(carry_ref)

    lf = _log_sigmoid(g_ref[...] + bias_ref[...])
    row = lax.broadcasted_iota(jnp.int32, (ts, ts), 0)
    col = lax.broadcasted_iota(jnp.int32, (ts, ts), 1)
    tri = jnp.where(row >= col, 1.0, 0.0).astype(BF16)
    hi, mid, lo = _split3_bf16(lf)
    c = _dot(tri, hi) + _dot(tri, mid) + _dot(tri, lo) + carry_ref[0:1, :]
    carry_ref[...] = jnp.broadcast_to(c[ts - 1:ts, :], carry_ref.shape)
    c2 = c * LOG2E
    parts = jnp.concatenate(_split3_bf16(c2), axis=1)
    kb_ref[...] = (_dot(parts, selk_ref[...]) + ck_ref[...]).astype(BF16)
    qb_ref[...] = (_dot(parts, selq_ref[...]) + cq_ref[...]).astype(BF16)
    gsum = gsum_ref[...]
    stat_ref[0] = jnp.concatenate(
        [_head_norms_sq(q_ref, gsum), _head_norms_sq(k_ref, gsum), c2[0:1, :], c2[ts - 1:ts, :],
         jnp.zeros((STAT_ROWS - 4, GATE_PAD), F32)], axis=0)


def fox_bias_columns(gates, bias_row, proj, batch, seq):
    ts = FOX_BLOCK
    ns = seq // ts
    m = batch * seq
    width = FOX_HEADS * LANES
    sel_k, sel_q, const_k, const_q = _fox_selectors()
    gsum = np.zeros((width, GATE_PAD), np.float32)
    for h in range(FOX_HEADS):
        gsum[h * FOX_HEAD_DIM:(h + 1) * FOX_HEAD_DIM, h] = 1.0
    gsum = jnp.asarray(gsum, BF16)
    whole = lambda shape: pl.BlockSpec(shape, lambda b, s: (0, 0))
    rows = lambda b, s: b * ns + s
    return pl.pallas_call(
        functools.partial(_fox_bias_kernel, ts=ts),
        grid=(batch, ns),
        in_specs=[pl.BlockSpec((ts, GATE_PAD), lambda b, s: (rows(b, s), 0)),
                  whole((1, GATE_PAD)),
                  whole(sel_k.shape), whole(sel_q.shape),
                  whole((1, width)), whole((1, width)), whole(gsum.shape),
                  pl.BlockSpec((ts, width), lambda b, s: (rows(b, s), 0)),
                  pl.BlockSpec((ts, width), lambda b, s: (rows(b, s), 1))],
        out_specs=[pl.BlockSpec((ts, width), lambda b, s: (rows(b, s), 0)),
                   pl.BlockSpec((ts, width), lambda b, s: (rows(b, s), 0)),
                   pl.BlockSpec((1, STAT_ROWS, GATE_PAD), lambda b, s: (rows(b, s), 0, 0))],
        out_shape=[jax.ShapeDtypeStruct((m, width), BF16),
                   jax.ShapeDtypeStruct((m, width), BF16),
                   jax.ShapeDtypeStruct((batch * ns, STAT_ROWS, GATE_PAD), F32)],
        scratch_shapes=[pltpu.VMEM((8, GATE_PAD), F32)],
        compiler_params=_cparams("parallel", "arbitrary"),
        name="fox_bias_columns",
    )(gates, bias_row, sel_k, sel_q, const_k, const_q, gsum, proj, proj)


FOX_SKIP_BITS = 170.0
FOX_NORM_MARGIN = 1.02


def fox_first_tiles(stats, batch, seq):
    nq = seq // FOX_BLOCK
    st = stats.reshape(batch, nq, STAT_ROWS, GATE_PAD)[..., :FOX_HEADS]
    qn = jnp.sqrt(st[:, :, 0, :]) * FOX_NORM_MARGIN
    kn = jnp.sqrt(st[:, :, 1, :])
    c_first, c_last = st[:, :, 2, :], st[:, :, 3, :]
    upper = (qn[:, :, None, :] * kn[:, None, :, :]
             + c_first[:, :, None, :] - c_last[:, None, :, :])
    row_max_lower = -(qn * kn)[:, :, None, :]
    blocks = jnp.arange(nq)
    needed = (upper - row_max_lower >= -FOX_SKIP_BITS) | (blocks[None, None, :, None] >= blocks[None, :, None, None])
    first = jnp.argmax(needed, axis=2).astype(jnp.int32)
    return jnp.transpose(first, (0, 2, 1)).reshape(-1)


FOX_GROUP = 4


def _fox_attn_kernel(first_ref, q_ref, qb_ref, k_ref, kb_ref, v_ref, o_ref,
                     vt_ref, m_ref, l_ref, acc_ref, *, blk, nblk):
    qi = pl.program_id(2)
    group = FOX_GROUP

    @pl.when(qi == 0)
    def _():
        for c in range(nblk):
            vt_ref[c] = v_ref[c * blk:(c + 1) * blk, :].astype(F32).T.astype(BF16)

    q_aug_t = jnp.concatenate([q_ref[...].astype(F32).T, qb_ref[...].astype(F32).T],
                              axis=0).astype(BF16)
    m_ref[...] = jnp.full_like(m_ref, -jnp.inf)
    l_ref[...] = jnp.zeros_like(l_ref)
    acc_ref[...] = jnp.zeros_like(acc_ref)

    def tiles(kbs, masks):
        scores = []
        for kb, masked in zip(kbs, masks):
            start = pl.multiple_of(kb * blk, blk)
            k_aug = jnp.concatenate([k_ref[pl.ds(start, blk), :], kb_ref[pl.ds(start, blk), :]],
                                    axis=1)
            s = _dot(k_aug, q_aug_t)
            if masked:
                key = lax.broadcasted_iota(jnp.int32, (blk, blk), 0)
                qry = lax.broadcasted_iota(jnp.int32, (blk, blk), 1)
                s = jnp.where(key <= qry, s, -jnp.inf)
            scores.append(s)
        m_run = m_ref[...]
        l_run = l_ref[...]
        alphas, probs = [], []
        for s in scores:
            m_new = jnp.maximum(m_run, jnp.max(s, axis=0, keepdims=True))
            alpha = jnp.exp2(m_run - m_new)
            p = jnp.exp2(s - m_new)
            l_run = alpha * l_run + jnp.sum(p, axis=0, keepdims=True)
            alphas.append(alpha)
            probs.append(p.astype(BF16))
            m_run = m_new
        acc = acc_ref[...]
        for kb, alpha, p in zip(kbs, alphas, probs):
            acc = alpha * acc + _dot(vt_ref[kb], p)
        acc_ref[...] = acc
        l_ref[...] = l_run
        m_ref[...] = m_run

    first = first_ref[(pl.program_id(0) * pl.num_programs(1) + pl.program_id(1)) * nblk + qi]
    n_full = qi - first

    def body(j, carry):
        tiles([first + group * j + t for t in range(group)], [False] * group)
        return carry

    lax.fori_loop(0, n_full // group, body, 0)

    for r in range(group):
        @pl.when(n_full % group == r)
        def _(r=r):
            tiles([qi - r + t for t in range(r + 1)], [False] * r + [True])

    o_ref[...] = (acc_ref[...] / l_ref[...]).T.astype(o_ref.dtype)


def fox_attention(proj, key_bias, query_bias, first_tiles, batch, seq):
    blk = FOX_BLOCK
    nq = seq // blk
    m = batch * seq
    nh, dh = FOX_HEADS, FOX_HEAD_DIM
    return pl.pallas_call(
        functools.partial(_fox_attn_kernel, blk=blk, nblk=nq),
        grid=(batch, nh, nq),
        in_specs=[
            pl.BlockSpec(memory_space=pltpu.SMEM),
            pl.BlockSpec((blk, dh), lambda b, h, i: (b * nq + i, h)),
            pl.BlockSpec((blk, LANES), lambda b, h, i: (b * nq + i, h)),
            pl.BlockSpec((seq, dh), lambda b, h, i: (b, nh + h)),
            pl.BlockSpec((seq, LANES), lambda b, h, i: (b, h)),
            pl.BlockSpec((seq, dh), lambda b, h, i: (b, 2 * nh + h)),
        ],
        out_specs=pl.BlockSpec((blk, dh), lambda b, h, i: (b * nq + i, h)),
        out_shape=jax.ShapeDtypeStruct((m, nh * dh), BF16),
        scratch_shapes=[pltpu.VMEM((nq, dh, blk), BF16),
                        pltpu.VMEM((1, blk), F32), pltpu.VMEM((1, blk), F32),
                        pltpu.VMEM((dh, blk), F32)],
        compiler_params=_cparams("parallel", "parallel", "arbitrary"),
        name="fox_attention",
    )(first_tiles, proj, query_bias, proj, key_bias, proj)


def _pad_cols(t, width):
    return jnp.pad(t, ((0, 0), (0, width - t.shape[1])))


def _even_col_scale():
    cs = jnp.ones((EVEN_MAIN,), F32)
    cs = cs.at[EVEN_QS:EVEN_QS + SWA_Q_HEADS * HEAD_DIM].set(HEAD_DIM ** -0.5)
    cs = cs.at[EVEN_QM:EVEN_QM + MLSTM_HEADS * MLSTM_QK_DIM].set(MLSTM_QK_DIM ** -0.5)
    return cs.reshape(1, EVEN_MAIN)


def _fox_col_scale():
    cs = jnp.ones((FOX_MAIN,), F32)
    cs = cs.at[:FOX_HEADS * FOX_HEAD_DIM].set(FOX_HEAD_DIM ** -0.5 * LOG2E)
    return cs.reshape(1, FOX_MAIN)


def _even_mixer(h, gain, w_in, w_out, sinks, b_i, b_f, head_gain, batch, seq):
    q_s, k_s, v_s, q_m, k_m, v_m, o_m, gate_cols = jnp.split(
        w_in, [1024, 1280, 1536, 2048, 2560, 3584, 4608], axis=1)
    w_main = jnp.concatenate([q_s, v_m, o_m, q_m, k_m, k_s, v_s], axis=1).astype(BF16)
    w_gate = _pad_cols(gate_cols, GATE_PAD).astype(BF16)
    proj, gates = norm_proj(h, gain, w_main, _even_col_scale(), w_gate, tn=768)
    attn = swa_attention(proj, sinks, batch, seq)
    bias_row = _pad_cols(jnp.concatenate([b_i, b_f]).reshape(1, -1), GATE_PAD)
    mem = mlstm(proj, gates, bias_row, head_gain, batch, seq)
    w_out = w_out.astype(BF16)
    split = SWA_Q_HEADS * HEAD_DIM
    return matmul_residual([attn, mem], [w_out[:split], w_out[split:]], h)


def _fox_mixer(h, gain, w_in, b_f, w_out, batch, seq):
    w_main = w_in[:, :FOX_MAIN].astype(BF16)
    w_gate = _pad_cols(w_in[:, FOX_MAIN:], GATE_PAD).astype(BF16)
    proj, gates = norm_proj(h, gain, w_main, _fox_col_scale(), w_gate, tn=768)
    key_bias, query_bias, stats = fox_bias_columns(
        gates, _pad_cols(b_f.reshape(1, -1), GATE_PAD), proj, batch, seq)
    first_tiles = fox_first_tiles(stats, batch, seq)
    attn = fox_attention(proj, key_bias, query_bias, first_tiles, batch, seq)
    return matmul_residual([attn], [w_out.astype(BF16)], h)


def kernel(x, norm_mix, norm_ffn, norm_final, w_in_even, w_out_even, swa_sinks, mlstm_b_i, mlstm_b_f, mlstm_head_gain, w_in_odd, fox_b_f, w_out_odd, w_gate, w_up, w_down):
    batch, seq, d = x.shape
    depth = norm_mix.shape[0]
    h = x.reshape(batch * seq, d)
    for layer in range(depth):
        j = layer // 2
        if layer % 2 == 0:
            h = _even_mixer(h, norm_mix[layer], w_in_even[j], w_out_even[j], swa_sinks[j],
                            mlstm_b_i[j], mlstm_b_f[j], mlstm_head_gain[j], batch, seq)
        else:
            h = _fox_mixer(h, norm_mix[layer], w_in_odd[j], fox_b_f[j], w_out_odd[j], batch, seq)
        act = ffn_up(h, norm_ffn[layer], w_gate[layer].astype(BF16), w_up[layer].astype(BF16))
        h = matmul_residual([act], [w_down[layer].astype(BF16)], h)
    return final_norm(h, norm_final).reshape(batch, seq, d)
```

```python
import functools

import jax
import jax.numpy as jnp
import numpy as np
from jax import lax
from jax.experimental import pallas as pl
from jax.experimental.pallas import tpu as pltpu

F32 = jnp.float32
BF16 = jnp.bfloat16

D_MODEL = 2048
RMS_EPS = 1e-5
LOG2E = 1.4426950408889634

HEAD_DIM = 64
SWA_Q_HEADS = 16
SWA_KV_HEADS = 4
SWA_GROUP = SWA_Q_HEADS // SWA_KV_HEADS
WINDOW = 128
SWA_BLOCK = 128

MLSTM_HEADS = 4
MLSTM_V_DIM = 256
MLSTM_QK_DIM = 128
MLSTM_CHUNK = 128
GATE_SOFTCAP = 15.0

FOX_HEAD_DIM = 128
FOX_HEADS = 16

FFN_HIDDEN = 5632

EVEN_MAIN = 4608
EVEN_QS, EVEN_VM, EVEN_OM = 0, 1024, 2048
EVEN_QM, EVEN_KM, EVEN_KS, EVEN_VS = 3072, 3584, 4096, 4352
FOX_MAIN = 3 * FOX_HEADS * FOX_HEAD_DIM

LANES = 128
GATE_PAD = LANES
VMEM_LIMIT = 56 * 1024 * 1024


def _cparams(*semantics, flags=None):
    return pltpu.CompilerParams(dimension_semantics=semantics,
                                vmem_limit_bytes=VMEM_LIMIT, flags=flags)


def _rms_norm_rows(x, gain):
    ms = jnp.mean(x * x, axis=-1, keepdims=True)
    return x * lax.rsqrt(ms + RMS_EPS) * gain


def _log_sigmoid(x):
    return jnp.minimum(x, 0.0) - jnp.log(1.0 + jnp.exp(-jnp.abs(x)))


def _sigmoid(x):
    return 1.0 / (1.0 + jnp.exp(-x))


def _split3_bf16(x):
    hi = x.astype(BF16)
    r1 = x - hi.astype(F32)
    mid = r1.astype(BF16)
    lo = (r1 - mid.astype(F32)).astype(BF16)
    return hi, mid, lo


def _dot(a, b):
    return jnp.dot(a, b, preferred_element_type=F32)


def _dot_nt(a, b):
    return lax.dot_general(a, b, (((1,), (1,)), ((), ())), preferred_element_type=F32)


def _norm_proj_kernel(x_ref, g_ref, w_ref, cs_ref, wg_ref, o_ref, og_ref, xn_ref):
    @pl.when(pl.program_id(1) == 0)
    def _():
        xn = _rms_norm_rows(x_ref[...], g_ref[...]).astype(BF16)
        xn_ref[...] = xn
        og_ref[...] = _dot(xn, wg_ref[...])

    o_ref[...] = (_dot(xn_ref[...], w_ref[...]) * cs_ref[...]).astype(o_ref.dtype)


def norm_proj(x, gain, w, col_scale, w_gate, *, tm=1024, tn=768):
    m, d = x.shape
    n = w.shape[1]
    tm = min(tm, m)
    return pl.pallas_call(
        _norm_proj_kernel,
        grid=(m // tm, n // tn),
        in_specs=[
            pl.BlockSpec((tm, d), lambda i, j: (i, 0)),
            pl.BlockSpec((1, d), lambda i, j: (0, 0)),
            pl.BlockSpec((d, tn), lambda i, j: (0, j)),
            pl.BlockSpec((1, tn), lambda i, j: (0, j)),
            pl.BlockSpec((d, GATE_PAD), lambda i, j: (0, 0)),
        ],
        out_specs=[
            pl.BlockSpec((tm, tn), lambda i, j: (i, j)),
            pl.BlockSpec((tm, GATE_PAD), lambda i, j: (i, 0)),
        ],
        out_shape=[
            jax.ShapeDtypeStruct((m, n), BF16),
            jax.ShapeDtypeStruct((m, GATE_PAD), F32),
        ],
        scratch_shapes=[pltpu.VMEM((tm, d), BF16)],
        compiler_params=_cparams("parallel", "arbitrary"),
        name="norm_proj",
    )(x, gain.reshape(1, d), w, col_scale, w_gate)


def _matmul_res_kernel(*refs, n_lhs):
    a_refs = refs[:n_lhs]
    w_refs = refs[n_lhs:2 * n_lhs]
    r_ref, o_ref = refs[2 * n_lhs], refs[2 * n_lhs + 1]
    acc = r_ref[...]
    for a_ref, w_ref in zip(a_refs, w_refs):
        acc = acc + _dot(a_ref[...], w_ref[...])
    o_ref[...] = acc


def matmul_residual(lhs_list, w_list, res, layer, *, tm=1024, tn=512):
    m, n = res.shape
    tm = min(tm, m)
    n_lhs = len(lhs_list)
    in_specs = []
    for a in lhs_list:
        in_specs.append(pl.BlockSpec((tm, a.shape[1]), lambda i, j: (i, 0)))
    for w in w_list:
        in_specs.append(pl.BlockSpec((None, w.shape[1], tn), lambda i, j: (layer, 0, j)))
    in_specs.append(pl.BlockSpec((tm, tn), lambda i, j: (i, j)))
    return pl.pallas_call(
        functools.partial(_matmul_res_kernel, n_lhs=n_lhs),
        grid=(m // tm, n // tn),
        in_specs=in_specs,
        out_specs=pl.BlockSpec((tm, tn), lambda i, j: (i, j)),
        out_shape=jax.ShapeDtypeStruct((m, n), F32),
        compiler_params=_cparams("parallel", "arbitrary"),
        name="matmul_residual",
    )(*lhs_list, *w_list, res)


def _out_proj_norm_kernel(*refs, n_lhs):
    a_refs = refs[:n_lhs]
    w_refs = refs[n_lhs:2 * n_lhs]
    r_ref, g_ref, h_ref, xn_ref = refs[2 * n_lhs:2 * n_lhs + 4]
    acc = r_ref[...]
    for a_ref, w_ref in zip(a_refs, w_refs):
        acc = acc + _dot(a_ref[...], w_ref[...])
    h_ref[...] = acc
    xn_ref[...] = _rms_norm_rows(acc, g_ref[...]).astype(xn_ref.dtype)


def out_proj_norm(lhs_list, w_list, res, gain, *, tm=512):
    m, n = res.shape
    tm = min(tm, m)
    n_lhs = len(lhs_list)
    in_specs = [pl.BlockSpec((tm, a.shape[1]), lambda i: (i, 0)) for a in lhs_list]
    in_specs += [pl.BlockSpec(w.shape, lambda i: (0, 0)) for w in w_list]
    in_specs += [pl.BlockSpec((tm, n), lambda i: (i, 0)), pl.BlockSpec((1, n), lambda i: (0, 0))]
    return pl.pallas_call(
        functools.partial(_out_proj_norm_kernel, n_lhs=n_lhs),
        grid=(m // tm,),
        in_specs=in_specs,
        out_specs=[pl.BlockSpec((tm, n), lambda i: (i, 0)), pl.BlockSpec((tm, n), lambda i: (i, 0))],
        out_shape=[jax.ShapeDtypeStruct((m, n), F32), jax.ShapeDtypeStruct((m, n), BF16)],
        compiler_params=_cparams("parallel"),
        name="out_proj_norm",
    )(*lhs_list, *w_list, res, gain.reshape(1, n))


def _ffn_up_kernel(xn_ref, wg_ref, wu_ref, o_ref):
    xn = xn_ref[...]
    gate = _dot(xn, wg_ref[...])
    up = _dot(xn, wu_ref[...])
    o_ref[...] = (gate * _sigmoid(gate) * up).astype(o_ref.dtype)


def ffn_up(xn, w_gate, w_up, layer, *, tm=1024, tn=512):
    m, d = xn.shape
    n = w_gate.shape[2]
    tm = min(tm, m)
    return pl.pallas_call(
        _ffn_up_kernel,
        grid=(m // tm, n // tn),
        in_specs=[
            pl.BlockSpec((tm, d), lambda i, j: (i, 0)),
            pl.BlockSpec((None, d, tn), lambda i, j: (layer, 0, j)),
            pl.BlockSpec((None, d, tn), lambda i, j: (layer, 0, j)),
        ],
        out_specs=pl.BlockSpec((tm, tn), lambda i, j: (i, j)),
        out_shape=jax.ShapeDtypeStruct((m, n), BF16),
        compiler_params=_cparams("parallel", "arbitrary"),
        name="ffn_up",
    )(xn, w_gate, w_up)


def _cast_kernel(x_ref, o_ref):
    o_ref[...] = x_ref[...].astype(o_ref.dtype)


def cast_bf16(w, *, rows=512):
    lead, r, c = w.shape
    flat = w.reshape(lead * r, c)
    out = pl.pallas_call(
        _cast_kernel,
        grid=(lead * r // rows,),
        in_specs=[pl.BlockSpec((rows, c), lambda i: (i, 0))],
        out_specs=pl.BlockSpec((rows, c), lambda i: (i, 0)),
        out_shape=jax.ShapeDtypeStruct(flat.shape, BF16),
        compiler_params=_cparams("parallel"),
        name="cast_bf16",
    )(flat)
    return out.reshape(lead, r, c)


def _final_norm_kernel(x_ref, g_ref, o_ref):
    o_ref[...] = _rms_norm_rows(x_ref[...], g_ref[...])


def final_norm(x, gain, *, tm=1024):
    m, d = x.shape
    tm = min(tm, m)
    return pl.pallas_call(
        _final_norm_kernel,
        grid=(m // tm,),
        in_specs=[pl.BlockSpec((tm, d), lambda i: (i, 0)),
                  pl.BlockSpec((1, d), lambda i: (0, 0))],
        out_specs=pl.BlockSpec((tm, d), lambda i: (i, 0)),
        out_shape=jax.ShapeDtypeStruct((m, d), F32),
        compiler_params=_cparams("parallel"),
        name="final_norm",
    )(x, gain.reshape(1, d))


def _swa_kernel(sinks_ref, q_ref, kc_ref, kp_ref, vc_ref, vp_ref, o_ref, bias_ref):
    blk = SWA_BLOCK
    heads = range(SWA_Q_HEADS)

    @pl.when((pl.program_id(0) == 0) & (pl.program_id(1) == 0))
    def _():
        qi = lax.broadcasted_iota(jnp.int32, (blk, 2 * blk), 0) + blk
        kj = lax.broadcasted_iota(jnp.int32, (blk, 2 * blk), 1)
        dist = qi - kj
        in_window = (dist >= 0) & (dist < WINDOW)
        distf = dist.astype(F32)
        for h in heads:
            slope2 = 2.0 ** (-8.0 * (h + 1) / SWA_Q_HEADS) * LOG2E
            bias_ref[0, h] = jnp.where(in_window & (kj >= blk), -slope2 * distf, -jnp.inf)
            bias_ref[1, h] = jnp.where(in_window, -slope2 * distf, -jnp.inf)

    k = jnp.concatenate([kp_ref[...], kc_ref[...]], axis=0)
    v = jnp.concatenate([vp_ref[...], vc_ref[...]], axis=0)
    which = jnp.where(pl.program_id(1) > 0, 1, 0)
    kv = lambda t, h: t[:, (h // SWA_GROUP) * HEAD_DIM:(h // SWA_GROUP + 1) * HEAD_DIM]
    scores = [_dot_nt(q_ref[:, h * HEAD_DIM:(h + 1) * HEAD_DIM], kv(k, h)) for h in heads]
    probs, dens = [], []
    for h in heads:
        s = scores[h] + bias_ref[which, h]
        sink = sinks_ref[h] * LOG2E
        m = jnp.maximum(jnp.max(s, axis=-1, keepdims=True), sink)
        p = jnp.exp2(s - m)
        dens.append(jnp.sum(p, axis=-1, keepdims=True) + jnp.exp2(sink - m))
        probs.append(p.astype(BF16))
    for h in heads:
        o = _dot(probs[h], kv(v, h)) / dens[h]
        o_ref[:, h * HEAD_DIM:(h + 1) * HEAD_DIM] = o.astype(o_ref.dtype)


def swa_attention(proj, sinks, batch, seq):
    blk = SWA_BLOCK
    nb = seq // blk
    m = batch * seq
    qw = SWA_Q_HEADS * HEAD_DIM
    kw = SWA_KV_HEADS * HEAD_DIM
    cur = lambda b, n: b * nb + n
    prev = lambda b, n: b * nb + jnp.maximum(n - 1, 0)
    return pl.pallas_call(
        _swa_kernel,
        grid=(batch, nb),
        in_specs=[
            pl.BlockSpec(memory_space=pltpu.SMEM),
            pl.BlockSpec((blk, qw), lambda b, n: (cur(b, n), EVEN_QS // qw)),
            pl.BlockSpec((blk, kw), lambda b, n: (cur(b, n), EVEN_KS // kw)),
            pl.BlockSpec((blk, kw), lambda b, n: (prev(b, n), EVEN_KS // kw)),
            pl.BlockSpec((blk, kw), lambda b, n: (cur(b, n), EVEN_VS // kw)),
            pl.BlockSpec((blk, kw), lambda b, n: (prev(b, n), EVEN_VS // kw)),
        ],
        out_specs=pl.BlockSpec((blk, qw), lambda b, n: (cur(b, n), 0)),
        out_shape=jax.ShapeDtypeStruct((m, qw), BF16),
        scratch_shapes=[pltpu.VMEM((2, SWA_Q_HEADS, blk, 2 * blk), F32)],
        compiler_params=_cparams("arbitrary", "arbitrary"),
        name="swa_attention",
    )(sinks, proj, proj, proj, proj, proj)


def _mlstm_kernel(g_ref, bias_ref, gain_ref, q_ref, k_ref, v_ref, og_ref, o_ref,
                  c_ref, m_ref):
    L = MLSTM_CHUNK
    dk, dv = MLSTM_QK_DIM, MLSTM_V_DIM
    nh = MLSTM_HEADS

    @pl.when(pl.program_id(1) == 0)
    def _():
        c_ref[...] = jnp.zeros_like(c_ref)
        m_ref[...] = jnp.zeros_like(m_ref)

    a_all = GATE_SOFTCAP * jnp.tanh((g_ref[...] + bias_ref[...]) / GATE_SOFTCAP)
    lf_all = _log_sigmoid(a_all)
    row = lax.broadcasted_iota(jnp.int32, (L, L), 0)
    col = lax.broadcasted_iota(jnp.int32, (L, L), 1)
    causal = row >= col
    tri = jnp.where(causal, 1.0, 0.0).astype(BF16)
    hi, mid, lo = _split3_bf16(lf_all)
    b_all = _dot(tri, hi) + _dot(tri, mid) + _dot(tri, lo)
    a_all_t = a_all.T
    b_all_t = b_all.T
    ones = jnp.ones((L, LANES), BF16)

    for h in range(nh):
        li_c = a_all[:, h:h + 1]
        li_r = a_all_t[h:h + 1, :]
        b_c = b_all[:, nh + h:nh + h + 1]
        b_r = b_all_t[nh + h:nh + h + 1, :]
        b_last = b_r[:, L - 1:L]
        m_prev = m_ref[h:h + 1, 0:1]

        a_c = b_last - b_c + li_c
        a_max = jnp.max(a_c, axis=0, keepdims=True)
        w_c = jnp.exp(a_c - a_max)

        log_d = jnp.where(causal, b_c - b_r + li_r, -jnp.inf)
        g_c = b_c + m_prev
        m_t = jnp.maximum(g_c, jnp.max(log_d, axis=-1, keepdims=True))
        d_mat = jnp.exp(log_d - m_t)
        inter = jnp.exp(g_c - m_t)

        q = q_ref[:, h * dk:(h + 1) * dk]
        k = k_ref[:, h * dk:(h + 1) * dk]
        v_aug = jnp.concatenate([v_ref[:, h * dv:(h + 1) * dv], ones], axis=1)
        p = (_dot_nt(q, k) * d_mat).astype(BF16)
        c_prev = c_ref[h]
        num = _dot(p, v_aug) + inter * _dot(q, c_prev.astype(BF16))
        den = num[:, dv:dv + 1]
        hh = num[:, :dv] / jnp.maximum(jnp.abs(den), jnp.exp(-m_t))
        hn = _rms_norm_rows(hh, gain_ref[:, h * dv:(h + 1) * dv])
        gate = _sigmoid(og_ref[:, h * dv:(h + 1) * dv].astype(F32))
        o_ref[:, h * dv:(h + 1) * dv] = (gate * hn).astype(o_ref.dtype)

        m_new = jnp.maximum(b_last + m_prev, a_max)
        decay = jnp.exp(b_last + m_prev - m_new)
        scl = jnp.exp(a_max - m_new)
        kw_t = (k.astype(F32) * w_c).T.astype(BF16)
        c_ref[h] = decay * c_prev + scl * _dot(kw_t, v_aug)
        m_ref[h:h + 1, :] = jnp.broadcast_to(m_new, (1, LANES))


def mlstm(proj, gates, bias_row, head_gain, batch, seq):
    L = MLSTM_CHUNK
    nc = seq // L
    m = batch * seq
    nh, dk, dv = MLSTM_HEADS, MLSTM_QK_DIM, MLSTM_V_DIM
    rows = lambda b, c: b * nc + c
    return pl.pallas_call(
        _mlstm_kernel,
        grid=(batch, nc),
        in_specs=[
            pl.BlockSpec((L, GATE_PAD), lambda b, c: (rows(b, c), 0)),
            pl.BlockSpec((1, GATE_PAD), lambda b, c: (0, 0)),
            pl.BlockSpec((1, nh * dv), lambda b, c: (0, 0)),
            pl.BlockSpec((L, nh * dk), lambda b, c: (rows(b, c), EVEN_QM // (nh * dk))),
            pl.BlockSpec((L, nh * dk), lambda b, c: (rows(b, c), EVEN_KM // (nh * dk))),
            pl.BlockSpec((L, nh * dv), lambda b, c: (rows(b, c), EVEN_VM // (nh * dv))),
            pl.BlockSpec((L, nh * dv), lambda b, c: (rows(b, c), EVEN_OM // (nh * dv))),
        ],
        out_specs=pl.BlockSpec((L, nh * dv), lambda b, c: (rows(b, c), 0)),
        out_shape=jax.ShapeDtypeStruct((m, nh * dv), BF16),
        scratch_shapes=[pltpu.VMEM((nh, dk, dv + LANES), F32),
                        pltpu.VMEM((8, LANES), F32)],
        compiler_params=_cparams("parallel", "arbitrary"),
        name="mlstm",
    )(gates, bias_row, head_gain.reshape(1, nh * dv), proj, proj, proj, proj)


BIAS_COLS = 3


def _fox_selectors():
    width = FOX_HEADS * LANES
    sel_k = np.zeros((BIAS_COLS * GATE_PAD, width), np.float32)
    sel_q = np.zeros((BIAS_COLS * GATE_PAD, width), np.float32)
    const_k = np.zeros((1, width), np.float32)
    const_q = np.zeros((1, width), np.float32)
    for h in range(FOX_HEADS):
        for part in range(BIAS_COLS):
            sel_k[part * GATE_PAD + h, h * LANES + part] = 1.0
            sel_q[part * GATE_PAD + h, h * LANES + BIAS_COLS + part] = 1.0
            const_k[0, h * LANES + BIAS_COLS + part] = 1.0
            const_q[0, h * LANES + part] = -1.0
    return (jnp.asarray(sel_k, BF16), jnp.asarray(sel_q, BF16),
            jnp.asarray(const_k), jnp.asarray(const_q))


FOX_BLOCK = 512
STAT_ROWS = 8


def _head_norms_sq(x_ref, gsum):
    return jnp.max(_dot(jnp.square(x_ref[...]), gsum), axis=0, keepdims=True)


def _fox_bias_kernel(g_ref, bias_ref, selk_ref, selq_ref, ck_ref, cq_ref, gsum_ref, q_ref, k_ref,
                     kb_ref, qb_ref, stat_ref, carry_ref, *, ts):
    @pl.when(pl.program_id(1) == 0)
    def _():
        carry_ref[...] = jnp.zeros_like(carry_ref)

    lf = _log_sigmoid(g_ref[...] + bias_ref[...])
    row = lax.broadcasted_iota(jnp.int32, (ts, ts), 0)
    col = lax.broadcasted_iota(jnp.int32, (ts, ts), 1)
    tri = jnp.where(row >= col, 1.0, 0.0).astype(BF16)
    hi, mid, lo = _split3_bf16(lf)
    c = _dot(tri, hi) + _dot(tri, mid) + _dot(tri, lo) + carry_ref[0:1, :]
    carry_ref[...] = jnp.broadcast_to(c[ts - 1:ts, :], carry_ref.shape)
    c2 = c * LOG2E
    parts = jnp.concatenate(_split3_bf16(c2), axis=1)
    kb_ref[...] = (_dot(parts, selk_ref[...]) + ck_ref[...]).astype(BF16)
    qb_ref[...] = (_dot(parts, selq_ref[...]) + cq_ref[...]).astype(BF16)
    gsum = gsum_ref[...]
    stat_ref[0] = jnp.concatenate(
        [_head_norms_sq(q_ref, gsum), _head_norms_sq(k_ref, gsum), c2[0:1, :], c2[ts - 1:ts, :],
         jnp.zeros((STAT_ROWS - 4, GATE_PAD), F32)], axis=0)


def fox_bias_columns(gates, bias_row, proj, batch, seq):
    ts = FOX_BLOCK
    ns = seq // ts
    m = batch * seq
    width = FOX_HEADS * LANES
    sel_k, sel_q, const_k, const_q = _fox_selectors()
    gsum = np.zeros((width, GATE_PAD), np.float32)
    for h in range(FOX_HEADS):
        gsum[h * FOX_HEAD_DIM:(h + 1) * FOX_HEAD_DIM, h] = 1.0
    gsum = jnp.asarray(gsum, BF16)
    whole = lambda shape: pl.BlockSpec(shape, lambda b, s: (0, 0))
    rows = lambda b, s: b * ns + s
    return pl.pallas_call(
        functools.partial(_fox_bias_kernel, ts=ts),
        grid=(batch, ns),
        in_specs=[pl.BlockSpec((ts, GATE_PAD), lambda b, s: (rows(b, s), 0)),
                  whole((1, GATE_PAD)),
                  whole(sel_k.shape), whole(sel_q.shape),
                  whole((1, width)), whole((1, width)), whole(gsum.shape),
                  pl.BlockSpec((ts, width), lambda b, s: (rows(b, s), 0)),
                  pl.BlockSpec((ts, width), lambda b, s: (rows(b, s), 1))],
        out_specs=[pl.BlockSpec((ts, width), lambda b, s: (rows(b, s), 0)),
                   pl.BlockSpec((ts, width), lambda b, s: (rows(b, s), 0)),
                   pl.BlockSpec((1, STAT_ROWS, GATE_PAD), lambda b, s: (rows(b, s), 0, 0))],
        out_shape=[jax.ShapeDtypeStruct((m, width), BF16),
                   jax.ShapeDtypeStruct((m, width), BF16),
                   jax.ShapeDtypeStruct((batch * ns, STAT_ROWS, GATE_PAD), F32)],
        scratch_shapes=[pltpu.VMEM((8, GATE_PAD), F32)],
        compiler_params=_cparams("parallel", "arbitrary"),
        name="fox_bias_columns",
    )(gates, bias_row, sel_k, sel_q, const_k, const_q, gsum, proj, proj)


FOX_SKIP_BITS = 170.0
FOX_NORM_MARGIN = 1.02


def fox_first_tiles(stats, batch, seq):
    nq = seq // FOX_BLOCK
    st = stats.reshape(batch, nq, STAT_ROWS, GATE_PAD)[..., :FOX_HEADS]
    qn = jnp.sqrt(st[:, :, 0, :]) * FOX_NORM_MARGIN
    kn = jnp.sqrt(st[:, :, 1, :])
    c_first, c_last = st[:, :, 2, :], st[:, :, 3, :]
    upper = (qn[:, :, None, :] * kn[:, None, :, :]
             + c_first[:, :, None, :] - c_last[:, None, :, :])
    row_max_lower = -(qn * kn)[:, :, None, :]
    blocks = jnp.arange(nq)
    needed = (upper - row_max_lower >= -FOX_SKIP_BITS) | (blocks[None, None, :, None] >= blocks[None, :, None, None])
    first = jnp.argmax(needed, axis=2).astype(jnp.int32)
    return jnp.transpose(first, (0, 2, 1)).reshape(-1)


FOX_GROUP = 4


def _fox_attn_kernel(first_ref, q_ref, qb_ref, k_ref, kb_ref, v_ref, o_ref,
                     vt_ref, m_ref, l_ref, acc_ref, *, blk, nblk):
    qi = pl.program_id(2)
    group = FOX_GROUP

    @pl.when(qi == 0)
    def _():
        for c in range(nblk):
            vt_ref[c] = v_ref[c * blk:(c + 1) * blk, :].astype(F32).T.astype(BF16)

    q_aug_t = jnp.concatenate([q_ref[...].astype(F32).T, qb_ref[...].astype(F32).T],
                              axis=0).astype(BF16)
    m_ref[...] = jnp.full_like(m_ref, -jnp.inf)
    l_ref[...] = jnp.zeros_like(l_ref)
    acc_ref[...] = jnp.zeros_like(acc_ref)

    def tiles(kbs, masks):
        scores = []
        for kb, masked in zip(kbs, masks):
            start = pl.multiple_of(kb * blk, blk)
            k_aug = jnp.concatenate([k_ref[pl.ds(start, blk), :], kb_ref[pl.ds(start, blk), :]],
                                    axis=1)
            s = _dot(k_aug, q_aug_t)
            if masked:
                key = lax.broadcasted_iota(jnp.int32, (blk, blk), 0)
                qry = lax.broadcasted_iota(jnp.int32, (blk, blk), 1)
                s = jnp.where(key <= qry, s, -jnp.inf)
            scores.append(s)
        m_run = m_ref[...]
        l_run = l_ref[...]
        alphas, probs = [], []
        for s in scores:
            m_new = jnp.maximum(m_run, jnp.max(s, axis=0, keepdims=True))
            alpha = jnp.exp2(m_run - m_new)
            p = jnp.exp2(s - m_new)
            l_run = alpha * l_run + jnp.sum(p, axis=0, keepdims=True)
            alphas.append(alpha)
            probs.append(p.astype(BF16))
            m_run = m_new
        acc = acc_ref[...]
        for kb, alpha, p in zip(kbs, alphas, probs):
            acc = alpha * acc + _dot(vt_ref[kb], p)
        acc_ref[...] = acc
        l_ref[...] = l_run
        m_ref[...] = m_run

    first = first_ref[(pl.program_id(0) * pl.num_programs(1) + pl.program_id(1)) * nblk + qi]
    n_full = qi - first

    def body(j, carry):
        tiles([first + group * j + t for t in range(group)], [False] * group)
        return carry

    lax.fori_loop(0, n_full // group, body, 0)

    for r in range(group):
        @pl.when(n_full % group == r)
        def _(r=r):
            tiles([qi - r + t for t in range(r + 1)], [False] * r + [True])

    o_ref[...] = (acc_ref[...] / l_ref[...]).T.astype(o_ref.dtype)


def fox_attention(proj, key_bias, query_bias, first_tiles, batch, seq):
    blk = FOX_BLOCK
    nq = seq // blk
    m = batch * seq
    nh, dh = FOX_HEADS, FOX_HEAD_DIM
    return pl.pallas_call(
        functools.partial(_fox_attn_kernel, blk=blk, nblk=nq),
        grid=(batch, nh, nq),
        in_specs=[
            pl.BlockSpec(memory_space=pltpu.SMEM),
            pl.BlockSpec((blk, dh), lambda b, h, i: (b * nq + i, h)),
            pl.BlockSpec((blk, LANES), lambda b, h, i: (b * nq + i, h)),
            pl.BlockSpec((seq, dh), lambda b, h, i: (b, nh + h)),
            pl.BlockSpec((seq, LANES), lambda b, h, i: (b, h)),
            pl.BlockSpec((seq, dh), lambda b, h, i: (b, 2 * nh + h)),
        ],
        out_specs=pl.BlockSpec((blk, dh), lambda b, h, i: (b * nq + i, h)),
        out_shape=jax.ShapeDtypeStruct((m, nh * dh), BF16),
        scratch_shapes=[pltpu.VMEM((nq, dh, blk), BF16),
                        pltpu.VMEM((1, blk), F32), pltpu.VMEM((1, blk), F32),
                        pltpu.VMEM((dh, blk), F32)],
        compiler_params=_cparams("parallel", "parallel", "arbitrary"),
        name="fox_attention",
    )(first_tiles, proj, query_bias, proj, key_bias, proj)


def _pad_cols(t, width):
    return jnp.pad(t, ((0, 0), (0, width - t.shape[1])))


def _even_col_scale():
    cs = jnp.ones((EVEN_MAIN,), F32)
    cs = cs.at[EVEN_QS:EVEN_QS + SWA_Q_HEADS * HEAD_DIM].set(HEAD_DIM ** -0.5 * LOG2E)
    cs = cs.at[EVEN_QM:EVEN_QM + MLSTM_HEADS * MLSTM_QK_DIM].set(MLSTM_QK_DIM ** -0.5)
    return cs.reshape(1, EVEN_MAIN)


def _fox_col_scale():
    cs = jnp.ones((FOX_MAIN,), F32)
    cs = cs.at[:FOX_HEADS * FOX_HEAD_DIM].set(FOX_HEAD_DIM ** -0.5 * LOG2E)
    return cs.reshape(1, FOX_MAIN)


def _even_mixer(h, gain, w_in, w_out, sinks, b_i, b_f, head_gain, ffn_gain, batch, seq):
    q_s, k_s, v_s, q_m, k_m, v_m, o_m, gate_cols = jnp.split(
        w_in, [1024, 1280, 1536, 2048, 2560, 3584, 4608], axis=1)
    w_main = jnp.concatenate([q_s, v_m, o_m, q_m, k_m, k_s, v_s], axis=1).astype(BF16)
    w_gate = _pad_cols(gate_cols, GATE_PAD).astype(BF16)
    proj, gates = norm_proj(h, gain, w_main, _even_col_scale(), w_gate, tn=768)
    attn = swa_attention(proj, sinks, batch, seq)
    bias_row = _pad_cols(jnp.concatenate([b_i, b_f]).reshape(1, -1), GATE_PAD)
    mem = mlstm(proj, gates, bias_row, head_gain, batch, seq)
    w_out = w_out.astype(BF16)
    split = SWA_Q_HEADS * HEAD_DIM
    return out_proj_norm([attn, mem], [w_out[:split], w_out[split:]], h, ffn_gain)


def _fox_mixer(h, gain, w_in, b_f, w_out, ffn_gain, batch, seq):
    w_main = w_in[:, :FOX_MAIN].astype(BF16)
    w_gate = _pad_cols(w_in[:, FOX_MAIN:], GATE_PAD).astype(BF16)
    proj, gates = norm_proj(h, gain, w_main, _fox_col_scale(), w_gate, tn=768)
    key_bias, query_bias, stats = fox_bias_columns(
        gates, _pad_cols(b_f.reshape(1, -1), GATE_PAD), proj, batch, seq)
    first_tiles = fox_first_tiles(stats, batch, seq)
    attn = fox_attention(proj, key_bias, query_bias, first_tiles, batch, seq)
    return out_proj_norm([attn], [w_out.astype(BF16)], h, ffn_gain)


def kernel(x, norm_mix, norm_ffn, norm_final, w_in_even, w_out_even, swa_sinks, mlstm_b_i, mlstm_b_f, mlstm_head_gain, w_in_odd, fox_b_f, w_out_odd, w_gate, w_up, w_down):
    batch, seq, d = x.shape
    depth = norm_mix.shape[0]
    h = x.reshape(batch * seq, d)
    w_gate, w_up, w_down = cast_bf16(w_gate), cast_bf16(w_up), cast_bf16(w_down)
    for layer in range(depth):
        j = layer // 2
        if layer % 2 == 0:
            h, xn = _even_mixer(h, norm_mix[layer], w_in_even[j], w_out_even[j], swa_sinks[j],
                                mlstm_b_i[j], mlstm_b_f[j], mlstm_head_gain[j], norm_ffn[layer],
                                batch, seq)
        else:
            h, xn = _fox_mixer(h, norm_mix[layer], w_in_odd[j], fox_b_f[j], w_out_odd[j],
                               norm_ffn[layer], batch, seq)
        act = ffn_up(xn, w_gate, w_up, layer)
        h = matmul_residual([act], [w_down], h, layer)
    return final_norm(h, norm_final).reshape(batch, seq, d)
```

```python
import functools

import jax
import jax.numpy as jnp
import numpy as np
from jax import lax
from jax.experimental import pallas as pl
from jax.experimental.pallas import tpu as pltpu

F32 = jnp.float32
BF16 = jnp.bfloat16

D_MODEL = 2048
RMS_EPS = 1e-5
LOG2E = 1.4426950408889634

HEAD_DIM = 64
SWA_Q_HEADS = 16
SWA_KV_HEADS = 4
SWA_GROUP = SWA_Q_HEADS // SWA_KV_HEADS
WINDOW = 128
SWA_BLOCK = 128

MLSTM_HEADS = 4
MLSTM_V_DIM = 256
MLSTM_QK_DIM = 128
MLSTM_CHUNK = 512
GATE_SOFTCAP = 15.0

FOX_HEAD_DIM = 128
FOX_HEADS = 16

FFN_HIDDEN = 5632

EVEN_MAIN = 4608
EVEN_QS, EVEN_VM, EVEN_OM = 0, 1024, 2048
EVEN_QM, EVEN_KM, EVEN_KS, EVEN_VS = 3072, 3584, 4096, 4352
FOX_MAIN = 3 * FOX_HEADS * FOX_HEAD_DIM

LANES = 128
GATE_PAD = LANES
VMEM_LIMIT = 56 * 1024 * 1024


def _cparams(*semantics, flags=None):
    return pltpu.CompilerParams(dimension_semantics=semantics,
                                vmem_limit_bytes=VMEM_LIMIT, flags=flags)


def _rms_norm_rows(x, gain):
    ms = jnp.mean(x * x, axis=-1, keepdims=True)
    return x * lax.rsqrt(ms + RMS_EPS) * gain


def _log_sigmoid(x):
    return jnp.minimum(x, 0.0) - jnp.log(1.0 + jnp.exp(-jnp.abs(x)))


def _sigmoid(x):
    return 1.0 / (1.0 + jnp.exp(-x))


def _split3_bf16(x):
    hi = x.astype(BF16)
    r1 = x - hi.astype(F32)
    mid = r1.astype(BF16)
    lo = (r1 - mid.astype(F32)).astype(BF16)
    return hi, mid, lo


def _dot(a, b):
    return jnp.dot(a, b, preferred_element_type=F32)


def _dot_nt(a, b):
    return lax.dot_general(a, b, (((1,), (1,)), ((), ())), preferred_element_type=F32)


def _norm_proj_kernel(x_ref, g_ref, w_ref, cs_ref, wg_ref, o_ref, og_ref, xn_ref):
    @pl.when(pl.program_id(1) == 0)
    def _():
        xn = _rms_norm_rows(x_ref[...], g_ref[...]).astype(BF16)
        xn_ref[...] = xn
        og_ref[...] = _dot(xn, wg_ref[...])

    o_ref[...] = (_dot(xn_ref[...], w_ref[...]) * cs_ref[...]).astype(o_ref.dtype)


def norm_proj(x, gain, w, col_scale, w_gate, *, tm=1024, tn=768):
    m, d = x.shape
    n = w.shape[1]
    tm = min(tm, m)
    return pl.pallas_call(
        _norm_proj_kernel,
        grid=(m // tm, n // tn),
        in_specs=[
            pl.BlockSpec((tm, d), lambda i, j: (i, 0)),
            pl.BlockSpec((1, d), lambda i, j: (0, 0)),
            pl.BlockSpec((d, tn), lambda i, j: (0, j)),
            pl.BlockSpec((1, tn), lambda i, j: (0, j)),
            pl.BlockSpec((d, GATE_PAD), lambda i, j: (0, 0)),
        ],
        out_specs=[
            pl.BlockSpec((tm, tn), lambda i, j: (i, j)),
            pl.BlockSpec((tm, GATE_PAD), lambda i, j: (i, 0)),
        ],
        out_shape=[
            jax.ShapeDtypeStruct((m, n), BF16),
            jax.ShapeDtypeStruct((m, GATE_PAD), F32),
        ],
        scratch_shapes=[pltpu.VMEM((tm, d), BF16)],
        compiler_params=_cparams("parallel", "arbitrary"),
        name="norm_proj",
    )(x, gain.reshape(1, d), w, col_scale, w_gate)


def _matmul_res_kernel(*refs, n_lhs):
    a_refs = refs[:n_lhs]
    w_refs = refs[n_lhs:2 * n_lhs]
    r_ref, o_ref = refs[2 * n_lhs], refs[2 * n_lhs + 1]
    acc = r_ref[...]
    for a_ref, w_ref in zip(a_refs, w_refs):
        acc = acc + _dot(a_ref[...], w_ref[...])
    o_ref[...] = acc


def matmul_residual(lhs_list, w_list, res, layer, *, tm=1024, tn=512):
    m, n = res.shape
    tm = min(tm, m)
    n_lhs = len(lhs_list)
    in_specs = []
    for a in lhs_list:
        in_specs.append(pl.BlockSpec((tm, a.shape[1]), lambda i, j: (i, 0)))
    for w in w_list:
        in_specs.append(pl.BlockSpec((None, w.shape[1], tn), lambda i, j: (layer, 0, j)))
    in_specs.append(pl.BlockSpec((tm, tn), lambda i, j: (i, j)))
    return pl.pallas_call(
        functools.partial(_matmul_res_kernel, n_lhs=n_lhs),
        grid=(m // tm, n // tn),
        in_specs=in_specs,
        out_specs=pl.BlockSpec((tm, tn), lambda i, j: (i, j)),
        out_shape=jax.ShapeDtypeStruct((m, n), F32),
        compiler_params=_cparams("parallel", "arbitrary"),
        name="matmul_residual",
    )(*lhs_list, *w_list, res)


def _out_proj_norm_kernel(*refs, n_lhs):
    a_refs = refs[:n_lhs]
    w_refs = refs[n_lhs:2 * n_lhs]
    r_ref, g_ref, h_ref, xn_ref = refs[2 * n_lhs:2 * n_lhs + 4]
    acc = r_ref[...]
    for a_ref, w_ref in zip(a_refs, w_refs):
        acc = acc + _dot(a_ref[...], w_ref[...])
    h_ref[...] = acc
    xn_ref[...] = _rms_norm_rows(acc, g_ref[...]).astype(xn_ref.dtype)


def out_proj_norm(lhs_list, w_list, res, gain, *, tm=512):
    m, n = res.shape
    tm = min(tm, m)
    n_lhs = len(lhs_list)
    in_specs = [pl.BlockSpec((tm, a.shape[1]), lambda i: (i, 0)) for a in lhs_list]
    in_specs += [pl.BlockSpec(w.shape, lambda i: (0, 0)) for w in w_list]
    in_specs += [pl.BlockSpec((tm, n), lambda i: (i, 0)), pl.BlockSpec((1, n), lambda i: (0, 0))]
    return pl.pallas_call(
        functools.partial(_out_proj_norm_kernel, n_lhs=n_lhs),
        grid=(m // tm,),
        in_specs=in_specs,
        out_specs=[pl.BlockSpec((tm, n), lambda i: (i, 0)), pl.BlockSpec((tm, n), lambda i: (i, 0))],
        out_shape=[jax.ShapeDtypeStruct((m, n), F32), jax.ShapeDtypeStruct((m, n), BF16)],
        compiler_params=_cparams("parallel"),
        name="out_proj_norm",
    )(*lhs_list, *w_list, res, gain.reshape(1, n))


def _ffn_up_kernel(xn_ref, wg_ref, wu_ref, o_ref):
    xn = xn_ref[...]
    gate = _dot(xn, wg_ref[...])
    up = _dot(xn, wu_ref[...])
    o_ref[...] = (gate * _sigmoid(gate) * up).astype(o_ref.dtype)


def ffn_up(xn, w_gate, w_up, layer, *, tm=1024, tn=512):
    m, d = xn.shape
    n = w_gate.shape[2]
    tm = min(tm, m)
    return pl.pallas_call(
        _ffn_up_kernel,
        grid=(m // tm, n // tn),
        in_specs=[
            pl.BlockSpec((tm, d), lambda i, j: (i, 0)),
            pl.BlockSpec((None, d, tn), lambda i, j: (layer, 0, j)),
            pl.BlockSpec((None, d, tn), lambda i, j: (layer, 0, j)),
        ],
        out_specs=pl.BlockSpec((tm, tn), lambda i, j: (i, j)),
        out_shape=jax.ShapeDtypeStruct((m, n), BF16),
        compiler_params=_cparams("parallel", "arbitrary"),
        name="ffn_up",
    )(xn, w_gate, w_up)


def _cast_kernel(x_ref, o_ref):
    o_ref[...] = x_ref[...].astype(o_ref.dtype)


def cast_bf16(w, *, rows=512):
    lead, r, c = w.shape
    flat = w.reshape(lead * r, c)
    out = pl.pallas_call(
        _cast_kernel,
        grid=(lead * r // rows,),
        in_specs=[pl.BlockSpec((rows, c), lambda i: (i, 0))],
        out_specs=pl.BlockSpec((rows, c), lambda i: (i, 0)),
        out_shape=jax.ShapeDtypeStruct(flat.shape, BF16),
        compiler_params=_cparams("parallel"),
        name="cast_bf16",
    )(flat)
    return out.reshape(lead, r, c)


def _final_norm_kernel(x_ref, g_ref, o_ref):
    o_ref[...] = _rms_norm_rows(x_ref[...], g_ref[...])


def final_norm(x, gain, *, tm=1024):
    m, d = x.shape
    tm = min(tm, m)
    return pl.pallas_call(
        _final_norm_kernel,
        grid=(m // tm,),
        in_specs=[pl.BlockSpec((tm, d), lambda i: (i, 0)),
                  pl.BlockSpec((1, d), lambda i: (0, 0))],
        out_specs=pl.BlockSpec((tm, d), lambda i: (i, 0)),
        out_shape=jax.ShapeDtypeStruct((m, d), F32),
        compiler_params=_cparams("parallel"),
        name="final_norm",
    )(x, gain.reshape(1, d))


def _swa_kernel(sinks_ref, q_ref, kc_ref, kp_ref, vc_ref, vp_ref, o_ref, bias_ref):
    blk, dh = SWA_BLOCK, HEAD_DIM
    nkv = SWA_KV_HEADS
    keys = 2 * blk

    @pl.when((pl.program_id(0) == 0) & (pl.program_id(1) == 0))
    def _():
        kj = lax.broadcasted_iota(jnp.int32, (keys, blk), 0)
        qi = lax.broadcasted_iota(jnp.int32, (keys, blk), 1) + blk
        dist = qi - kj
        in_window = (dist >= 0) & (dist < WINDOW)
        distf = dist.astype(F32)
        for g in range(nkv):
            for first, valid in ((0, in_window & (kj >= blk)), (1, in_window)):
                quad = [jnp.where(valid, -(2.0 ** (-8.0 * (4 * g + hh + 1) / SWA_Q_HEADS) * LOG2E)
                                  * distf, -jnp.inf) for hh in range(SWA_GROUP)]
                bias_ref[first, g] = jnp.concatenate(
                    [jnp.concatenate([quad[0], quad[2]], axis=1),
                     jnp.concatenate([quad[1], quad[3]], axis=1)], axis=0)

    which = jnp.where(pl.program_id(1) > 0, 1, 0)
    k = jnp.concatenate([kp_ref[...], kc_ref[...]], axis=0).astype(F32)
    v_t = jnp.concatenate([vp_ref[...], vc_ref[...]], axis=0).astype(F32).T
    lane = lax.broadcasted_iota(jnp.int32, (keys, LANES), 1)
    low = lane < dh

    def key_operand(g):
        kblk = k[:, (g // 2) * LANES:(g // 2 + 1) * LANES]
        swapped = pltpu.roll(kblk, dh, axis=1)
        k_low = jnp.where(low, kblk if g % 2 == 0 else swapped, 0.0)
        k_high = jnp.where(low, 0.0, swapped if g % 2 == 0 else kblk)
        return jnp.concatenate([k_low, k_high], axis=0).astype(BF16)

    def value_operand(g):
        vh_t = v_t[g * dh:(g + 1) * dh, :]
        zero = jnp.zeros_like(vh_t)
        return jnp.concatenate([jnp.concatenate([vh_t, zero], axis=1),
                                jnp.concatenate([zero, vh_t], axis=1)], axis=0).astype(BF16)

    scores = []
    for g in range(nkv):
        q2 = jnp.concatenate([q_ref[:, (2 * g) * LANES:(2 * g + 1) * LANES],
                              q_ref[:, (2 * g + 1) * LANES:(2 * g + 2) * LANES]], axis=0)
        scores.append(_dot_nt(key_operand(g), q2))
    probs, scales = [], []
    for g in range(nkv):
        s = scores[g] + bias_ref[which, g]
        halves, inv = [], []
        for half in range(2):
            sh = s[half * keys:(half + 1) * keys, :]
            sink = jnp.concatenate(
                [jnp.full((1, blk), sinks_ref[4 * g + 2 * pair + half] * LOG2E, F32)
                 for pair in range(2)], axis=1)
            m = jnp.maximum(jnp.max(sh, axis=0, keepdims=True), sink)
            p = jnp.exp2(sh - m)
            den = jnp.sum(p, axis=0, keepdims=True) + jnp.exp2(sink - m)
            halves.append(p.astype(BF16))
            inv.append(jnp.broadcast_to(1.0 / den, (dh, 2 * blk)))
        probs.append(jnp.concatenate(halves, axis=0))
        scales.append(jnp.concatenate(inv, axis=0))
    for g in range(nkv):
        o_t = _dot(value_operand(g), probs[g]) * scales[g]
        for pair in range(2):
            o_ref[:, (2 * g + pair) * LANES:(2 * g + pair + 1) * LANES] = (
                o_t[:, pair * blk:(pair + 1) * blk].T.astype(o_ref.dtype))


def swa_attention(proj, sinks, batch, seq):
    blk = SWA_BLOCK
    nb = seq // blk
    m = batch * seq
    qw = SWA_Q_HEADS * HEAD_DIM
    kw = SWA_KV_HEADS * HEAD_DIM
    cur = lambda b, n: b * nb + n
    prev = lambda b, n: b * nb + jnp.maximum(n - 1, 0)
    return pl.pallas_call(
        _swa_kernel,
        grid=(batch, nb),
        in_specs=[
            pl.BlockSpec(memory_space=pltpu.SMEM),
            pl.BlockSpec((blk, qw), lambda b, n: (cur(b, n), EVEN_QS // qw)),
            pl.BlockSpec((blk, kw), lambda b, n: (cur(b, n), EVEN_KS // kw)),
            pl.BlockSpec((blk, kw), lambda b, n: (prev(b, n), EVEN_KS // kw)),
            pl.BlockSpec((blk, kw), lambda b, n: (cur(b, n), EVEN_VS // kw)),
            pl.BlockSpec((blk, kw), lambda b, n: (prev(b, n), EVEN_VS // kw)),
        ],
        out_specs=pl.BlockSpec((blk, qw), lambda b, n: (cur(b, n), 0)),
        out_shape=jax.ShapeDtypeStruct((m, qw), BF16),
        scratch_shapes=[pltpu.VMEM((2, SWA_KV_HEADS, 4 * blk, 2 * blk), F32)],
        compiler_params=_cparams("arbitrary", "arbitrary"),
        name="swa_attention",
    )(sinks, proj, proj, proj, proj, proj)


def _mlstm_kernel(g_ref, bias_ref, gain_ref, q_ref, k_ref, v_ref, og_ref, o_ref,
                  c_ref, m_ref):
    L = MLSTM_CHUNK
    dk, dv = MLSTM_QK_DIM, MLSTM_V_DIM
    nh = MLSTM_HEADS

    @pl.when(pl.program_id(1) == 0)
    def _():
        c_ref[...] = jnp.zeros_like(c_ref)
        m_ref[...] = jnp.zeros_like(m_ref)

    a_all = GATE_SOFTCAP * jnp.tanh((g_ref[...] + bias_ref[...]) / GATE_SOFTCAP)
    lf_all = _log_sigmoid(a_all)
    row = lax.broadcasted_iota(jnp.int32, (L, L), 0)
    col = lax.broadcasted_iota(jnp.int32, (L, L), 1)
    causal = row >= col
    tri = jnp.where(causal, 1.0, 0.0).astype(BF16)
    hi, mid, lo = _split3_bf16(lf_all)
    b_all = _dot(tri, hi) + _dot(tri, mid) + _dot(tri, lo)
    a_all_t = a_all.T
    b_all_t = b_all.T
    ones = jnp.ones((L, LANES), BF16)

    for h in range(nh):
        li_c = a_all[:, h:h + 1]
        li_r = a_all_t[h:h + 1, :]
        b_c = b_all[:, nh + h:nh + h + 1]
        b_r = b_all_t[nh + h:nh + h + 1, :]
        b_last = b_r[:, L - 1:L]
        m_prev = m_ref[h:h + 1, 0:1]

        a_c = b_last - b_c + li_c
        a_max = jnp.max(a_c, axis=0, keepdims=True)
        w_c = jnp.exp(a_c - a_max)

        log_d = jnp.where(causal, b_c - b_r + li_r, -jnp.inf)
        g_c = b_c + m_prev
        m_t = jnp.maximum(g_c, jnp.max(log_d, axis=-1, keepdims=True))
        d_mat = jnp.exp(log_d - m_t)
        inter = jnp.exp(g_c - m_t)

        q = q_ref[:, h * dk:(h + 1) * dk]
        k = k_ref[:, h * dk:(h + 1) * dk]
        v_aug = jnp.concatenate([v_ref[:, h * dv:(h + 1) * dv], ones], axis=1)
        p = (_dot_nt(q, k) * d_mat).astype(BF16)
        c_prev = c_ref[h]
        num = _dot(p, v_aug) + inter * _dot(q, c_prev.astype(BF16))
        den = num[:, dv:dv + 1]
        hh = num[:, :dv] / jnp.maximum(jnp.abs(den), jnp.exp(-m_t))
        hn = _rms_norm_rows(hh, gain_ref[:, h * dv:(h + 1) * dv])
        gate = _sigmoid(og_ref[:, h * dv:(h + 1) * dv].astype(F32))
        o_ref[:, h * dv:(h + 1) * dv] = (gate * hn).astype(o_ref.dtype)

        m_new = jnp.maximum(b_last + m_prev, a_max)
        decay = jnp.exp(b_last + m_prev - m_new)
        scl = jnp.exp(a_max - m_new)
        kw_t = (k.astype(F32) * w_c).T.astype(BF16)
        c_ref[h] = decay * c_prev + scl * _dot(kw_t, v_aug)
        m_ref[h:h + 1, :] = jnp.broadcast_to(m_new, (1, LANES))


def mlstm(proj, gates, bias_row, head_gain, batch, seq):
    L = MLSTM_CHUNK
    nc = seq // L
    m = batch * seq
    nh, dk, dv = MLSTM_HEADS, MLSTM_QK_DIM, MLSTM_V_DIM
    rows = lambda b, c: b * nc + c
    return pl.pallas_call(
        _mlstm_kernel,
        grid=(batch, nc),
        in_specs=[
            pl.BlockSpec((L, GATE_PAD), lambda b, c: (rows(b, c), 0)),
            pl.BlockSpec((1, GATE_PAD), lambda b, c: (0, 0)),
            pl.BlockSpec((1, nh * dv), lambda b, c: (0, 0)),
            pl.BlockSpec((L, nh * dk), lambda b, c: (rows(b, c), EVEN_QM // (nh * dk))),
            pl.BlockSpec((L, nh * dk), lambda b, c: (rows(b, c), EVEN_KM // (nh * dk))),
            pl.BlockSpec((L, nh * dv), lambda b, c: (rows(b, c), EVEN_VM // (nh * dv))),
            pl.BlockSpec((L, nh * dv), lambda b, c: (rows(b, c), EVEN_OM // (nh * dv))),
        ],
        out_specs=pl.BlockSpec((L, nh * dv), lambda b, c: (rows(b, c), 0)),
        out_shape=jax.ShapeDtypeStruct((m, nh * dv), BF16),
        scratch_shapes=[pltpu.VMEM((nh, dk, dv + LANES), F32),
                        pltpu.VMEM((8, LANES), F32)],
        compiler_params=_cparams("parallel", "arbitrary"),
        name="mlstm",
    )(gates, bias_row, head_gain.reshape(1, nh * dv), proj, proj, proj, proj)


BIAS_COLS = 3


def _fox_selectors():
    width = FOX_HEADS * LANES
    sel_k = np.zeros((BIAS_COLS * GATE_PAD, width), np.float32)
    sel_q = np.zeros((BIAS_COLS * GATE_PAD, width), np.float32)
    const_k = np.zeros((1, width), np.float32)
    const_q = np.zeros((1, width), np.float32)
    for h in range(FOX_HEADS):
        for part in range(BIAS_COLS):
            sel_k[part * GATE_PAD + h, h * LANES + part] = 1.0
            sel_q[part * GATE_PAD + h, h * LANES + BIAS_COLS + part] = 1.0
            const_k[0, h * LANES + BIAS_COLS + part] = 1.0
            const_q[0, h * LANES + part] = -1.0
    return (jnp.asarray(sel_k, BF16), jnp.asarray(sel_q, BF16),
            jnp.asarray(const_k), jnp.asarray(const_q))


FOX_BLOCK = 512
STAT_ROWS = 8


def _head_norms_sq(x_ref, gsum):
    return jnp.max(_dot(jnp.square(x_ref[...]), gsum), axis=0, keepdims=True)


def _fox_bias_kernel(g_ref, bias_ref, selk_ref, selq_ref, ck_ref, cq_ref, gsum_ref, q_ref, k_ref,
                     kb_ref, qb_ref, stat_ref, carry_ref, *, ts):
    @pl.when(pl.program_id(1) == 0)
    def _():
        carry_ref[...] = jnp.zeros_like(carry_ref)

    lf = _log_sigmoid(g_ref[...] + bias_ref[...])
    row = lax.broadcasted_iota(jnp.int32, (ts, ts), 0)
    col = lax.broadcasted_iota(jnp.int32, (ts, ts), 1)
    tri = jnp.where(row >= col, 1.0, 0.0).astype(BF16)
    hi, mid, lo = _split3_bf16(lf)
    c = _dot(tri, hi) + _dot(tri, mid) + _dot(tri, lo) + carry_ref[0:1, :]
    carry_ref[...] = jnp.broadcast_to(c[ts - 1:ts, :], carry_ref.shape)
    c2 = c * LOG2E
    parts = jnp.concatenate(_split3_bf16(c2), axis=1)
    kb_ref[...] = (_dot(parts, selk_ref[...]) + ck_ref[...]).astype(BF16)
    qb_ref[...] = (_dot(parts, selq_ref[...]) + cq_ref[...]).astype(BF16)
    gsum = gsum_ref[...]
    stat_ref[0] = jnp.concatenate(
        [_head_norms_sq(q_ref, gsum), _head_norms_sq(k_ref, gsum), c2[0:1, :], c2[ts - 1:ts, :],
         jnp.zeros((STAT_ROWS - 4, GATE_PAD), F32)], axis=0)


def fox_bias_columns(gates, bias_row, proj, batch, seq):
    ts = FOX_BLOCK
    ns = seq // ts
    m = batch * seq
    width = FOX_HEADS * LANES
    sel_k, sel_q, const_k, const_q = _fox_selectors()
    gsum = np.zeros((width, GATE_PAD), np.float32)
    for h in range(FOX_HEADS):
        gsum[h * FOX_HEAD_DIM:(h + 1) * FOX_HEAD_DIM, h] = 1.0
    gsum = jnp.asarray(gsum, BF16)
    whole = lambda shape: pl.BlockSpec(shape, lambda b, s: (0, 0))
    rows = lambda b, s: b * ns + s
    return pl.pallas_call(
        functools.partial(_fox_bias_kernel, ts=ts),
        grid=(batch, ns),
        in_specs=[pl.BlockSpec((ts, GATE_PAD), lambda b, s: (rows(b, s), 0)),
                  whole((1, GATE_PAD)),
                  whole(sel_k.shape), whole(sel_q.shape),
                  whole((1, width)), whole((1, width)), whole(gsum.shape),
                  pl.BlockSpec((ts, width), lambda b, s: (rows(b, s), 0)),
                  pl.BlockSpec((ts, width), lambda b, s: (rows(b, s), 1))],
        out_specs=[pl.BlockSpec((ts, width), lambda b, s: (rows(b, s), 0)),
                   pl.BlockSpec((ts, width), lambda b, s: (rows(b, s), 0)),
                   pl.BlockSpec((1, STAT_ROWS, GATE_PAD), lambda b, s: (rows(b, s), 0, 0))],
        out_shape=[jax.ShapeDtypeStruct((m, width), BF16),
                   jax.ShapeDtypeStruct((m, width), BF16),
                   jax.ShapeDtypeStruct((batch * ns, STAT_ROWS, GATE_PAD), F32)],
        scratch_shapes=[pltpu.VMEM((8, GATE_PAD), F32)],
        compiler_params=_cparams("parallel", "arbitrary"),
        name="fox_bias_columns",
    )(gates, bias_row, sel_k, sel_q, const_k, const_q, gsum, proj, proj)


FOX_SKIP_BITS = 170.0
FOX_NORM_MARGIN = 1.02


def fox_first_tiles(stats, batch, seq):
    nq = seq // FOX_BLOCK
    st = stats.reshape(batch, nq, STAT_ROWS, GATE_PAD)[..., :FOX_HEADS]
    qn = jnp.sqrt(st[:, :, 0, :]) * FOX_NORM_MARGIN
    kn = jnp.sqrt(st[:, :, 1, :])
    c_first, c_last = st[:, :, 2, :], st[:, :, 3, :]
    upper = (qn[:, :, None, :] * kn[:, None, :, :]
             + c_first[:, :, None, :] - c_last[:, None, :, :])
    row_max_lower = -(qn * kn)[:, :, None, :]
    blocks = jnp.arange(nq)
    needed = (upper - row_max_lower >= -FOX_SKIP_BITS) | (blocks[None, None, :, None] >= blocks[None, :, None, None])
    first = jnp.argmax(needed, axis=2).astype(jnp.int32)
    return jnp.transpose(first, (0, 2, 1)).reshape(-1)


FOX_GROUP = 4


def _fox_attn_kernel(first_ref, q_ref, qb_ref, k_ref, kb_ref, v_ref, o_ref,
                     vt_ref, m_ref, l_ref, acc_ref, *, blk, nblk):
    qi = pl.program_id(2)
    group = FOX_GROUP

    @pl.when(qi == 0)
    def _():
        for c in range(nblk):
            vt_ref[c] = v_ref[c * blk:(c + 1) * blk, :].astype(F32).T.astype(BF16)

    q_aug_t = jnp.concatenate([q_ref[...].astype(F32).T, qb_ref[...].astype(F32).T],
                              axis=0).astype(BF16)
    m_ref[...] = jnp.full_like(m_ref, -jnp.inf)
    l_ref[...] = jnp.zeros_like(l_ref)
    acc_ref[...] = jnp.zeros_like(acc_ref)

    def tiles(kbs, masks):
        scores = []
        for kb, masked in zip(kbs, masks):
            start = pl.multiple_of(kb * blk, blk)
            k_aug = jnp.concatenate([k_ref[pl.ds(start, blk), :], kb_ref[pl.ds(start, blk), :]],
                                    axis=1)
            s = _dot(k_aug, q_aug_t)
            if masked:
                key = lax.broadcasted_iota(jnp.int32, (blk, blk), 0)
                qry = lax.broadcasted_iota(jnp.int32, (blk, blk), 1)
                s = jnp.where(key <= qry, s, -jnp.inf)
            scores.append(s)
        m_run = m_ref[...]
        l_run = l_ref[...]
        alphas, probs = [], []
        for s in scores:
            m_new = jnp.maximum(m_run, jnp.max(s, axis=0, keepdims=True))
            alpha = jnp.exp2(m_run - m_new)
            p = jnp.exp2(s - m_new)
            l_run = alpha * l_run + jnp.sum(p, axis=0, keepdims=True)
            alphas.append(alpha)
            probs.append(p.astype(BF16))
            m_run = m_new
        acc = acc_ref[...]
        for kb, alpha, p in zip(kbs, alphas, probs):
            acc = alpha * acc + _dot(vt_ref[kb], p)
        acc_ref[...] = acc
        l_ref[...] = l_run
        m_ref[...] = m_run

    first = first_ref[(pl.program_id(0) * pl.num_programs(1) + pl.program_id(1)) * nblk + qi]
    n_full = qi - first

    def body(j, carry):
        tiles([first + group * j + t for t in range(group)], [False] * group)
        return carry

    lax.fori_loop(0, n_full // group, body, 0)

    for r in range(group):
        @pl.when(n_full % group == r)
        def _(r=r):
            tiles([qi - r + t for t in range(r + 1)], [False] * r + [True])

    o_ref[...] = (acc_ref[...] / l_ref[...]).T.astype(o_ref.dtype)


def fox_attention(proj, key_bias, query_bias, first_tiles, batch, seq):
    blk = FOX_BLOCK
    nq = seq // blk
    m = batch * seq
    nh, dh = FOX_HEADS, FOX_HEAD_DIM
    return pl.pallas_call(
        functools.partial(_fox_attn_kernel, blk=blk, nblk=nq),
        grid=(batch, nh, nq),
        in_specs=[
            pl.BlockSpec(memory_space=pltpu.SMEM),
            pl.BlockSpec((blk, dh), lambda b, h, i: (b * nq + i, h)),
            pl.BlockSpec((blk, LANES), lambda b, h, i: (b * nq + i, h)),
            pl.BlockSpec((seq, dh), lambda b, h, i: (b, nh + h)),
            pl.BlockSpec((seq, LANES), lambda b, h, i: (b, h)),
            pl.BlockSpec((seq, dh), lambda b, h, i: (b, 2 * nh + h)),
        ],
        out_specs=pl.BlockSpec((blk, dh), lambda b, h, i: (b * nq + i, h)),
        out_shape=jax.ShapeDtypeStruct((m, nh * dh), BF16),
        scratch_shapes=[pltpu.VMEM((nq, dh, blk), BF16),
                        pltpu.VMEM((1, blk), F32), pltpu.VMEM((1, blk), F32),
                        pltpu.VMEM((dh, blk), F32)],
        compiler_params=_cparams("parallel", "parallel", "arbitrary"),
        name="fox_attention",
    )(first_tiles, proj, query_bias, proj, key_bias, proj)


def _pad_cols(t, width):
    return jnp.pad(t, ((0, 0), (0, width - t.shape[1])))


def _even_col_scale():
    cs = jnp.ones((EVEN_MAIN,), F32)
    cs = cs.at[EVEN_QS:EVEN_QS + SWA_Q_HEADS * HEAD_DIM].set(HEAD_DIM ** -0.5 * LOG2E)
    cs = cs.at[EVEN_QM:EVEN_QM + MLSTM_HEADS * MLSTM_QK_DIM].set(MLSTM_QK_DIM ** -0.5)
    return cs.reshape(1, EVEN_MAIN)


def _fox_col_scale():
    cs = jnp.ones((FOX_MAIN,), F32)
    cs = cs.at[:FOX_HEADS * FOX_HEAD_DIM].set(FOX_HEAD_DIM ** -0.5 * LOG2E)
    return cs.reshape(1, FOX_MAIN)


def _even_mixer(h, gain, w_in, w_out, sinks, b_i, b_f, head_gain, ffn_gain, batch, seq):
    q_s, k_s, v_s, q_m, k_m, v_m, o_m, gate_cols = jnp.split(
        w_in, [1024, 1280, 1536, 2048, 2560, 3584, 4608], axis=1)
    w_main = jnp.concatenate([q_s, v_m, o_m, q_m, k_m, k_s, v_s], axis=1).astype(BF16)
    w_gate = _pad_cols(gate_cols, GATE_PAD).astype(BF16)
    proj, gates = norm_proj(h, gain, w_main, _even_col_scale(), w_gate, tn=768)
    attn = swa_attention(proj, sinks, batch, seq)
    bias_row = _pad_cols(jnp.concatenate([b_i, b_f]).reshape(1, -1), GATE_PAD)
    mem = mlstm(proj, gates, bias_row, head_gain, batch, seq)
    w_out = w_out.astype(BF16)
    split = SWA_Q_HEADS * HEAD_DIM
    return out_proj_norm([attn, mem], [w_out[:split], w_out[split:]], h, ffn_gain)


def _fox_mixer(h, gain, w_in, b_f, w_out, ffn_gain, batch, seq):
    w_main = w_in[:, :FOX_MAIN].astype(BF16)
    w_gate = _pad_cols(w_in[:, FOX_MAIN:], GATE_PAD).astype(BF16)
    proj, gates = norm_proj(h, gain, w_main, _fox_col_scale(), w_gate, tn=768)
    key_bias, query_bias, stats = fox_bias_columns(
        gates, _pad_cols(b_f.reshape(1, -1), GATE_PAD), proj, batch, seq)
    first_tiles = fox_first_tiles(stats, batch, seq)
    attn = fox_attention(proj, key_bias, query_bias, first_tiles, batch, seq)
    return out_proj_norm([attn], [w_out.astype(BF16)], h, ffn_gain)


def kernel(x, norm_mix, norm_ffn, norm_final, w_in_even, w_out_even, swa_sinks, mlstm_b_i, mlstm_b_f, mlstm_head_gain, w_in_odd, fox_b_f, w_out_odd, w_gate, w_up, w_down):
    batch, seq, d = x.shape
    depth = norm_mix.shape[0]
    h = x.reshape(batch * seq, d)
    w_gate, w_up, w_down = cast_bf16(w_gate), cast_bf16(w_up), cast_bf16(w_down)
    for layer in range(depth):
        j = layer // 2
        if layer % 2 == 0:
            h, xn = _even_mixer(h, norm_mix[layer], w_in_even[j], w_out_even[j], swa_sinks[j],
                                mlstm_b_i[j], mlstm_b_f[j], mlstm_head_gain[j], norm_ffn[layer],
                                batch, seq)
        else:
            h, xn = _fox_mixer(h, norm_mix[layer], w_in_odd[j], fox_b_f[j], w_out_odd[j],
                               norm_ffn[layer], batch, seq)
        act = ffn_up(xn, w_gate, w_up, layer)
        h = matmul_residual([act], [w_down], h, layer)
    return final_norm(h, norm_final).reshape(batch, seq, d)
```

```python
import functools

import jax
import jax.numpy as jnp
import numpy as np
from jax import lax
from jax.experimental import pallas as pl
from jax.experimental.pallas import tpu as pltpu

F32 = jnp.float32
BF16 = jnp.bfloat16

D_MODEL = 2048
RMS_EPS = 1e-5
LOG2E = 1.4426950408889634

HEAD_DIM = 64
SWA_Q_HEADS = 16
SWA_KV_HEADS = 4
SWA_GROUP = SWA_Q_HEADS // SWA_KV_HEADS
WINDOW = 128
SWA_BLOCK = 128

MLSTM_HEADS = 4
MLSTM_V_DIM = 256
MLSTM_QK_DIM = 128
MLSTM_CHUNK = 512
GATE_SOFTCAP = 15.0

FOX_HEAD_DIM = 128
FOX_HEADS = 16

FFN_HIDDEN = 5632

EVEN_MAIN = 4608
EVEN_QS, EVEN_VM, EVEN_OM = 0, 1024, 2048
EVEN_QM, EVEN_KM, EVEN_KS, EVEN_VS = 3072, 3584, 4096, 4352
FOX_MAIN = 3 * FOX_HEADS * FOX_HEAD_DIM

LANES = 128
GATE_PAD = LANES
VMEM_LIMIT = 56 * 1024 * 1024


def _cparams(*semantics, flags=None):
    return pltpu.CompilerParams(dimension_semantics=semantics,
                                vmem_limit_bytes=VMEM_LIMIT, flags=flags)


def _rms_norm_rows(x, gain):
    ms = jnp.mean(x * x, axis=-1, keepdims=True)
    return x * lax.rsqrt(ms + RMS_EPS) * gain


def _log_sigmoid(x):
    return jnp.minimum(x, 0.0) - jnp.log(1.0 + jnp.exp(-jnp.abs(x)))


def _sigmoid(x):
    return 1.0 / (1.0 + jnp.exp(-x))


def _split3_bf16(x):
    hi = x.astype(BF16)
    r1 = x - hi.astype(F32)
    mid = r1.astype(BF16)
    lo = (r1 - mid.astype(F32)).astype(BF16)
    return hi, mid, lo


def _dot(a, b):
    return jnp.dot(a, b, preferred_element_type=F32)


def _dot_nt(a, b):
    return lax.dot_general(a, b, (((1,), (1,)), ((), ())), preferred_element_type=F32)


def _norm_proj_kernel(x_ref, g_ref, w_ref, cs_ref, wg_ref, o_ref, og_ref, xn_ref, *, normed):
    lhs_ref = x_ref if normed else xn_ref

    @pl.when(pl.program_id(1) == 0)
    def _():
        if not normed:
            xn_ref[...] = _rms_norm_rows(x_ref[...], g_ref[...]).astype(BF16)
        og_ref[...] = _dot(lhs_ref[...], wg_ref[...])

    o_ref[...] = (_dot(lhs_ref[...], w_ref[...]) * cs_ref[...]).astype(o_ref.dtype)


def norm_proj(x, gain, w, col_scale, w_gate, *, tm=1024, tn=768):
    m, d = x.shape
    n = w.shape[1]
    tm = min(tm, m)
    return pl.pallas_call(
        functools.partial(_norm_proj_kernel, normed=(x.dtype == BF16)),
        grid=(m // tm, n // tn),
        in_specs=[
            pl.BlockSpec((tm, d), lambda i, j: (i, 0)),
            pl.BlockSpec((1, d), lambda i, j: (0, 0)),
            pl.BlockSpec((d, tn), lambda i, j: (0, j)),
            pl.BlockSpec((1, tn), lambda i, j: (0, j)),
            pl.BlockSpec((d, GATE_PAD), lambda i, j: (0, 0)),
        ],
        out_specs=[
            pl.BlockSpec((tm, tn), lambda i, j: (i, j)),
            pl.BlockSpec((tm, GATE_PAD), lambda i, j: (i, 0)),
        ],
        out_shape=[
            jax.ShapeDtypeStruct((m, n), BF16),
            jax.ShapeDtypeStruct((m, GATE_PAD), F32),
        ],
        scratch_shapes=[pltpu.VMEM((tm, d), BF16)],
        compiler_params=_cparams("parallel", "arbitrary"),
        name="norm_proj",
    )(x, gain.reshape(1, d), w, col_scale, w_gate)


def _matmul_res_kernel(*refs, n_lhs):
    a_refs = refs[:n_lhs]
    w_refs = refs[n_lhs:2 * n_lhs]
    r_ref, o_ref = refs[2 * n_lhs], refs[2 * n_lhs + 1]
    acc = r_ref[...]
    for a_ref, w_ref in zip(a_refs, w_refs):
        acc = acc + _dot(a_ref[...], w_ref[...])
    o_ref[...] = acc


def matmul_residual(lhs_list, w_list, res, layer, *, tm=1024, tn=512):
    m, n = res.shape
    tm = min(tm, m)
    n_lhs = len(lhs_list)
    in_specs = []
    for a in lhs_list:
        in_specs.append(pl.BlockSpec((tm, a.shape[1]), lambda i, j: (i, 0)))
    for w in w_list:
        in_specs.append(pl.BlockSpec((None, w.shape[1], tn), lambda i, j: (layer, 0, j)))
    in_specs.append(pl.BlockSpec((tm, tn), lambda i, j: (i, j)))
    return pl.pallas_call(
        functools.partial(_matmul_res_kernel, n_lhs=n_lhs),
        grid=(m // tm, n // tn),
        in_specs=in_specs,
        out_specs=pl.BlockSpec((tm, tn), lambda i, j: (i, j)),
        out_shape=jax.ShapeDtypeStruct((m, n), F32),
        compiler_params=_cparams("parallel", "arbitrary"),
        name="matmul_residual",
    )(*lhs_list, *w_list, res)


def _out_proj_norm_kernel(*refs, n_lhs):
    a_refs = refs[:n_lhs]
    w_refs = refs[n_lhs:2 * n_lhs]
    r_ref, g_ref, h_ref, xn_ref = refs[2 * n_lhs:2 * n_lhs + 4]
    acc = r_ref[...]
    for a_ref, w_ref in zip(a_refs, w_refs):
        acc = acc + _dot(a_ref[...], w_ref[...])
    h_ref[...] = acc
    xn_ref[...] = _rms_norm_rows(acc, g_ref[...]).astype(xn_ref.dtype)


def out_proj_norm(lhs_list, w_list, res, gain, *, tm=512):
    m, n = res.shape
    tm = min(tm, m)
    n_lhs = len(lhs_list)
    in_specs = [pl.BlockSpec((tm, a.shape[1]), lambda i: (i, 0)) for a in lhs_list]
    in_specs += [pl.BlockSpec(w.shape, lambda i: (0, 0)) for w in w_list]
    in_specs += [pl.BlockSpec((tm, n), lambda i: (i, 0)), pl.BlockSpec((1, n), lambda i: (0, 0))]
    return pl.pallas_call(
        functools.partial(_out_proj_norm_kernel, n_lhs=n_lhs),
        grid=(m // tm,),
        in_specs=in_specs,
        out_specs=[pl.BlockSpec((tm, n), lambda i: (i, 0)), pl.BlockSpec((tm, n), lambda i: (i, 0))],
        out_shape=[jax.ShapeDtypeStruct((m, n), F32), jax.ShapeDtypeStruct((m, n), BF16)],
        compiler_params=_cparams("parallel"),
        name="out_proj_norm",
    )(*lhs_list, *w_list, res, gain.reshape(1, n))


def _down_proj_norm_kernel(a_ref, w_ref, r_ref, g_ref, *out_refs, last):
    acc = r_ref[...] + _dot(a_ref[...], w_ref[...])
    normed = _rms_norm_rows(acc, g_ref[...])
    if last:
        out_refs[0][...] = normed
    else:
        out_refs[0][...] = acc
        out_refs[1][...] = normed.astype(out_refs[1].dtype)


def down_proj_norm(act, w_stack, layer, res, gain, *, last, tm=256):
    m, n = res.shape
    k = act.shape[1]
    tm = min(tm, m)
    rows = lambda width: pl.BlockSpec((tm, width), lambda i: (i, 0))
    out_shape = [jax.ShapeDtypeStruct((m, n), F32)]
    if not last:
        out_shape.append(jax.ShapeDtypeStruct((m, n), BF16))
    return pl.pallas_call(
        functools.partial(_down_proj_norm_kernel, last=last),
        grid=(m // tm,),
        in_specs=[rows(k),
                  pl.BlockSpec((None, k, n), lambda i: (layer, 0, 0),
                               pipeline_mode=pl.Buffered(1)),
                  rows(n),
                  pl.BlockSpec((1, n), lambda i: (0, 0))],
        out_specs=[rows(n)] * len(out_shape),
        out_shape=out_shape,
        compiler_params=_cparams("parallel"),
        name="down_proj_norm",
    )(act, w_stack, res, gain.reshape(1, n))


def _ffn_up_kernel(xn_ref, wg_ref, wu_ref, o_ref):
    xn = xn_ref[...]
    gate = _dot(xn, wg_ref[...])
    up = _dot(xn, wu_ref[...])
    o_ref[...] = (gate * _sigmoid(gate) * up).astype(o_ref.dtype)


def ffn_up(xn, w_gate, w_up, layer, *, tm=1024, tn=512):
    m, d = xn.shape
    n = w_gate.shape[2]
    tm = min(tm, m)
    return pl.pallas_call(
        _ffn_up_kernel,
        grid=(m // tm, n // tn),
        in_specs=[
            pl.BlockSpec((tm, d), lambda i, j: (i, 0)),
            pl.BlockSpec((None, d, tn), lambda i, j: (layer, 0, j)),
            pl.BlockSpec((None, d, tn), lambda i, j: (layer, 0, j)),
        ],
        out_specs=pl.BlockSpec((tm, tn), lambda i, j: (i, j)),
        out_shape=jax.ShapeDtypeStruct((m, n), BF16),
        compiler_params=_cparams("parallel", "arbitrary"),
        name="ffn_up",
    )(xn, w_gate, w_up)


def _cast_kernel(x_ref, o_ref):
    o_ref[...] = x_ref[...].astype(o_ref.dtype)


def cast_bf16(w, *, rows=512):
    lead, r, c = w.shape
    flat = w.reshape(lead * r, c)
    out = pl.pallas_call(
        _cast_kernel,
        grid=(lead * r // rows,),
        in_specs=[pl.BlockSpec((rows, c), lambda i: (i, 0))],
        out_specs=pl.BlockSpec((rows, c), lambda i: (i, 0)),
        out_shape=jax.ShapeDtypeStruct(flat.shape, BF16),
        compiler_params=_cparams("parallel"),
        name="cast_bf16",
    )(flat)
    return out.reshape(lead, r, c)


def _final_norm_kernel(x_ref, g_ref, o_ref):
    o_ref[...] = _rms_norm_rows(x_ref[...], g_ref[...])


def final_norm(x, gain, *, tm=1024):
    m, d = x.shape
    tm = min(tm, m)
    return pl.pallas_call(
        _final_norm_kernel,
        grid=(m // tm,),
        in_specs=[pl.BlockSpec((tm, d), lambda i: (i, 0)),
                  pl.BlockSpec((1, d), lambda i: (0, 0))],
        out_specs=pl.BlockSpec((tm, d), lambda i: (i, 0)),
        out_shape=jax.ShapeDtypeStruct((m, d), F32),
        compiler_params=_cparams("parallel"),
        name="final_norm",
    )(x, gain.reshape(1, d))


def _swa_kernel(sinks_ref, q_ref, kc_ref, kp_ref, vc_ref, vp_ref, o_ref, bias_ref):
    blk, dh = SWA_BLOCK, HEAD_DIM
    nkv = SWA_KV_HEADS
    keys = 2 * blk

    @pl.when((pl.program_id(0) == 0) & (pl.program_id(1) == 0))
    def _():
        kj = lax.broadcasted_iota(jnp.int32, (keys, blk), 0)
        qi = lax.broadcasted_iota(jnp.int32, (keys, blk), 1) + blk
        dist = qi - kj
        in_window = (dist >= 0) & (dist < WINDOW)
        distf = dist.astype(F32)
        for g in range(nkv):
            for first, valid in ((0, in_window & (kj >= blk)), (1, in_window)):
                quad = [jnp.where(valid, -(2.0 ** (-8.0 * (4 * g + hh + 1) / SWA_Q_HEADS) * LOG2E)
                                  * distf, -jnp.inf) for hh in range(SWA_GROUP)]
                bias_ref[first, g] = jnp.concatenate(
                    [jnp.concatenate([quad[0], quad[2]], axis=1),
                     jnp.concatenate([quad[1], quad[3]], axis=1)], axis=0)

    which = jnp.where(pl.program_id(1) > 0, 1, 0)
    k = jnp.concatenate([kp_ref[...], kc_ref[...]], axis=0).astype(F32)
    v_t = jnp.concatenate([vp_ref[...], vc_ref[...]], axis=0).astype(F32).T
    lane = lax.broadcasted_iota(jnp.int32, (keys, LANES), 1)
    low = lane < dh

    def key_operand(g):
        kblk = k[:, (g // 2) * LANES:(g // 2 + 1) * LANES]
        swapped = pltpu.roll(kblk, dh, axis=1)
        k_low = jnp.where(low, kblk if g % 2 == 0 else swapped, 0.0)
        k_high = jnp.where(low, 0.0, swapped if g % 2 == 0 else kblk)
        return jnp.concatenate([k_low, k_high], axis=0).astype(BF16)

    def value_operand(g):
        vh_t = v_t[g * dh:(g + 1) * dh, :]
        zero = jnp.zeros_like(vh_t)
        return jnp.concatenate([jnp.concatenate([vh_t, zero], axis=1),
                                jnp.concatenate([zero, vh_t], axis=1)], axis=0).astype(BF16)

    scores = []
    for g in range(nkv):
        q2 = jnp.concatenate([q_ref[:, (2 * g) * LANES:(2 * g + 1) * LANES],
                              q_ref[:, (2 * g + 1) * LANES:(2 * g + 2) * LANES]], axis=0)
        scores.append(_dot_nt(key_operand(g), q2))
    probs, scales = [], []
    for g in range(nkv):
        s = scores[g] + bias_ref[which, g]
        halves, inv = [], []
        for half in range(2):
            sh = s[half * keys:(half + 1) * keys, :]
            sink = jnp.concatenate(
                [jnp.full((1, blk), sinks_ref[4 * g + 2 * pair + half] * LOG2E, F32)
                 for pair in range(2)], axis=1)
            m = jnp.maximum(jnp.max(sh, axis=0, keepdims=True), sink)
            p = jnp.exp2(sh - m)
            den = jnp.sum(p, axis=0, keepdims=True) + jnp.exp2(sink - m)
            halves.append(p.astype(BF16))
            inv.append(jnp.broadcast_to(1.0 / den, (dh, 2 * blk)))
        probs.append(jnp.concatenate(halves, axis=0))
        scales.append(jnp.concatenate(inv, axis=0))
    for g in range(nkv):
        o_t = _dot(value_operand(g), probs[g]) * scales[g]
        for pair in range(2):
            o_ref[:, (2 * g + pair) * LANES:(2 * g + pair + 1) * LANES] = (
                o_t[:, pair * blk:(pair + 1) * blk].T.astype(o_ref.dtype))


def swa_attention(proj, sinks, batch, seq):
    blk = SWA_BLOCK
    nb = seq // blk
    m = batch * seq
    qw = SWA_Q_HEADS * HEAD_DIM
    kw = SWA_KV_HEADS * HEAD_DIM
    cur = lambda b, n: b * nb + n
    prev = lambda b, n: b * nb + jnp.maximum(n - 1, 0)
    return pl.pallas_call(
        _swa_kernel,
        grid=(batch, nb),
        in_specs=[
            pl.BlockSpec(memory_space=pltpu.SMEM),
            pl.BlockSpec((blk, qw), lambda b, n: (cur(b, n), EVEN_QS // qw)),
            pl.BlockSpec((blk, kw), lambda b, n: (cur(b, n), EVEN_KS // kw)),
            pl.BlockSpec((blk, kw), lambda b, n: (prev(b, n), EVEN_KS // kw)),
            pl.BlockSpec((blk, kw), lambda b, n: (cur(b, n), EVEN_VS // kw)),
            pl.BlockSpec((blk, kw), lambda b, n: (prev(b, n), EVEN_VS // kw)),
        ],
        out_specs=pl.BlockSpec((blk, qw), lambda b, n: (cur(b, n), 0)),
        out_shape=jax.ShapeDtypeStruct((m, qw), BF16),
        scratch_shapes=[pltpu.VMEM((2, SWA_KV_HEADS, 4 * blk, 2 * blk), F32)],
        compiler_params=_cparams("arbitrary", "arbitrary"),
        name="swa_attention",
    )(sinks, proj, proj, proj, proj, proj)


def _mlstm_kernel(g_ref, bias_ref, gain_ref, q_ref, k_ref, v_ref, og_ref, o_ref,
                  c_ref, m_ref):
    L = MLSTM_CHUNK
    dk, dv = MLSTM_QK_DIM, MLSTM_V_DIM
    nh = MLSTM_HEADS

    @pl.when(pl.program_id(1) == 0)
    def _():
        c_ref[...] = jnp.zeros_like(c_ref)
        m_ref[...] = jnp.zeros_like(m_ref)

    a_all = GATE_SOFTCAP * jnp.tanh((g_ref[...] + bias_ref[...]) / GATE_SOFTCAP)
    lf_all = _log_sigmoid(a_all)
    row = lax.broadcasted_iota(jnp.int32, (L, L), 0)
    col = lax.broadcasted_iota(jnp.int32, (L, L), 1)
    causal = row >= col
    tri = jnp.where(causal, 1.0, 0.0).astype(BF16)
    hi, mid, lo = _split3_bf16(lf_all)
    b_all = _dot(tri, hi) + _dot(tri, mid) + _dot(tri, lo)
    a_all_t = a_all.T
    b_all_t = b_all.T
    ones = jnp.ones((L, LANES), BF16)

    for h in range(nh):
        li_c = a_all[:, h:h + 1]
        li_r = a_all_t[h:h + 1, :]
        b_c = b_all[:, nh + h:nh + h + 1]
        b_r = b_all_t[nh + h:nh + h + 1, :]
        b_last = b_r[:, L - 1:L]
        m_prev = m_ref[h:h + 1, 0:1]

        a_c = b_last - b_c + li_c
        a_max = jnp.max(a_c, axis=0, keepdims=True)
        w_c = jnp.exp(a_c - a_max)

        log_d = jnp.where(causal, b_c - b_r + li_r, -jnp.inf)
        g_c = b_c + m_prev
        m_t = jnp.maximum(g_c, jnp.max(log_d, axis=-1, keepdims=True))
        d_mat = jnp.exp(log_d - m_t)
        inter = jnp.exp(g_c - m_t)

        q = q_ref[:, h * dk:(h + 1) * dk]
        k = k_ref[:, h * dk:(h + 1) * dk]
        v_aug = jnp.concatenate([v_ref[:, h * dv:(h + 1) * dv], ones], axis=1)
        p = (_dot_nt(q, k) * d_mat).astype(BF16)
        c_prev = c_ref[h]
        num = _dot(p, v_aug) + inter * _dot(q, c_prev.astype(BF16))
        den = num[:, dv:dv + 1]
        hh = num[:, :dv] / jnp.maximum(jnp.abs(den), jnp.exp(-m_t))
        hn = _rms_norm_rows(hh, gain_ref[:, h * dv:(h + 1) * dv])
        gate = _sigmoid(og_ref[:, h * dv:(h + 1) * dv].astype(F32))
        o_ref[:, h * dv:(h + 1) * dv] = (gate * hn).astype(o_ref.dtype)

        m_new = jnp.maximum(b_last + m_prev, a_max)
        decay = jnp.exp(b_last + m_prev - m_new)
        scl = jnp.exp(a_max - m_new)
        kw_t = (k.astype(F32) * w_c).T.astype(BF16)
        c_ref[h] = decay * c_prev + scl * _dot(kw_t, v_aug)
        m_ref[h:h + 1, :] = jnp.broadcast_to(m_new, (1, LANES))


def mlstm(proj, gates, bias_row, head_gain, batch, seq):
    L = MLSTM_CHUNK
    nc = seq // L
    m = batch * seq
    nh, dk, dv = MLSTM_HEADS, MLSTM_QK_DIM, MLSTM_V_DIM
    rows = lambda b, c: b * nc + c
    return pl.pallas_call(
        _mlstm_kernel,
        grid=(batch, nc),
        in_specs=[
            pl.BlockSpec((L, GATE_PAD), lambda b, c: (rows(b, c), 0)),
            pl.BlockSpec((1, GATE_PAD), lambda b, c: (0, 0)),
            pl.BlockSpec((1, nh * dv), lambda b, c: (0, 0)),
            pl.BlockSpec((L, nh * dk), lambda b, c: (rows(b, c), EVEN_QM // (nh * dk))),
            pl.BlockSpec((L, nh * dk), lambda b, c: (rows(b, c), EVEN_KM // (nh * dk))),
            pl.BlockSpec((L, nh * dv), lambda b, c: (rows(b, c), EVEN_VM // (nh * dv))),
            pl.BlockSpec((L, nh * dv), lambda b, c: (rows(b, c), EVEN_OM // (nh * dv))),
        ],
        out_specs=pl.BlockSpec((L, nh * dv), lambda b, c: (rows(b, c), 0)),
        out_shape=jax.ShapeDtypeStruct((m, nh * dv), BF16),
        scratch_shapes=[pltpu.VMEM((nh, dk, dv + LANES), F32),
                        pltpu.VMEM((8, LANES), F32)],
        compiler_params=_cparams("parallel", "arbitrary"),
        name="mlstm",
    )(gates, bias_row, head_gain.reshape(1, nh * dv), proj, proj, proj, proj)


BIAS_COLS = 3


def _fox_selectors():
    width = FOX_HEADS * LANES
    sel_k = np.zeros((BIAS_COLS * GATE_PAD, width), np.float32)
    sel_q = np.zeros((BIAS_COLS * GATE_PAD, width), np.float32)
    const_k = np.zeros((1, width), np.float32)
    const_q = np.zeros((1, width), np.float32)
    for h in range(FOX_HEADS):
        for part in range(BIAS_COLS):
            sel_k[part * GATE_PAD + h, h * LANES + part] = 1.0
            sel_q[part * GATE_PAD + h, h * LANES + BIAS_COLS + part] = 1.0
            const_k[0, h * LANES + BIAS_COLS + part] = 1.0
            const_q[0, h * LANES + part] = -1.0
    return (jnp.asarray(sel_k, BF16), jnp.asarray(sel_q, BF16),
            jnp.asarray(const_k), jnp.asarray(const_q))


FOX_BLOCK = 512
STAT_ROWS = 8


def _head_norms_sq(x_ref, gsum):
    return jnp.max(_dot(jnp.square(x_ref[...]), gsum), axis=0, keepdims=True)


def _fox_bias_kernel(g_ref, bias_ref, selk_ref, selq_ref, ck_ref, cq_ref, gsum_ref, q_ref, k_ref,
                     kb_ref, qb_ref, stat_ref, carry_ref, *, ts):
    @pl.when(pl.program_id(1) == 0)
    def _():
        carry_ref[...] = jnp.zeros_like(carry_ref)

    lf = _log_sigmoid(g_ref[...] + bias_ref[...])
    row = lax.broadcasted_iota(jnp.int32, (ts, ts), 0)
    col = lax.broadcasted_iota(jnp.int32, (ts, ts), 1)
    tri = jnp.where(row >= col, 1.0, 0.0).astype(BF16)
    hi, mid, lo = _split3_bf16(lf)
    c = _dot(tri, hi) + _dot(tri, mid) + _dot(tri, lo) + carry_ref[0:1, :]
    carry_ref[...] = jnp.broadcast_to(c[ts - 1:ts, :], carry_ref.shape)
    c2 = c * LOG2E
    parts = jnp.concatenate(_split3_bf16(c2), axis=1)
    kb_ref[...] = (_dot(parts, selk_ref[...]) + ck_ref[...]).astype(BF16)
    qb_ref[...] = (_dot(parts, selq_ref[...]) + cq_ref[...]).astype(BF16)
    gsum = gsum_ref[...]
    stat_ref[0] = jnp.concatenate(
        [_head_norms_sq(q_ref, gsum), _head_norms_sq(k_ref, gsum), c2[0:1, :], c2[ts - 1:ts, :],
         jnp.zeros((STAT_ROWS - 4, GATE_PAD), F32)], axis=0)


def fox_bias_columns(gates, bias_row, proj, batch, seq):
    ts = FOX_BLOCK
    ns = seq // ts
    m = batch * seq
    width = FOX_HEADS * LANES
    sel_k, sel_q, const_k, const_q = _fox_selectors()
    gsum = np.zeros((width, GATE_PAD), np.float32)
    for h in range(FOX_HEADS):
        gsum[h * FOX_HEAD_DIM:(h + 1) * FOX_HEAD_DIM, h] = 1.0
    gsum = jnp.asarray(gsum, BF16)
    whole = lambda shape: pl.BlockSpec(shape, lambda b, s: (0, 0))
    rows = lambda b, s: b * ns + s
    return pl.pallas_call(
        functools.partial(_fox_bias_kernel, ts=ts),
        grid=(batch, ns),
        in_specs=[pl.BlockSpec((ts, GATE_PAD), lambda b, s: (rows(b, s), 0)),
                  whole((1, GATE_PAD)),
                  whole(sel_k.shape), whole(sel_q.shape),
                  whole((1, width)), whole((1, width)), whole(gsum.shape),
                  pl.BlockSpec((ts, width), lambda b, s: (rows(b, s), 0)),
                  pl.BlockSpec((ts, width), lambda b, s: (rows(b, s), 1))],
        out_specs=[pl.BlockSpec((ts, width), lambda b, s: (rows(b, s), 0)),
                   pl.BlockSpec((ts, width), lambda b, s: (rows(b, s), 0)),
                   pl.BlockSpec((1, STAT_ROWS, GATE_PAD), lambda b, s: (rows(b, s), 0, 0))],
        out_shape=[jax.ShapeDtypeStruct((m, width), BF16),
                   jax.ShapeDtypeStruct((m, width), BF16),
                   jax.ShapeDtypeStruct((batch * ns, STAT_ROWS, GATE_PAD), F32)],
        scratch_shapes=[pltpu.VMEM((8, GATE_PAD), F32)],
        compiler_params=_cparams("parallel", "arbitrary"),
        name="fox_bias_columns",
    )(gates, bias_row, sel_k, sel_q, const_k, const_q, gsum, proj, proj)


FOX_SKIP_BITS = 170.0
FOX_NORM_MARGIN = 1.02


def fox_first_tiles(stats, batch, seq):
    nq = seq // FOX_BLOCK
    st = stats.reshape(batch, nq, STAT_ROWS, GATE_PAD)[..., :FOX_HEADS]
    qn = jnp.sqrt(st[:, :, 0, :]) * FOX_NORM_MARGIN
    kn = jnp.sqrt(st[:, :, 1, :])
    c_first, c_last = st[:, :, 2, :], st[:, :, 3, :]
    upper = (qn[:, :, None, :] * kn[:, None, :, :]
             + c_first[:, :, None, :] - c_last[:, None, :, :])
    row_max_lower = -(qn * kn)[:, :, None, :]
    blocks = jnp.arange(nq)
    needed = (upper - row_max_lower >= -FOX_SKIP_BITS) | (blocks[None, None, :, None] >= blocks[None, :, None, None])
    first = jnp.argmax(needed, axis=2).astype(jnp.int32)
    return jnp.transpose(first, (0, 2, 1)).reshape(-1)


FOX_GROUP = 4


def _fox_attn_kernel(first_ref, q_ref, qb_ref, k_ref, kb_ref, v_ref, o_ref,
                     vt_ref, m_ref, l_ref, acc_ref, *, blk, nblk):
    qi = pl.program_id(2)
    group = FOX_GROUP

    @pl.when(qi == 0)
    def _():
        for c in range(nblk):
            vt_ref[c] = v_ref[c * blk:(c + 1) * blk, :].astype(F32).T.astype(BF16)

    q_aug_t = jnp.concatenate([q_ref[...].astype(F32).T, qb_ref[...].astype(F32).T],
                              axis=0).astype(BF16)
    m_ref[...] = jnp.full_like(m_ref, -jnp.inf)
    l_ref[...] = jnp.zeros_like(l_ref)
    acc_ref[...] = jnp.zeros_like(acc_ref)

    def tiles(kbs, masks):
        scores = []
        for kb, masked in zip(kbs, masks):
            start = pl.multiple_of(kb * blk, blk)
            k_aug = jnp.concatenate([k_ref[pl.ds(start, blk), :], kb_ref[pl.ds(start, blk), :]],
                                    axis=1)
            s = _dot(k_aug, q_aug_t)
            if masked:
                key = lax.broadcasted_iota(jnp.int32, (blk, blk), 0)
                qry = lax.broadcasted_iota(jnp.int32, (blk, blk), 1)
                s = jnp.where(key <= qry, s, -jnp.inf)
            scores.append(s)
        m_run = m_ref[...]
        l_run = l_ref[...]
        alphas, probs = [], []
        for s in scores:
            m_new = jnp.maximum(m_run, jnp.max(s, axis=0, keepdims=True))
            alpha = jnp.exp2(m_run - m_new)
            p = jnp.exp2(s - m_new)
            l_run = alpha * l_run + jnp.sum(p, axis=0, keepdims=True)
            alphas.append(alpha)
            probs.append(p.astype(BF16))
            m_run = m_new
        acc = acc_ref[...]
        for kb, alpha, p in zip(kbs, alphas, probs):
            acc = alpha * acc + _dot(vt_ref[kb], p)
        acc_ref[...] = acc
        l_ref[...] = l_run
        m_ref[...] = m_run

    first = first_ref[(pl.program_id(0) * pl.num_programs(1) + pl.program_id(1)) * nblk + qi]
    n_full = qi - first

    def body(j, carry):
        tiles([first + group * j + t for t in range(group)], [False] * group)
        return carry

    lax.fori_loop(0, n_full // group, body, 0)

    for r in range(group):
        @pl.when(n_full % group == r)
        def _(r=r):
            tiles([qi - r + t for t in range(r + 1)], [False] * r + [True])

    o_ref[...] = (acc_ref[...] / l_ref[...]).T.astype(o_ref.dtype)


def fox_attention(proj, key_bias, query_bias, first_tiles, batch, seq):
    blk = FOX_BLOCK
    nq = seq // blk
    m = batch * seq
    nh, dh = FOX_HEADS, FOX_HEAD_DIM
    return pl.pallas_call(
        functools.partial(_fox_attn_kernel, blk=blk, nblk=nq),
        grid=(batch, nh, nq),
        in_specs=[
            pl.BlockSpec(memory_space=pltpu.SMEM),
            pl.BlockSpec((blk, dh), lambda b, h, i: (b * nq + i, h)),
            pl.BlockSpec((blk, LANES), lambda b, h, i: (b * nq + i, h)),
            pl.BlockSpec((seq, dh), lambda b, h, i: (b, nh + h)),
            pl.BlockSpec((seq, LANES), lambda b, h, i: (b, h)),
            pl.BlockSpec((seq, dh), lambda b, h, i: (b, 2 * nh + h)),
        ],
        out_specs=pl.BlockSpec((blk, dh), lambda b, h, i: (b * nq + i, h)),
        out_shape=jax.ShapeDtypeStruct((m, nh * dh), BF16),
        scratch_shapes=[pltpu.VMEM((nq, dh, blk), BF16),
                        pltpu.VMEM((1, blk), F32), pltpu.VMEM((1, blk), F32),
                        pltpu.VMEM((dh, blk), F32)],
        compiler_params=_cparams("parallel", "parallel", "arbitrary"),
        name="fox_attention",
    )(first_tiles, proj, query_bias, proj, key_bias, proj)


def _pad_cols(t, width):
    return jnp.pad(t, ((0, 0), (0, width - t.shape[1])))


def _even_col_scale():
    cs = jnp.ones((EVEN_MAIN,), F32)
    cs = cs.at[EVEN_QS:EVEN_QS + SWA_Q_HEADS * HEAD_DIM].set(HEAD_DIM ** -0.5 * LOG2E)
    cs = cs.at[EVEN_QM:EVEN_QM + MLSTM_HEADS * MLSTM_QK_DIM].set(MLSTM_QK_DIM ** -0.5)
    return cs.reshape(1, EVEN_MAIN)


def _fox_col_scale():
    cs = jnp.ones((FOX_MAIN,), F32)
    cs = cs.at[:FOX_HEADS * FOX_HEAD_DIM].set(FOX_HEAD_DIM ** -0.5 * LOG2E)
    return cs.reshape(1, FOX_MAIN)


def _even_mixer(h, x_in, gain, w_in, w_out, sinks, b_i, b_f, head_gain, ffn_gain, batch, seq):
    q_s, k_s, v_s, q_m, k_m, v_m, o_m, gate_cols = jnp.split(
        w_in, [1024, 1280, 1536, 2048, 2560, 3584, 4608], axis=1)
    w_main = jnp.concatenate([q_s, v_m, o_m, q_m, k_m, k_s, v_s], axis=1).astype(BF16)
    w_gate = _pad_cols(gate_cols, GATE_PAD).astype(BF16)
    proj, gates = norm_proj(x_in, gain, w_main, _even_col_scale(), w_gate, tn=768)
    attn = swa_attention(proj, sinks, batch, seq)
    bias_row = _pad_cols(jnp.concatenate([b_i, b_f]).reshape(1, -1), GATE_PAD)
    mem = mlstm(proj, gates, bias_row, head_gain, batch, seq)
    w_out = w_out.astype(BF16)
    split = SWA_Q_HEADS * HEAD_DIM
    return out_proj_norm([attn, mem], [w_out[:split], w_out[split:]], h, ffn_gain)


def _fox_mixer(h, x_in, gain, w_in, b_f, w_out, ffn_gain, batch, seq):
    w_main = w_in[:, :FOX_MAIN].astype(BF16)
    w_gate = _pad_cols(w_in[:, FOX_MAIN:], GATE_PAD).astype(BF16)
    proj, gates = norm_proj(x_in, gain, w_main, _fox_col_scale(), w_gate, tn=768)
    key_bias, query_bias, stats = fox_bias_columns(
        gates, _pad_cols(b_f.reshape(1, -1), GATE_PAD), proj, batch, seq)
    first_tiles = fox_first_tiles(stats, batch, seq)
    attn = fox_attention(proj, key_bias, query_bias, first_tiles, batch, seq)
    return out_proj_norm([attn], [w_out.astype(BF16)], h, ffn_gain)


def kernel(x, norm_mix, norm_ffn, norm_final, w_in_even, w_out_even, swa_sinks, mlstm_b_i, mlstm_b_f, mlstm_head_gain, w_in_odd, fox_b_f, w_out_odd, w_gate, w_up, w_down):
    batch, seq, d = x.shape
    depth = norm_mix.shape[0]
    h = x.reshape(batch * seq, d)
    x_in = h
    w_gate, w_up, w_down = cast_bf16(w_gate), cast_bf16(w_up), cast_bf16(w_down)
    for layer in range(depth):
        j = layer // 2
        if layer % 2 == 0:
            h, xn = _even_mixer(h, x_in, norm_mix[layer], w_in_even[j], w_out_even[j],
                                swa_sinks[j], mlstm_b_i[j], mlstm_b_f[j], mlstm_head_gain[j],
                                norm_ffn[layer], batch, seq)
        else:
            h, xn = _fox_mixer(h, x_in, norm_mix[layer], w_in_odd[j], fox_b_f[j], w_out_odd[j],
                               norm_ffn[layer], batch, seq)
        act = ffn_up(xn, w_gate, w_up, layer)
        if layer + 1 < depth:
            h, x_in = down_proj_norm(act, w_down, layer, h, norm_mix[layer + 1], last=False)
        else:
            (out,) = down_proj_norm(act, w_down, layer, h, norm_final, last=True)
    return out.reshape(batch, seq, d)
```

```python
import functools

import jax
import jax.numpy as jnp
import numpy as np
from jax import lax
from jax.experimental import pallas as pl
from jax.experimental.pallas import tpu as pltpu

F32 = jnp.float32
BF16 = jnp.bfloat16

D_MODEL = 2048
RMS_EPS = 1e-5
LOG2E = 1.4426950408889634

HEAD_DIM = 64
SWA_Q_HEADS = 16
SWA_KV_HEADS = 4
SWA_GROUP = SWA_Q_HEADS // SWA_KV_HEADS
WINDOW = 128
SWA_BLOCK = 128

MLSTM_HEADS = 4
MLSTM_V_DIM = 256
MLSTM_QK_DIM = 128
MLSTM_CHUNK = 512
GATE_SOFTCAP = 15.0

FOX_HEAD_DIM = 128
FOX_HEADS = 16

FFN_HIDDEN = 5632

EVEN_MAIN = 4608
EVEN_QS, EVEN_VM, EVEN_OM = 0, 1024, 2048
EVEN_QM, EVEN_KM, EVEN_KS, EVEN_VS = 3072, 3584, 4096, 4352
FOX_MAIN = 3 * FOX_HEADS * FOX_HEAD_DIM

LANES = 128
GATE_PAD = LANES
VMEM_LIMIT = 56 * 1024 * 1024


def _cparams(*semantics, flags=None):
    return pltpu.CompilerParams(dimension_semantics=semantics,
                                vmem_limit_bytes=VMEM_LIMIT, flags=flags)


def _rms_norm_rows(x, gain):
    ms = jnp.mean(x * x, axis=-1, keepdims=True)
    return x * lax.rsqrt(ms + RMS_EPS) * gain


def _log_sigmoid(x):
    return jnp.minimum(x, 0.0) - jnp.log(1.0 + jnp.exp(-jnp.abs(x)))


def _sigmoid(x):
    return 1.0 / (1.0 + jnp.exp(-x))


def _split3_bf16(x):
    hi = x.astype(BF16)
    r1 = x - hi.astype(F32)
    mid = r1.astype(BF16)
    lo = (r1 - mid.astype(F32)).astype(BF16)
    return hi, mid, lo


def _dot(a, b):
    return jnp.dot(a, b, preferred_element_type=F32)


def _dot_nt(a, b):
    return lax.dot_general(a, b, (((1,), (1,)), ((), ())), preferred_element_type=F32)


def _norm_proj_kernel(x_ref, g_ref, w_ref, cs_ref, wg_ref, o_ref, og_ref, xn_ref, *, normed):
    lhs_ref = x_ref if normed else xn_ref

    @pl.when(pl.program_id(1) == 0)
    def _():
        if not normed:
            xn_ref[...] = _rms_norm_rows(x_ref[...], g_ref[...]).astype(BF16)
        og_ref[...] = _dot(lhs_ref[...], wg_ref[...])

    o_ref[...] = (_dot(lhs_ref[...], w_ref[...]) * cs_ref[...]).astype(o_ref.dtype)


def norm_proj(x, gain, w, col_scale, w_gate, *, tm=1024, tn=1536):
    m, d = x.shape
    n = w.shape[1]
    tm = min(tm, m)
    return pl.pallas_call(
        functools.partial(_norm_proj_kernel, normed=(x.dtype == BF16)),
        grid=(m // tm, n // tn),
        in_specs=[
            pl.BlockSpec((tm, d), lambda i, j: (i, 0)),
            pl.BlockSpec((1, d), lambda i, j: (0, 0)),
            pl.BlockSpec((d, tn), lambda i, j: (0, j)),
            pl.BlockSpec((1, tn), lambda i, j: (0, j)),
            pl.BlockSpec((d, GATE_PAD), lambda i, j: (0, 0)),
        ],
        out_specs=[
            pl.BlockSpec((tm, tn), lambda i, j: (i, j)),
            pl.BlockSpec((tm, GATE_PAD), lambda i, j: (i, 0)),
        ],
        out_shape=[
            jax.ShapeDtypeStruct((m, n), BF16),
            jax.ShapeDtypeStruct((m, GATE_PAD), F32),
        ],
        scratch_shapes=[pltpu.VMEM((tm, d), BF16)],
        compiler_params=_cparams("parallel", "arbitrary"),
        name="norm_proj",
    )(x, gain.reshape(1, d), w, col_scale, w_gate)


def _matmul_res_kernel(*refs, n_lhs):
    a_refs = refs[:n_lhs]
    w_refs = refs[n_lhs:2 * n_lhs]
    r_ref, o_ref = refs[2 * n_lhs], refs[2 * n_lhs + 1]
    acc = r_ref[...]
    for a_ref, w_ref in zip(a_refs, w_refs):
        acc = acc + _dot(a_ref[...], w_ref[...])
    o_ref[...] = acc


def matmul_residual(lhs_list, w_list, res, layer, *, tm=1024, tn=512):
    m, n = res.shape
    tm = min(tm, m)
    n_lhs = len(lhs_list)
    in_specs = []
    for a in lhs_list:
        in_specs.append(pl.BlockSpec((tm, a.shape[1]), lambda i, j: (i, 0)))
    for w in w_list:
        in_specs.append(pl.BlockSpec((None, w.shape[1], tn), lambda i, j: (layer, 0, j)))
    in_specs.append(pl.BlockSpec((tm, tn), lambda i, j: (i, j)))
    return pl.pallas_call(
        functools.partial(_matmul_res_kernel, n_lhs=n_lhs),
        grid=(m // tm, n // tn),
        in_specs=in_specs,
        out_specs=pl.BlockSpec((tm, tn), lambda i, j: (i, j)),
        out_shape=jax.ShapeDtypeStruct((m, n), F32),
        compiler_params=_cparams("parallel", "arbitrary"),
        name="matmul_residual",
    )(*lhs_list, *w_list, res)


def _out_proj_norm_kernel(*refs, n_lhs):
    a_refs = refs[:n_lhs]
    w_refs = refs[n_lhs:2 * n_lhs]
    r_ref, g_ref, h_ref, xn_ref = refs[2 * n_lhs:2 * n_lhs + 4]
    acc = r_ref[...]
    for a_ref, w_ref in zip(a_refs, w_refs):
        acc = acc + _dot(a_ref[...], w_ref[...])
    h_ref[...] = acc
    xn_ref[...] = _rms_norm_rows(acc, g_ref[...]).astype(xn_ref.dtype)


def out_proj_norm(lhs_list, w_list, res, gain, *, tm=512):
    m, n = res.shape
    tm = min(tm, m)
    n_lhs = len(lhs_list)
    in_specs = [pl.BlockSpec((tm, a.shape[1]), lambda i: (i, 0)) for a in lhs_list]
    in_specs += [pl.BlockSpec(w.shape, lambda i: (0, 0)) for w in w_list]
    in_specs += [pl.BlockSpec((tm, n), lambda i: (i, 0)), pl.BlockSpec((1, n), lambda i: (0, 0))]
    return pl.pallas_call(
        functools.partial(_out_proj_norm_kernel, n_lhs=n_lhs),
        grid=(m // tm,),
        in_specs=in_specs,
        out_specs=[pl.BlockSpec((tm, n), lambda i: (i, 0)), pl.BlockSpec((tm, n), lambda i: (i, 0))],
        out_shape=[jax.ShapeDtypeStruct((m, n), F32), jax.ShapeDtypeStruct((m, n), BF16)],
        compiler_params=_cparams("parallel"),
        name="out_proj_norm",
    )(*lhs_list, *w_list, res, gain.reshape(1, n))


def _down_proj_norm_kernel(a_ref, w_ref, r_ref, g_ref, *out_refs, last):
    acc = r_ref[...] + _dot(a_ref[...], w_ref[...])
    normed = _rms_norm_rows(acc, g_ref[...])
    if last:
        out_refs[0][...] = normed
    else:
        out_refs[0][...] = acc
        out_refs[1][...] = normed.astype(out_refs[1].dtype)


def down_proj_norm(act, w_stack, layer, res, gain, *, last, tm=256):
    m, n = res.shape
    k = act.shape[1]
    tm = min(tm, m)
    rows = lambda width: pl.BlockSpec((tm, width), lambda i: (i, 0))
    out_shape = [jax.ShapeDtypeStruct((m, n), F32)]
    if not last:
        out_shape.append(jax.ShapeDtypeStruct((m, n), BF16))
    return pl.pallas_call(
        functools.partial(_down_proj_norm_kernel, last=last),
        grid=(m // tm,),
        in_specs=[rows(k),
                  pl.BlockSpec((None, k, n), lambda i: (layer, 0, 0),
                               pipeline_mode=pl.Buffered(1)),
                  rows(n),
                  pl.BlockSpec((1, n), lambda i: (0, 0))],
        out_specs=[rows(n)] * len(out_shape),
        out_shape=out_shape,
        compiler_params=_cparams("parallel"),
        name="down_proj_norm",
    )(act, w_stack, res, gain.reshape(1, n))


def _ffn_up_kernel(xn_ref, wg_ref, wu_ref, o_ref):
    xn = xn_ref[...]
    gate = _dot(xn, wg_ref[...])
    up = _dot(xn, wu_ref[...])
    o_ref[...] = (gate * _sigmoid(gate) * up).astype(o_ref.dtype)


def ffn_up(xn, w_gate, w_up, layer, *, tm=1024, tn=512):
    m, d = xn.shape
    n = w_gate.shape[2]
    tm = min(tm, m)
    return pl.pallas_call(
        _ffn_up_kernel,
        grid=(m // tm, n // tn),
        in_specs=[
            pl.BlockSpec((tm, d), lambda i, j: (i, 0)),
            pl.BlockSpec((None, d, tn), lambda i, j: (layer, 0, j)),
            pl.BlockSpec((None, d, tn), lambda i, j: (layer, 0, j)),
        ],
        out_specs=pl.BlockSpec((tm, tn), lambda i, j: (i, j)),
        out_shape=jax.ShapeDtypeStruct((m, n), BF16),
        compiler_params=_cparams("parallel", "arbitrary"),
        name="ffn_up",
    )(xn, w_gate, w_up)


def _cast_kernel(x_ref, o_ref):
    o_ref[...] = x_ref[...].astype(o_ref.dtype)


def cast_bf16(w, *, rows=512):
    lead, r, c = w.shape
    flat = w.reshape(lead * r, c)
    out = pl.pallas_call(
        _cast_kernel,
        grid=(lead * r // rows,),
        in_specs=[pl.BlockSpec((rows, c), lambda i: (i, 0))],
        out_specs=pl.BlockSpec((rows, c), lambda i: (i, 0)),
        out_shape=jax.ShapeDtypeStruct(flat.shape, BF16),
        compiler_params=_cparams("parallel"),
        name="cast_bf16",
    )(flat)
    return out.reshape(lead, r, c)


def _final_norm_kernel(x_ref, g_ref, o_ref):
    o_ref[...] = _rms_norm_rows(x_ref[...], g_ref[...])


def final_norm(x, gain, *, tm=1024):
    m, d = x.shape
    tm = min(tm, m)
    return pl.pallas_call(
        _final_norm_kernel,
        grid=(m // tm,),
        in_specs=[pl.BlockSpec((tm, d), lambda i: (i, 0)),
                  pl.BlockSpec((1, d), lambda i: (0, 0))],
        out_specs=pl.BlockSpec((tm, d), lambda i: (i, 0)),
        out_shape=jax.ShapeDtypeStruct((m, d), F32),
        compiler_params=_cparams("parallel"),
        name="final_norm",
    )(x, gain.reshape(1, d))


def _swa_kernel(sinks_ref, q_ref, kc_ref, kp_ref, vc_ref, vp_ref, o_ref, bias_ref):
    blk, dh = SWA_BLOCK, HEAD_DIM
    nkv = SWA_KV_HEADS
    keys = 2 * blk

    @pl.when((pl.program_id(0) == 0) & (pl.program_id(1) == 0))
    def _():
        kj = lax.broadcasted_iota(jnp.int32, (keys, blk), 0)
        qi = lax.broadcasted_iota(jnp.int32, (keys, blk), 1) + blk
        dist = qi - kj
        in_window = (dist >= 0) & (dist < WINDOW)
        distf = dist.astype(F32)
        for g in range(nkv):
            for first, valid in ((0, in_window & (kj >= blk)), (1, in_window)):
                quad = [jnp.where(valid, -(2.0 ** (-8.0 * (4 * g + hh + 1) / SWA_Q_HEADS) * LOG2E)
                                  * distf, -jnp.inf) for hh in range(SWA_GROUP)]
                bias_ref[first, g] = jnp.concatenate(
                    [jnp.concatenate([quad[0], quad[2]], axis=1),
                     jnp.concatenate([quad[1], quad[3]], axis=1)], axis=0)

    which = jnp.where(pl.program_id(1) > 0, 1, 0)
    k = jnp.concatenate([kp_ref[...], kc_ref[...]], axis=0).astype(F32)
    v_t = jnp.concatenate([vp_ref[...], vc_ref[...]], axis=0).astype(F32).T
    lane = lax.broadcasted_iota(jnp.int32, (keys, LANES), 1)
    low = lane < dh

    def key_operand(g):
        kblk = k[:, (g // 2) * LANES:(g // 2 + 1) * LANES]
        swapped = pltpu.roll(kblk, dh, axis=1)
        k_low = jnp.where(low, kblk if g % 2 == 0 else swapped, 0.0)
        k_high = jnp.where(low, 0.0, swapped if g % 2 == 0 else kblk)
        return jnp.concatenate([k_low, k_high], axis=0).astype(BF16)

    def value_operand(g):
        vh_t = v_t[g * dh:(g + 1) * dh, :]
        zero = jnp.zeros_like(vh_t)
        return jnp.concatenate([jnp.concatenate([vh_t, zero], axis=1),
                                jnp.concatenate([zero, vh_t], axis=1)], axis=0).astype(BF16)

    scores = []
    for g in range(nkv):
        q2 = jnp.concatenate([q_ref[:, (2 * g) * LANES:(2 * g + 1) * LANES],
                              q_ref[:, (2 * g + 1) * LANES:(2 * g + 2) * LANES]], axis=0)
        scores.append(_dot_nt(key_operand(g), q2))
    probs, scales = [], []
    for g in range(nkv):
        s = scores[g] + bias_ref[which, g]
        halves, inv = [], []
        for half in range(2):
            sh = s[half * keys:(half + 1) * keys, :]
            sink = jnp.concatenate(
                [jnp.full((1, blk), sinks_ref[4 * g + 2 * pair + half] * LOG2E, F32)
                 for pair in range(2)], axis=1)
            m = jnp.maximum(jnp.max(sh, axis=0, keepdims=True), sink)
            p = jnp.exp2(sh - m)
            den = jnp.sum(p, axis=0, keepdims=True) + jnp.exp2(sink - m)
            halves.append(p.astype(BF16))
            inv.append(jnp.broadcast_to(1.0 / den, (dh, 2 * blk)))
        probs.append(jnp.concatenate(halves, axis=0))
        scales.append(jnp.concatenate(inv, axis=0))
    for g in range(nkv):
        o_t = _dot(value_operand(g), probs[g]) * scales[g]
        for pair in range(2):
            o_ref[:, (2 * g + pair) * LANES:(2 * g + pair + 1) * LANES] = (
                o_t[:, pair * blk:(pair + 1) * blk].T.astype(o_ref.dtype))


def swa_attention(proj, sinks, batch, seq):
    blk = SWA_BLOCK
    nb = seq // blk
    m = batch * seq
    qw = SWA_Q_HEADS * HEAD_DIM
    kw = SWA_KV_HEADS * HEAD_DIM
    cur = lambda b, n: b * nb + n
    prev = lambda b, n: b * nb + jnp.maximum(n - 1, 0)
    return pl.pallas_call(
        _swa_kernel,
        grid=(batch, nb),
        in_specs=[
            pl.BlockSpec(memory_space=pltpu.SMEM),
            pl.BlockSpec((blk, qw), lambda b, n: (cur(b, n), EVEN_QS // qw)),
            pl.BlockSpec((blk, kw), lambda b, n: (cur(b, n), EVEN_KS // kw)),
            pl.BlockSpec((blk, kw), lambda b, n: (prev(b, n), EVEN_KS // kw)),
            pl.BlockSpec((blk, kw), lambda b, n: (cur(b, n), EVEN_VS // kw)),
            pl.BlockSpec((blk, kw), lambda b, n: (prev(b, n), EVEN_VS // kw)),
        ],
        out_specs=pl.BlockSpec((blk, qw), lambda b, n: (cur(b, n), 0)),
        out_shape=jax.ShapeDtypeStruct((m, qw), BF16),
        scratch_shapes=[pltpu.VMEM((2, SWA_KV_HEADS, 4 * blk, 2 * blk), F32)],
        compiler_params=_cparams("arbitrary", "arbitrary"),
        name="swa_attention",
    )(sinks, proj, proj, proj, proj, proj)


def _mlstm_kernel(g_ref, bias_ref, gain_ref, q_ref, k_ref, v_ref, og_ref, o_ref,
                  c_ref, m_ref):
    L = MLSTM_CHUNK
    dk, dv = MLSTM_QK_DIM, MLSTM_V_DIM
    nh = MLSTM_HEADS

    @pl.when(pl.program_id(1) == 0)
    def _():
        c_ref[...] = jnp.zeros_like(c_ref)
        m_ref[...] = jnp.zeros_like(m_ref)

    a_all = GATE_SOFTCAP * jnp.tanh((g_ref[...] + bias_ref[...]) / GATE_SOFTCAP)
    lf_all = _log_sigmoid(a_all)
    row = lax.broadcasted_iota(jnp.int32, (L, L), 0)
    col = lax.broadcasted_iota(jnp.int32, (L, L), 1)
    causal = row >= col
    tri = jnp.where(causal, 1.0, 0.0).astype(BF16)
    hi, mid, lo = _split3_bf16(lf_all)
    b_all = _dot(tri, hi) + _dot(tri, mid) + _dot(tri, lo)
    a_all_t = a_all.T
    b_all_t = b_all.T
    ones = jnp.ones((L, LANES), BF16)

    for h in range(nh):
        li_c = a_all[:, h:h + 1]
        li_r = a_all_t[h:h + 1, :]
        b_c = b_all[:, nh + h:nh + h + 1]
        b_r = b_all_t[nh + h:nh + h + 1, :]
        b_last = b_r[:, L - 1:L]
        m_prev = m_ref[h:h + 1, 0:1]

        a_c = b_last - b_c + li_c
        a_max = jnp.max(a_c, axis=0, keepdims=True)
        w_c = jnp.exp(a_c - a_max)

        log_d = jnp.where(causal, b_c - b_r + li_r, -jnp.inf)
        g_c = b_c + m_prev
        m_t = jnp.maximum(g_c, jnp.max(log_d, axis=-1, keepdims=True))
        d_mat = jnp.exp(log_d - m_t)
        inter = jnp.exp(g_c - m_t)

        q = q_ref[:, h * dk:(h + 1) * dk]
        k = k_ref[:, h * dk:(h + 1) * dk]
        v_aug = jnp.concatenate([v_ref[:, h * dv:(h + 1) * dv], ones], axis=1)
        p = (_dot_nt(q, k) * d_mat).astype(BF16)
        c_prev = c_ref[h]
        num = _dot(p, v_aug) + inter * _dot(q, c_prev.astype(BF16))
        den = num[:, dv:dv + 1]
        hh = num[:, :dv] / jnp.maximum(jnp.abs(den), jnp.exp(-m_t))
        hn = _rms_norm_rows(hh, gain_ref[:, h * dv:(h + 1) * dv])
        gate = _sigmoid(og_ref[:, h * dv:(h + 1) * dv].astype(F32))
        o_ref[:, h * dv:(h + 1) * dv] = (gate * hn).astype(o_ref.dtype)

        m_new = jnp.maximum(b_last + m_prev, a_max)
        decay = jnp.exp(b_last + m_prev - m_new)
        scl = jnp.exp(a_max - m_new)
        kw_t = (k.astype(F32) * w_c).T.astype(BF16)
        c_ref[h] = decay * c_prev + scl * _dot(kw_t, v_aug)
        m_ref[h:h + 1, :] = jnp.broadcast_to(m_new, (1, LANES))


def mlstm(proj, gates, bias_row, head_gain, batch, seq):
    L = MLSTM_CHUNK
    nc = seq // L
    m = batch * seq
    nh, dk, dv = MLSTM_HEADS, MLSTM_QK_DIM, MLSTM_V_DIM
    rows = lambda b, c: b * nc + c
    return pl.pallas_call(
        _mlstm_kernel,
        grid=(batch, nc),
        in_specs=[
            pl.BlockSpec((L, GATE_PAD), lambda b, c: (rows(b, c), 0)),
            pl.BlockSpec((1, GATE_PAD), lambda b, c: (0, 0)),
            pl.BlockSpec((1, nh * dv), lambda b, c: (0, 0)),
            pl.BlockSpec((L, nh * dk), lambda b, c: (rows(b, c), EVEN_QM // (nh * dk))),
            pl.BlockSpec((L, nh * dk), lambda b, c: (rows(b, c), EVEN_KM // (nh * dk))),
            pl.BlockSpec((L, nh * dv), lambda b, c: (rows(b, c), EVEN_VM // (nh * dv))),
            pl.BlockSpec((L, nh * dv), lambda b, c: (rows(b, c), EVEN_OM // (nh * dv))),
        ],
        out_specs=pl.BlockSpec((L, nh * dv), lambda b, c: (rows(b, c), 0)),
        out_shape=jax.ShapeDtypeStruct((m, nh * dv), BF16),
        scratch_shapes=[pltpu.VMEM((nh, dk, dv + LANES), F32),
                        pltpu.VMEM((8, LANES), F32)],
        compiler_params=_cparams("parallel", "arbitrary"),
        name="mlstm",
    )(gates, bias_row, head_gain.reshape(1, nh * dv), proj, proj, proj, proj)


BIAS_COLS = 3


def _fox_selectors():
    width = FOX_HEADS * LANES
    sel_k = np.zeros((GATE_PAD, width), np.float32)
    sel_q = np.zeros((GATE_PAD, width), np.float32)
    const_k = np.zeros((1, width), np.float32)
    const_q = np.zeros((1, width), np.float32)
    for h in range(FOX_HEADS):
        for part in range(BIAS_COLS):
            sel_k[part * FOX_HEADS + h, h * LANES + part] = 1.0
            sel_q[part * FOX_HEADS + h, h * LANES + BIAS_COLS + part] = 1.0
            const_k[0, h * LANES + BIAS_COLS + part] = 1.0
            const_q[0, h * LANES + part] = -1.0
    return (jnp.asarray(sel_k, BF16), jnp.asarray(sel_q, BF16),
            jnp.asarray(const_k), jnp.asarray(const_q))


FOX_BLOCK = 512
STAT_ROWS = 8


def _head_norms_sq(x_ref, gsum):
    return jnp.max(_dot(jnp.square(x_ref[...]), gsum), axis=0, keepdims=True)


def _fox_bias_kernel(g_ref, bias_ref, selk_ref, selq_ref, ck_ref, cq_ref, gsum_ref, q_ref, k_ref,
                     kb_ref, qb_ref, stat_ref, carry_ref, *, ts):
    @pl.when(pl.program_id(1) == 0)
    def _():
        carry_ref[...] = jnp.zeros_like(carry_ref)

    lf = _log_sigmoid(g_ref[...] + bias_ref[...])
    row = lax.broadcasted_iota(jnp.int32, (ts, ts), 0)
    col = lax.broadcasted_iota(jnp.int32, (ts, ts), 1)
    tri = jnp.where(row >= col, 1.0, 0.0).astype(BF16)
    hi, mid, lo = _split3_bf16(lf)
    c = _dot(tri, hi) + _dot(tri, mid) + _dot(tri, lo) + carry_ref[0:1, :]
    carry_ref[...] = jnp.broadcast_to(c[ts - 1:ts, :], carry_ref.shape)
    c2 = c * LOG2E
    p_hi, p_mid, p_lo = (t.astype(F32) for t in _split3_bf16(c2))
    lane = lax.broadcasted_iota(jnp.int32, (ts, GATE_PAD), 1)
    parts = jnp.where(lane < FOX_HEADS, p_hi,
                      jnp.where(lane < 2 * FOX_HEADS, pltpu.roll(p_mid, FOX_HEADS, axis=1),
                                pltpu.roll(p_lo, 2 * FOX_HEADS, axis=1))).astype(BF16)
    kb_ref[...] = (_dot(parts, selk_ref[...]) + ck_ref[...]).astype(BF16)
    qb_ref[...] = (_dot(parts, selq_ref[...]) + cq_ref[...]).astype(BF16)
    gsum = gsum_ref[...]
    stat_ref[0] = jnp.concatenate(
        [_head_norms_sq(q_ref, gsum), _head_norms_sq(k_ref, gsum), c2[0:1, :], c2[ts - 1:ts, :],
         jnp.zeros((STAT_ROWS - 4, GATE_PAD), F32)], axis=0)


def fox_bias_columns(gates, bias_row, proj, batch, seq):
    ts = FOX_BLOCK
    ns = seq // ts
    m = batch * seq
    width = FOX_HEADS * LANES
    sel_k, sel_q, const_k, const_q = _fox_selectors()
    gsum = np.zeros((width, GATE_PAD), np.float32)
    for h in range(FOX_HEADS):
        gsum[h * FOX_HEAD_DIM:(h + 1) * FOX_HEAD_DIM, h] = 1.0
    gsum = jnp.asarray(gsum, BF16)
    whole = lambda shape: pl.BlockSpec(shape, lambda b, s: (0, 0))
    rows = lambda b, s: b * ns + s
    return pl.pallas_call(
        functools.partial(_fox_bias_kernel, ts=ts),
        grid=(batch, ns),
        in_specs=[pl.BlockSpec((ts, GATE_PAD), lambda b, s: (rows(b, s), 0)),
                  whole((1, GATE_PAD)),
                  whole(sel_k.shape), whole(sel_q.shape),
                  whole((1, width)), whole((1, width)), whole(gsum.shape),
                  pl.BlockSpec((ts, width), lambda b, s: (rows(b, s), 0)),
                  pl.BlockSpec((ts, width), lambda b, s: (rows(b, s), 1))],
        out_specs=[pl.BlockSpec((ts, width), lambda b, s: (rows(b, s), 0)),
                   pl.BlockSpec((ts, width), lambda b, s: (rows(b, s), 0)),
                   pl.BlockSpec((1, STAT_ROWS, GATE_PAD), lambda b, s: (rows(b, s), 0, 0))],
        out_shape=[jax.ShapeDtypeStruct((m, width), BF16),
                   jax.ShapeDtypeStruct((m, width), BF16),
                   jax.ShapeDtypeStruct((batch * ns, STAT_ROWS, GATE_PAD), F32)],
        scratch_shapes=[pltpu.VMEM((8, GATE_PAD), F32)],
        compiler_params=_cparams("parallel", "arbitrary"),
        name="fox_bias_columns",
    )(gates, bias_row, sel_k, sel_q, const_k, const_q, gsum, proj, proj)


FOX_SKIP_BITS = 170.0
FOX_NORM_MARGIN = 1.02


def fox_first_tiles(stats, batch, seq):
    nq = seq // FOX_BLOCK
    st = stats.reshape(batch, nq, STAT_ROWS, GATE_PAD)[..., :FOX_HEADS]
    qn = jnp.sqrt(st[:, :, 0, :]) * FOX_NORM_MARGIN
    kn = jnp.sqrt(st[:, :, 1, :])
    c_first, c_last = st[:, :, 2, :], st[:, :, 3, :]
    upper = (qn[:, :, None, :] * kn[:, None, :, :]
             + c_first[:, :, None, :] - c_last[:, None, :, :])
    row_max_lower = -(qn * kn)[:, :, None, :]
    blocks = jnp.arange(nq)
    needed = (upper - row_max_lower >= -FOX_SKIP_BITS) | (blocks[None, None, :, None] >= blocks[None, :, None, None])
    first = jnp.argmax(needed, axis=2).astype(jnp.int32)
    return jnp.transpose(first, (0, 2, 1)).reshape(-1)


FOX_GROUP = 4


def _fox_attn_kernel(first_ref, q_ref, qb_ref, k_ref, kb_ref, v_ref, o_ref,
                     vt_ref, m_ref, l_ref, acc_ref, *, blk, nblk):
    qi = pl.program_id(2)
    group = FOX_GROUP

    @pl.when(qi == 0)
    def _():
        for c in range(nblk):
            vt_ref[c] = v_ref[c * blk:(c + 1) * blk, :].astype(F32).T.astype(BF16)

    q_aug_t = jnp.concatenate([q_ref[...].astype(F32).T, qb_ref[...].astype(F32).T],
                              axis=0).astype(BF16)
    m_ref[...] = jnp.full_like(m_ref, -jnp.inf)
    l_ref[...] = jnp.zeros_like(l_ref)
    acc_ref[...] = jnp.zeros_like(acc_ref)

    def tiles(kbs, masks):
        n = len(kbs)
        start = pl.multiple_of(kbs[0] * blk, blk)
        k_aug = jnp.concatenate([k_ref[pl.ds(start, n * blk), :], kb_ref[pl.ds(start, n * blk), :]],
                                axis=1)
        s_all = _dot(k_aug, q_aug_t)
        scores = []
        for t, masked in enumerate(masks):
            s = s_all[t * blk:(t + 1) * blk, :]
            if masked:
                key = lax.broadcasted_iota(jnp.int32, (blk, blk), 0)
                qry = lax.broadcasted_iota(jnp.int32, (blk, blk), 1)
                s = jnp.where(key <= qry, s, -jnp.inf)
            scores.append(s)
        m_run = m_ref[...]
        l_run = l_ref[...]
        acc = acc_ref[...]
        for kb, s in zip(kbs, scores):
            m_new = jnp.maximum(m_run, jnp.max(s, axis=0, keepdims=True))
            alpha = jnp.exp2(m_run - m_new)
            p = jnp.exp2(s - m_new)
            l_run = alpha * l_run + jnp.sum(p, axis=0, keepdims=True)
            acc = alpha * acc + _dot(vt_ref[kb], p.astype(BF16))
            m_run = m_new
        acc_ref[...] = acc
        l_ref[...] = l_run
        m_ref[...] = m_run

    first = first_ref[(pl.program_id(0) * pl.num_programs(1) + pl.program_id(1)) * nblk + qi]
    n_full = qi - first

    def body(j, carry):
        tiles([first + group * j + t for t in range(group)], [False] * group)
        return carry

    lax.fori_loop(0, n_full // group, body, 0)

    for r in range(group):
        @pl.when(n_full % group == r)
        def _(r=r):
            tiles([qi - r + t for t in range(r + 1)], [False] * r + [True])

    o_ref[...] = (acc_ref[...] / l_ref[...]).T.astype(o_ref.dtype)


def fox_attention(proj, key_bias, query_bias, first_tiles, batch, seq):
    blk = FOX_BLOCK
    nq = seq // blk
    m = batch * seq
    nh, dh = FOX_HEADS, FOX_HEAD_DIM
    return pl.pallas_call(
        functools.partial(_fox_attn_kernel, blk=blk, nblk=nq),
        grid=(batch, nh, nq),
        in_specs=[
            pl.BlockSpec(memory_space=pltpu.SMEM),
            pl.BlockSpec((blk, dh), lambda b, h, i: (b * nq + i, h)),
            pl.BlockSpec((blk, LANES), lambda b, h, i: (b * nq + i, h)),
            pl.BlockSpec((seq, dh), lambda b, h, i: (b, nh + h)),
            pl.BlockSpec((seq, LANES), lambda b, h, i: (b, h)),
            pl.BlockSpec((seq, dh), lambda b, h, i: (b, 2 * nh + h)),
        ],
        out_specs=pl.BlockSpec((blk, dh), lambda b, h, i: (b * nq + i, h)),
        out_shape=jax.ShapeDtypeStruct((m, nh * dh), BF16),
        scratch_shapes=[pltpu.VMEM((nq, dh, blk), BF16),
                        pltpu.VMEM((1, blk), F32), pltpu.VMEM((1, blk), F32),
                        pltpu.VMEM((dh, blk), F32)],
        compiler_params=_cparams("parallel", "parallel", "arbitrary"),
        name="fox_attention",
    )(first_tiles, proj, query_bias, proj, key_bias, proj)


def _pad_cols(t, width):
    return jnp.pad(t, ((0, 0), (0, width - t.shape[1])))


def _even_col_scale():
    cs = jnp.ones((EVEN_MAIN,), F32)
    cs = cs.at[EVEN_QS:EVEN_QS + SWA_Q_HEADS * HEAD_DIM].set(HEAD_DIM ** -0.5 * LOG2E)
    cs = cs.at[EVEN_QM:EVEN_QM + MLSTM_HEADS * MLSTM_QK_DIM].set(MLSTM_QK_DIM ** -0.5)
    return cs.reshape(1, EVEN_MAIN)


def _fox_col_scale():
    cs = jnp.ones((FOX_MAIN,), F32)
    cs = cs.at[:FOX_HEADS * FOX_HEAD_DIM].set(FOX_HEAD_DIM ** -0.5 * LOG2E)
    return cs.reshape(1, FOX_MAIN)


def _even_mixer(h, x_in, gain, w_in, w_out, sinks, b_i, b_f, head_gain, ffn_gain, batch, seq):
    q_s, k_s, v_s, q_m, k_m, v_m, o_m, gate_cols = jnp.split(
        w_in, [1024, 1280, 1536, 2048, 2560, 3584, 4608], axis=1)
    w_main = jnp.concatenate([q_s, v_m, o_m, q_m, k_m, k_s, v_s], axis=1).astype(BF16)
    w_gate = _pad_cols(gate_cols, GATE_PAD).astype(BF16)
    proj, gates = norm_proj(x_in, gain, w_main, _even_col_scale(), w_gate)
    attn = swa_attention(proj, sinks, batch, seq)
    bias_row = _pad_cols(jnp.concatenate([b_i, b_f]).reshape(1, -1), GATE_PAD)
    mem = mlstm(proj, gates, bias_row, head_gain, batch, seq)
    w_out = w_out.astype(BF16)
    split = SWA_Q_HEADS * HEAD_DIM
    return out_proj_norm([attn, mem], [w_out[:split], w_out[split:]], h, ffn_gain)


def _fox_mixer(h, x_in, gain, w_in, b_f, w_out, ffn_gain, batch, seq):
    w_main = w_in[:, :FOX_MAIN].astype(BF16)
    w_gate = _pad_cols(w_in[:, FOX_MAIN:], GATE_PAD).astype(BF16)
    proj, gates = norm_proj(x_in, gain, w_main, _fox_col_scale(), w_gate)
    key_bias, query_bias, stats = fox_bias_columns(
        gates, _pad_cols(b_f.reshape(1, -1), GATE_PAD), proj, batch, seq)
    first_tiles = fox_first_tiles(stats, batch, seq)
    attn = fox_attention(proj, key_bias, query_bias, first_tiles, batch, seq)
    return out_proj_norm([attn], [w_out.astype(BF16)], h, ffn_gain)


def kernel(x, norm_mix, norm_ffn, norm_final, w_in_even, w_out_even, swa_sinks, mlstm_b_i, mlstm_b_f, mlstm_head_gain, w_in_odd, fox_b_f, w_out_odd, w_gate, w_up, w_down):
    batch, seq, d = x.shape
    depth = norm_mix.shape[0]
    h = x.reshape(batch * seq, d)
    x_in = h
    w_gate, w_up, w_down = cast_bf16(w_gate), cast_bf16(w_up), cast_bf16(w_down)
    for layer in range(depth):
        j = layer // 2
        if layer % 2 == 0:
            h, xn = _even_mixer(h, x_in, norm_mix[layer], w_in_even[j], w_out_even[j],
                                swa_sinks[j], mlstm_b_i[j], mlstm_b_f[j], mlstm_head_gain[j],
                                norm_ffn[layer], batch, seq)
        else:
            h, xn = _fox_mixer(h, x_in, norm_mix[layer], w_in_odd[j], fox_b_f[j], w_out_odd[j],
                               norm_ffn[layer], batch, seq)
        act = ffn_up(xn, w_gate, w_up, layer)
        if layer + 1 < depth:
            h, x_in = down_proj_norm(act, w_down, layer, h, norm_mix[layer + 1], last=False)
        else:
            (out,) = down_proj_norm(act, w_down, layer, h, norm_final, last=True)
    return out.reshape(batch, seq, d)
```

```python
import functools

import jax
import jax.numpy as jnp
import numpy as np
from jax import lax
from jax.experimental import pallas as pl
from jax.experimental.pallas import tpu as pltpu

F32 = jnp.float32
BF16 = jnp.bfloat16

D_MODEL = 2048
RMS_EPS = 1e-5
LOG2E = 1.4426950408889634

HEAD_DIM = 64
SWA_Q_HEADS = 16
SWA_KV_HEADS = 4
SWA_GROUP = SWA_Q_HEADS // SWA_KV_HEADS
WINDOW = 128
SWA_BLOCK = 128

MLSTM_HEADS = 4
MLSTM_V_DIM = 256
MLSTM_QK_DIM = 128
MLSTM_CHUNK = 256
GATE_SOFTCAP = 15.0

FOX_HEAD_DIM = 128
FOX_HEADS = 16

FFN_HIDDEN = 5632

EVEN_MAIN = 4608
EVEN_QS, EVEN_VM, EVEN_OM = 0, 1024, 2048
EVEN_QM, EVEN_KM, EVEN_KS, EVEN_VS = 3072, 3584, 4096, 4352
FOX_MAIN = 3 * FOX_HEADS * FOX_HEAD_DIM

LANES = 128
GATE_PAD = LANES
VMEM_LIMIT = 56 * 1024 * 1024


def _cparams(*semantics, flags=None):
    return pltpu.CompilerParams(dimension_semantics=semantics,
                                vmem_limit_bytes=VMEM_LIMIT, flags=flags)


def _rms_norm_rows(x, gain):
    ms = jnp.mean(x * x, axis=-1, keepdims=True)
    return x * lax.rsqrt(ms + RMS_EPS) * gain


def _log_sigmoid(x):
    return jnp.minimum(x, 0.0) - jnp.log(1.0 + jnp.exp(-jnp.abs(x)))


def _sigmoid(x):
    return 1.0 / (1.0 + jnp.exp(-x))


def _split3_bf16(x):
    hi = x.astype(BF16)
    r1 = x - hi.astype(F32)
    mid = r1.astype(BF16)
    lo = (r1 - mid.astype(F32)).astype(BF16)
    return hi, mid, lo


def _dot(a, b):
    return jnp.dot(a, b, preferred_element_type=F32)


def _dot_nt(a, b):
    return lax.dot_general(a, b, (((1,), (1,)), ((), ())), preferred_element_type=F32)


def _norm_proj_kernel(x_ref, g_ref, w_ref, cs_ref, wg_ref, o_ref, og_ref, xn_ref, *, normed):
    lhs_ref = x_ref if normed else xn_ref

    @pl.when(pl.program_id(1) == 0)
    def _():
        if not normed:
            xn_ref[...] = _rms_norm_rows(x_ref[...], g_ref[...]).astype(BF16)
        og_ref[...] = _dot(lhs_ref[...], wg_ref[...])

    o_ref[...] = (_dot(lhs_ref[...], w_ref[...]) * cs_ref[...]).astype(o_ref.dtype)


def norm_proj(x, gain, w, col_scale, w_gate, *, tm=1024, tn=1536):
    m, d = x.shape
    n = w.shape[1]
    tm = min(tm, m)
    return pl.pallas_call(
        functools.partial(_norm_proj_kernel, normed=(x.dtype == BF16)),
        grid=(m // tm, n // tn),
        in_specs=[
            pl.BlockSpec((tm, d), lambda i, j: (i, 0)),
            pl.BlockSpec((1, d), lambda i, j: (0, 0)),
            pl.BlockSpec((d, tn), lambda i, j: (0, j)),
            pl.BlockSpec((1, tn), lambda i, j: (0, j)),
            pl.BlockSpec((d, GATE_PAD), lambda i, j: (0, 0)),
        ],
        out_specs=[
            pl.BlockSpec((tm, tn), lambda i, j: (i, j)),
            pl.BlockSpec((tm, GATE_PAD), lambda i, j: (i, 0)),
        ],
        out_shape=[
            jax.ShapeDtypeStruct((m, n), BF16),
            jax.ShapeDtypeStruct((m, GATE_PAD), F32),
        ],
        scratch_shapes=[pltpu.VMEM((tm, d), BF16)],
        compiler_params=_cparams("parallel", "arbitrary"),
        name="norm_proj",
    )(x, gain.reshape(1, d), w, col_scale, w_gate)


def _matmul_res_kernel(*refs, n_lhs):
    a_refs = refs[:n_lhs]
    w_refs = refs[n_lhs:2 * n_lhs]
    r_ref, o_ref = refs[2 * n_lhs], refs[2 * n_lhs + 1]
    acc = r_ref[...]
    for a_ref, w_ref in zip(a_refs, w_refs):
        acc = acc + _dot(a_ref[...], w_ref[...])
    o_ref[...] = acc


def matmul_residual(lhs_list, w_list, res, layer, *, tm=1024, tn=512):
    m, n = res.shape
    tm = min(tm, m)
    n_lhs = len(lhs_list)
    in_specs = []
    for a in lhs_list:
        in_specs.append(pl.BlockSpec((tm, a.shape[1]), lambda i, j: (i, 0)))
    for w in w_list:
        in_specs.append(pl.BlockSpec((None, w.shape[1], tn), lambda i, j: (layer, 0, j)))
    in_specs.append(pl.BlockSpec((tm, tn), lambda i, j: (i, j)))
    return pl.pallas_call(
        functools.partial(_matmul_res_kernel, n_lhs=n_lhs),
        grid=(m // tm, n // tn),
        in_specs=in_specs,
        out_specs=pl.BlockSpec((tm, tn), lambda i, j: (i, j)),
        out_shape=jax.ShapeDtypeStruct((m, n), F32),
        compiler_params=_cparams("parallel", "arbitrary"),
        name="matmul_residual",
    )(*lhs_list, *w_list, res)


def _out_proj_norm_kernel(*refs, n_lhs):
    a_refs = refs[:n_lhs]
    w_refs = refs[n_lhs:2 * n_lhs]
    r_ref, g_ref, h_ref, xn_ref = refs[2 * n_lhs:2 * n_lhs + 4]
    acc = r_ref[...]
    for a_ref, w_ref in zip(a_refs, w_refs):
        acc = acc + _dot(a_ref[...], w_ref[...])
    h_ref[...] = acc
    xn_ref[...] = _rms_norm_rows(acc, g_ref[...]).astype(xn_ref.dtype)


def out_proj_norm(lhs_list, w_list, res, gain, *, tm=512):
    m, n = res.shape
    tm = min(tm, m)
    n_lhs = len(lhs_list)
    in_specs = [pl.BlockSpec((tm, a.shape[1]), lambda i: (i, 0)) for a in lhs_list]
    in_specs += [pl.BlockSpec(w.shape, lambda i: (0, 0)) for w in w_list]
    in_specs += [pl.BlockSpec((tm, n), lambda i: (i, 0)), pl.BlockSpec((1, n), lambda i: (0, 0))]
    return pl.pallas_call(
        functools.partial(_out_proj_norm_kernel, n_lhs=n_lhs),
        grid=(m // tm,),
        in_specs=in_specs,
        out_specs=[pl.BlockSpec((tm, n), lambda i: (i, 0)), pl.BlockSpec((tm, n), lambda i: (i, 0))],
        out_shape=[jax.ShapeDtypeStruct((m, n), F32), jax.ShapeDtypeStruct((m, n), BF16)],
        compiler_params=_cparams("parallel"),
        name="out_proj_norm",
    )(*lhs_list, *w_list, res, gain.reshape(1, n))


def _down_proj_norm_kernel(a_ref, w_ref, r_ref, g_ref, *out_refs, last):
    acc = r_ref[...] + _dot(a_ref[...], w_ref[...])
    normed = _rms_norm_rows(acc, g_ref[...])
    if last:
        out_refs[0][...] = normed
    else:
        out_refs[0][...] = acc
        out_refs[1][...] = normed.astype(out_refs[1].dtype)


def down_proj_norm(act, w_stack, layer, res, gain, *, last, tm=256):
    m, n = res.shape
    k = act.shape[1]
    tm = min(tm, m)
    rows = lambda width: pl.BlockSpec((tm, width), lambda i: (i, 0))
    out_shape = [jax.ShapeDtypeStruct((m, n), F32)]
    if not last:
        out_shape.append(jax.ShapeDtypeStruct((m, n), BF16))
    return pl.pallas_call(
        functools.partial(_down_proj_norm_kernel, last=last),
        grid=(m // tm,),
        in_specs=[rows(k),
                  pl.BlockSpec((None, k, n), lambda i: (layer, 0, 0),
                               pipeline_mode=pl.Buffered(1)),
                  rows(n),
                  pl.BlockSpec((1, n), lambda i: (0, 0))],
        out_specs=[rows(n)] * len(out_shape),
        out_shape=out_shape,
        compiler_params=_cparams("parallel"),
        name="down_proj_norm",
    )(act, w_stack, res, gain.reshape(1, n))


def _ffn_up_kernel(xn_ref, wg_ref, wu_ref, o_ref):
    xn = xn_ref[...]
    gate = _dot(xn, wg_ref[...])
    up = _dot(xn, wu_ref[...])
    o_ref[...] = (gate * _sigmoid(gate) * up).astype(o_ref.dtype)


def ffn_up(xn, w_gate, w_up, layer, *, tm=1024, tn=512):
    m, d = xn.shape
    n = w_gate.shape[2]
    tm = min(tm, m)
    return pl.pallas_call(
        _ffn_up_kernel,
        grid=(m // tm, n // tn),
        in_specs=[
            pl.BlockSpec((tm, d), lambda i, j: (i, 0)),
            pl.BlockSpec((None, d, tn), lambda i, j: (layer, 0, j)),
            pl.BlockSpec((None, d, tn), lambda i, j: (layer, 0, j)),
        ],
        out_specs=pl.BlockSpec((tm, tn), lambda i, j: (i, j)),
        out_shape=jax.ShapeDtypeStruct((m, n), BF16),
        compiler_params=_cparams("parallel", "arbitrary"),
        name="ffn_up",
    )(xn, w_gate, w_up)


def _cast_kernel(x_ref, o_ref):
    o_ref[...] = x_ref[...].astype(o_ref.dtype)


def cast_bf16(w, *, rows=512):
    lead, r, c = w.shape
    flat = w.reshape(lead * r, c)
    out = pl.pallas_call(
        _cast_kernel,
        grid=(lead * r // rows,),
        in_specs=[pl.BlockSpec((rows, c), lambda i: (i, 0))],
        out_specs=pl.BlockSpec((rows, c), lambda i: (i, 0)),
        out_shape=jax.ShapeDtypeStruct(flat.shape, BF16),
        compiler_params=_cparams("parallel"),
        name="cast_bf16",
    )(flat)
    return out.reshape(lead, r, c)


def _final_norm_kernel(x_ref, g_ref, o_ref):
    o_ref[...] = _rms_norm_rows(x_ref[...], g_ref[...])


def final_norm(x, gain, *, tm=1024):
    m, d = x.shape
    tm = min(tm, m)
    return pl.pallas_call(
        _final_norm_kernel,
        grid=(m // tm,),
        in_specs=[pl.BlockSpec((tm, d), lambda i: (i, 0)),
                  pl.BlockSpec((1, d), lambda i: (0, 0))],
        out_specs=pl.BlockSpec((tm, d), lambda i: (i, 0)),
        out_shape=jax.ShapeDtypeStruct((m, d), F32),
        compiler_params=_cparams("parallel"),
        name="final_norm",
    )(x, gain.reshape(1, d))


def _swa_kernel(sinks_ref, q_ref, kc_ref, kp_ref, vc_ref, vp_ref, o_ref, bias_ref):
    blk, dh = SWA_BLOCK, HEAD_DIM
    nkv = SWA_KV_HEADS
    keys = 2 * blk

    @pl.when((pl.program_id(0) == 0) & (pl.program_id(1) == 0))
    def _():
        kj = lax.broadcasted_iota(jnp.int32, (keys, blk), 0)
        qi = lax.broadcasted_iota(jnp.int32, (keys, blk), 1) + blk
        dist = qi - kj
        in_window = (dist >= 0) & (dist < WINDOW)
        distf = dist.astype(F32)
        for g in range(nkv):
            for first, valid in ((0, in_window & (kj >= blk)), (1, in_window)):
                quad = [jnp.where(valid, -(2.0 ** (-8.0 * (4 * g + hh + 1) / SWA_Q_HEADS) * LOG2E)
                                  * distf, -jnp.inf) for hh in range(SWA_GROUP)]
                bias_ref[first, g] = jnp.concatenate(
                    [jnp.concatenate([quad[0], quad[2]], axis=1),
                     jnp.concatenate([quad[1], quad[3]], axis=1)], axis=0)

    which = jnp.where(pl.program_id(1) > 0, 1, 0)
    k = jnp.concatenate([kp_ref[...], kc_ref[...]], axis=0).astype(F32)
    v_t = jnp.concatenate([vp_ref[...], vc_ref[...]], axis=0).astype(F32).T
    lane = lax.broadcasted_iota(jnp.int32, (keys, LANES), 1)
    low = lane < dh

    def key_operand(g):
        kblk = k[:, (g // 2) * LANES:(g // 2 + 1) * LANES]
        swapped = pltpu.roll(kblk, dh, axis=1)
        k_low = jnp.where(low, kblk if g % 2 == 0 else swapped, 0.0)
        k_high = jnp.where(low, 0.0, swapped if g % 2 == 0 else kblk)
        return jnp.concatenate([k_low, k_high], axis=0).astype(BF16)

    def value_operand(g):
        vh_t = v_t[g * dh:(g + 1) * dh, :]
        zero = jnp.zeros_like(vh_t)
        return jnp.concatenate([jnp.concatenate([vh_t, zero], axis=1),
                                jnp.concatenate([zero, vh_t], axis=1)], axis=0).astype(BF16)

    scores = []
    for g in range(nkv):
        q2 = jnp.concatenate([q_ref[:, (2 * g) * LANES:(2 * g + 1) * LANES],
                              q_ref[:, (2 * g + 1) * LANES:(2 * g + 2) * LANES]], axis=0)
        scores.append(_dot_nt(key_operand(g), q2))
    probs, scales = [], []
    for g in range(nkv):
        s = scores[g] + bias_ref[which, g]
        halves, inv = [], []
        for half in range(2):
            sh = s[half * keys:(half + 1) * keys, :]
            sink = jnp.concatenate(
                [jnp.full((1, blk), sinks_ref[4 * g + 2 * pair + half] * LOG2E, F32)
                 for pair in range(2)], axis=1)
            m = jnp.maximum(jnp.max(sh, axis=0, keepdims=True), sink)
            p = jnp.exp2(sh - m)
            den = jnp.sum(p, axis=0, keepdims=True) + jnp.exp2(sink - m)
            halves.append(p.astype(BF16))
            inv.append(jnp.broadcast_to(1.0 / den, (dh, 2 * blk)))
        probs.append(jnp.concatenate(halves, axis=0))
        scales.append(jnp.concatenate(inv, axis=0))
    for g in range(nkv):
        o_t = _dot(value_operand(g), probs[g]) * scales[g]
        for pair in range(2):
            o_ref[:, (2 * g + pair) * LANES:(2 * g + pair + 1) * LANES] = (
                o_t[:, pair * blk:(pair + 1) * blk].T.astype(o_ref.dtype))


def swa_attention(proj, sinks, batch, seq):
    blk = SWA_BLOCK
    nb = seq // blk
    m = batch * seq
    qw = SWA_Q_HEADS * HEAD_DIM
    kw = SWA_KV_HEADS * HEAD_DIM
    cur = lambda b, n: b * nb + n
    prev = lambda b, n: b * nb + jnp.maximum(n - 1, 0)
    return pl.pallas_call(
        _swa_kernel,
        grid=(batch, nb),
        in_specs=[
            pl.BlockSpec(memory_space=pltpu.SMEM),
            pl.BlockSpec((blk, qw), lambda b, n: (cur(b, n), EVEN_QS // qw)),
            pl.BlockSpec((blk, kw), lambda b, n: (cur(b, n), EVEN_KS // kw)),
            pl.BlockSpec((blk, kw), lambda b, n: (prev(b, n), EVEN_KS // kw)),
            pl.BlockSpec((blk, kw), lambda b, n: (cur(b, n), EVEN_VS // kw)),
            pl.BlockSpec((blk, kw), lambda b, n: (prev(b, n), EVEN_VS // kw)),
        ],
        out_specs=pl.BlockSpec((blk, qw), lambda b, n: (cur(b, n), 0)),
        out_shape=jax.ShapeDtypeStruct((m, qw), BF16),
        scratch_shapes=[pltpu.VMEM((2, SWA_KV_HEADS, 4 * blk, 2 * blk), F32)],
        compiler_params=_cparams("arbitrary", "arbitrary"),
        name="swa_attention",
    )(sinks, proj, proj, proj, proj, proj)


def _mlstm_kernel(g_ref, bias_ref, gain_ref, q_ref, k_ref, v_ref, og_ref, o_ref,
                  c_ref, m_ref):
    L = MLSTM_CHUNK
    dk, dv = MLSTM_QK_DIM, MLSTM_V_DIM
    nh = MLSTM_HEADS

    @pl.when(pl.program_id(1) == 0)
    def _():
        c_ref[...] = jnp.zeros_like(c_ref)
        m_ref[...] = jnp.zeros_like(m_ref)

    a_all = GATE_SOFTCAP * jnp.tanh((g_ref[...] + bias_ref[...]) / GATE_SOFTCAP)
    lf_all = _log_sigmoid(a_all)
    row = lax.broadcasted_iota(jnp.int32, (L, L), 0)
    col = lax.broadcasted_iota(jnp.int32, (L, L), 1)
    causal = row >= col
    tri = jnp.where(causal, 1.0, 0.0).astype(BF16)
    hi, mid, lo = _split3_bf16(lf_all)
    b_all = _dot(tri, hi) + _dot(tri, mid) + _dot(tri, lo)
    a_all_t = a_all.T
    b_all_t = b_all.T
    ones = jnp.ones((L, LANES), BF16)

    for h in range(nh):
        li_r = a_all_t[h:h + 1, :]
        b_c = b_all[:, nh + h:nh + h + 1]
        b_r = b_all_t[nh + h:nh + h + 1, :]
        b_last = b_r[:, L - 1:L]
        m_prev = m_ref[h:h + 1, 0:1]

        a_r = b_last - b_r + li_r
        a_max = jnp.max(a_r, axis=-1, keepdims=True)
        w_r = jnp.exp(a_r - a_max)

        log_d = jnp.where(causal, b_c - b_r + li_r, -jnp.inf)
        g_c = b_c + m_prev
        m_t = jnp.maximum(g_c, jnp.max(log_d, axis=-1, keepdims=True))
        d_mat = jnp.exp(log_d - m_t)
        inter = jnp.exp(g_c - m_t)

        q = q_ref[:, h * dk:(h + 1) * dk]
        k = k_ref[:, h * dk:(h + 1) * dk]
        v_aug = jnp.concatenate([v_ref[:, h * dv:(h + 1) * dv], ones], axis=1)
        p = (_dot_nt(q, k) * d_mat).astype(BF16)
        c_prev = c_ref[h]
        num = _dot(p, v_aug) + inter * _dot(q, c_prev.astype(BF16))
        den = num[:, dv:dv + 1]
        hh = num[:, :dv] / jnp.maximum(jnp.abs(den), jnp.exp(-m_t))
        hn = _rms_norm_rows(hh, gain_ref[:, h * dv:(h + 1) * dv])
        gate = _sigmoid(og_ref[:, h * dv:(h + 1) * dv].astype(F32))
        o_ref[:, h * dv:(h + 1) * dv] = (gate * hn).astype(o_ref.dtype)

        m_new = jnp.maximum(b_last + m_prev, a_max)
        decay = jnp.exp(b_last + m_prev - m_new)
        scl = jnp.exp(a_max - m_new)
        kw_t = (k.astype(F32).T * w_r).astype(BF16)
        c_ref[h] = decay * c_prev + scl * _dot(kw_t, v_aug)
        m_ref[h:h + 1, :] = jnp.broadcast_to(m_new, (1, LANES))


def mlstm(proj, gates, bias_row, head_gain, batch, seq):
    L = MLSTM_CHUNK
    nc = seq // L
    m = batch * seq
    nh, dk, dv = MLSTM_HEADS, MLSTM_QK_DIM, MLSTM_V_DIM
    rows = lambda b, c: b * nc + c
    return pl.pallas_call(
        _mlstm_kernel,
        grid=(batch, nc),
        in_specs=[
            pl.BlockSpec((L, GATE_PAD), lambda b, c: (rows(b, c), 0)),
            pl.BlockSpec((1, GATE_PAD), lambda b, c: (0, 0)),
            pl.BlockSpec((1, nh * dv), lambda b, c: (0, 0)),
            pl.BlockSpec((L, nh * dk), lambda b, c: (rows(b, c), EVEN_QM // (nh * dk))),
            pl.BlockSpec((L, nh * dk), lambda b, c: (rows(b, c), EVEN_KM // (nh * dk))),
            pl.BlockSpec((L, nh * dv), lambda b, c: (rows(b, c), EVEN_VM // (nh * dv))),
            pl.BlockSpec((L, nh * dv), lambda b, c: (rows(b, c), EVEN_OM // (nh * dv))),
        ],
        out_specs=pl.BlockSpec((L, nh * dv), lambda b, c: (rows(b, c), 0)),
        out_shape=jax.ShapeDtypeStruct((m, nh * dv), BF16),
        scratch_shapes=[pltpu.VMEM((nh, dk, dv + LANES), F32),
                        pltpu.VMEM((8, LANES), F32)],
        compiler_params=_cparams("parallel", "arbitrary"),
        name="mlstm",
    )(gates, bias_row, head_gain.reshape(1, nh * dv), proj, proj, proj, proj)


BIAS_COLS = 3


def _fox_selectors():
    width = FOX_HEADS * LANES
    sel_k = np.zeros((GATE_PAD, width), np.float32)
    sel_q = np.zeros((GATE_PAD, width), np.float32)
    const_k = np.zeros((1, width), np.float32)
    const_q = np.zeros((1, width), np.float32)
    for h in range(FOX_HEADS):
        for part in range(BIAS_COLS):
            sel_k[part * FOX_HEADS + h, h * LANES + part] = 1.0
            sel_q[part * FOX_HEADS + h, h * LANES + BIAS_COLS + part] = 1.0
            const_k[0, h * LANES + BIAS_COLS + part] = 1.0
            const_q[0, h * LANES + part] = -1.0
    return (jnp.asarray(sel_k, BF16), jnp.asarray(sel_q, BF16),
            jnp.asarray(const_k), jnp.asarray(const_q))


FOX_BLOCK = 512
STAT_ROWS = 8


def _head_norms_sq(x_ref, gsum):
    return jnp.max(_dot(jnp.square(x_ref[...]), gsum), axis=0, keepdims=True)


def _fox_bias_kernel(g_ref, bias_ref, selk_ref, selq_ref, ck_ref, cq_ref, gsum_ref, q_ref, k_ref,
                     kb_ref, qb_ref, stat_ref, carry_ref, *, ts):
    @pl.when(pl.program_id(1) == 0)
    def _():
        carry_ref[...] = jnp.zeros_like(carry_ref)

    lf = _log_sigmoid(g_ref[...] + bias_ref[...])
    row = lax.broadcasted_iota(jnp.int32, (ts, ts), 0)
    col = lax.broadcasted_iota(jnp.int32, (ts, ts), 1)
    tri = jnp.where(row >= col, 1.0, 0.0).astype(BF16)
    hi, mid, lo = _split3_bf16(lf)
    c = _dot(tri, hi) + _dot(tri, mid) + _dot(tri, lo) + carry_ref[0:1, :]
    carry_ref[...] = jnp.broadcast_to(c[ts - 1:ts, :], carry_ref.shape)
    c2 = c * LOG2E
    p_hi, p_mid, p_lo = (t.astype(F32) for t in _split3_bf16(c2))
    lane = lax.broadcasted_iota(jnp.int32, (ts, GATE_PAD), 1)
    parts = jnp.where(lane < FOX_HEADS, p_hi,
                      jnp.where(lane < 2 * FOX_HEADS, pltpu.roll(p_mid, FOX_HEADS, axis=1),
                                pltpu.roll(p_lo, 2 * FOX_HEADS, axis=1))).astype(BF16)
    kb_ref[...] = (_dot(parts, selk_ref[...]) + ck_ref[...]).astype(BF16)
    qb_ref[...] = (_dot(parts, selq_ref[...]) + cq_ref[...]).astype(BF16)
    gsum = gsum_ref[...]
    stat_ref[0] = jnp.concatenate(
        [_head_norms_sq(q_ref, gsum), _head_norms_sq(k_ref, gsum), c2[0:1, :], c2[ts - 1:ts, :],
         jnp.zeros((STAT_ROWS - 4, GATE_PAD), F32)], axis=0)


def fox_bias_columns(gates, bias_row, proj, batch, seq):
    ts = FOX_BLOCK
    ns = seq // ts
    m = batch * seq
    width = FOX_HEADS * LANES
    sel_k, sel_q, const_k, const_q = _fox_selectors()
    gsum = np.zeros((width, GATE_PAD), np.float32)
    for h in range(FOX_HEADS):
        gsum[h * FOX_HEAD_DIM:(h + 1) * FOX_HEAD_DIM, h] = 1.0
    gsum = jnp.asarray(gsum, BF16)
    whole = lambda shape: pl.BlockSpec(shape, lambda b, s: (0, 0))
    rows = lambda b, s: b * ns + s
    return pl.pallas_call(
        functools.partial(_fox_bias_kernel, ts=ts),
        grid=(batch, ns),
        in_specs=[pl.BlockSpec((ts, GATE_PAD), lambda b, s: (rows(b, s), 0)),
                  whole((1, GATE_PAD)),
                  whole(sel_k.shape), whole(sel_q.shape),
                  whole((1, width)), whole((1, width)), whole(gsum.shape),
                  pl.BlockSpec((ts, width), lambda b, s: (rows(b, s), 0)),
                  pl.BlockSpec((ts, width), lambda b, s: (rows(b, s), 1))],
        out_specs=[pl.BlockSpec((ts, width), lambda b, s: (rows(b, s), 0)),
                   pl.BlockSpec((ts, width), lambda b, s: (rows(b, s), 0)),
                   pl.BlockSpec((1, STAT_ROWS, GATE_PAD), lambda b, s: (rows(b, s), 0, 0))],
        out_shape=[jax.ShapeDtypeStruct((m, width), BF16),
                   jax.ShapeDtypeStruct((m, width), BF16),
                   jax.ShapeDtypeStruct((batch * ns, STAT_ROWS, GATE_PAD), F32)],
        scratch_shapes=[pltpu.VMEM((8, GATE_PAD), F32)],
        compiler_params=_cparams("parallel", "arbitrary"),
        name="fox_bias_columns",
    )(gates, bias_row, sel_k, sel_q, const_k, const_q, gsum, proj, proj)


FOX_SKIP_BITS = 170.0
FOX_NORM_MARGIN = 1.02


def fox_first_tiles(stats, batch, seq):
    nq = seq // FOX_BLOCK
    st = stats.reshape(batch, nq, STAT_ROWS, GATE_PAD)[..., :FOX_HEADS]
    qn = jnp.sqrt(st[:, :, 0, :]) * FOX_NORM_MARGIN
    kn = jnp.sqrt(st[:, :, 1, :])
    c_first, c_last = st[:, :, 2, :], st[:, :, 3, :]
    upper = (qn[:, :, None, :] * kn[:, None, :, :]
             + c_first[:, :, None, :] - c_last[:, None, :, :])
    row_max_lower = -(qn * kn)[:, :, None, :]
    blocks = jnp.arange(nq)
    needed = (upper - row_max_lower >= -FOX_SKIP_BITS) | (blocks[None, None, :, None] >= blocks[None, :, None, None])
    first = jnp.argmax(needed, axis=2).astype(jnp.int32)
    return jnp.transpose(first, (0, 2, 1)).reshape(-1)


FOX_GROUP = 8


def _fox_attn_kernel(first_ref, q_ref, qb_ref, k_ref, kb_ref, v_ref, o_ref,
                     vt_ref, m_ref, l_ref, acc_ref, s_ref, *, blk, nblk):
    qi = pl.program_id(2)
    group = FOX_GROUP

    @pl.when(qi == 0)
    def _():
        for c in range(nblk):
            vt_ref[c] = v_ref[c * blk:(c + 1) * blk, :].astype(F32).T.astype(BF16)

    q_aug_t = jnp.concatenate([q_ref[...].astype(F32).T, qb_ref[...].astype(F32).T],
                              axis=0).astype(BF16)
    m_ref[...] = jnp.full_like(m_ref, -jnp.inf)
    l_ref[...] = jnp.zeros_like(l_ref)
    acc_ref[...] = jnp.zeros_like(acc_ref)

    def tiles(kbs, masks):
        n = len(kbs)
        start = pl.multiple_of(kbs[0] * blk, blk)
        k_aug = jnp.concatenate([k_ref[pl.ds(start, n * blk), :], kb_ref[pl.ds(start, n * blk), :]],
                                axis=1)
        s_all = _dot(k_aug, q_aug_t)
        tile_max = []
        for t, masked in enumerate(masks):
            s = s_all[t * blk:(t + 1) * blk, :]
            if masked:
                key = lax.broadcasted_iota(jnp.int32, (blk, blk), 0)
                qry = lax.broadcasted_iota(jnp.int32, (blk, blk), 1)
                s = jnp.where(key <= qry, s, -jnp.inf)
            s_ref[t] = s
            tile_max.append(jnp.max(s, axis=0, keepdims=True))
        m_run = m_ref[...]
        l_run = l_ref[...]
        acc = acc_ref[...]
        for t, kb in enumerate(kbs):
            m_new = jnp.maximum(m_run, tile_max[t])
            alpha = jnp.exp2(m_run - m_new)
            p = jnp.exp2(s_ref[t] - m_new)
            l_run = alpha * l_run + jnp.sum(p, axis=0, keepdims=True)
            acc = alpha * acc + _dot(vt_ref[kb], p.astype(BF16))
            m_run = m_new
        acc_ref[...] = acc
        l_ref[...] = l_run
        m_ref[...] = m_run

    first = first_ref[(pl.program_id(0) * pl.num_programs(1) + pl.program_id(1)) * nblk + qi]
    n_full = qi - first

    def body(j, carry):
        tiles([first + group * j + t for t in range(group)], [False] * group)
        return carry

    lax.fori_loop(0, n_full // group, body, 0)

    for r in range(group):
        @pl.when(n_full % group == r)
        def _(r=r):
            tiles([qi - r + t for t in range(r + 1)], [False] * r + [True])

    o_ref[...] = (acc_ref[...] / l_ref[...]).T.astype(o_ref.dtype)


def fox_attention(proj, key_bias, query_bias, first_tiles, batch, seq):
    blk = FOX_BLOCK
    nq = seq // blk
    m = batch * seq
    nh, dh = FOX_HEADS, FOX_HEAD_DIM
    return pl.pallas_call(
        functools.partial(_fox_attn_kernel, blk=blk, nblk=nq),
        grid=(batch, nh, nq),
        in_specs=[
            pl.BlockSpec(memory_space=pltpu.SMEM),
            pl.BlockSpec((blk, dh), lambda b, h, i: (b * nq + i, h)),
            pl.BlockSpec((blk, LANES), lambda b, h, i: (b * nq + i, h)),
            pl.BlockSpec((seq, dh), lambda b, h, i: (b, nh + h)),
            pl.BlockSpec((seq, LANES), lambda b, h, i: (b, h)),
            pl.BlockSpec((seq, dh), lambda b, h, i: (b, 2 * nh + h)),
        ],
        out_specs=pl.BlockSpec((blk, dh), lambda b, h, i: (b * nq + i, h)),
        out_shape=jax.ShapeDtypeStruct((m, nh * dh), BF16),
        scratch_shapes=[pltpu.VMEM((nq, dh, blk), BF16),
                        pltpu.VMEM((1, blk), F32), pltpu.VMEM((1, blk), F32),
                        pltpu.VMEM((dh, blk), F32),
                        pltpu.VMEM((FOX_GROUP, blk, blk), F32)],
        compiler_params=_cparams("parallel", "parallel", "arbitrary"),
        name="fox_attention",
    )(first_tiles, proj, query_bias, proj, key_bias, proj)


def _pad_cols(t, width):
    return jnp.pad(t, ((0, 0), (0, width - t.shape[1])))


def _even_col_scale():
    cs = jnp.ones((EVEN_MAIN,), F32)
    cs = cs.at[EVEN_QS:EVEN_QS + SWA_Q_HEADS * HEAD_DIM].set(HEAD_DIM ** -0.5 * LOG2E)
    cs = cs.at[EVEN_QM:EVEN_QM + MLSTM_HEADS * MLSTM_QK_DIM].set(MLSTM_QK_DIM ** -0.5)
    return cs.reshape(1, EVEN_MAIN)


def _fox_col_scale():
    cs = jnp.ones((FOX_MAIN,), F32)
    cs = cs.at[:FOX_HEADS * FOX_HEAD_DIM].set(FOX_HEAD_DIM ** -0.5 * LOG2E)
    return cs.reshape(1, FOX_MAIN)


def _even_mixer(h, x_in, gain, w_in, w_out, sinks, b_i, b_f, head_gain, ffn_gain, batch, seq):
    q_s, k_s, v_s, q_m, k_m, v_m, o_m, gate_cols = jnp.split(
        w_in, [1024, 1280, 1536, 2048, 2560, 3584, 4608], axis=1)
    w_main = jnp.concatenate([q_s, v_m, o_m, q_m, k_m, k_s, v_s], axis=1).astype(BF16)
    w_gate = _pad_cols(gate_cols, GATE_PAD).astype(BF16)
    proj, gates = norm_proj(x_in, gain, w_main, _even_col_scale(), w_gate)
    attn = swa_attention(proj, sinks, batch, seq)
    bias_row = _pad_cols(jnp.concatenate([b_i, b_f]).reshape(1, -1), GATE_PAD)
    mem = mlstm(proj, gates, bias_row, head_gain, batch, seq)
    w_out = w_out.astype(BF16)
    split = SWA_Q_HEADS * HEAD_DIM
    return out_proj_norm([attn, mem], [w_out[:split], w_out[split:]], h, ffn_gain)


def _fox_mixer(h, x_in, gain, w_in, b_f, w_out, ffn_gain, batch, seq):
    w_main = w_in[:, :FOX_MAIN].astype(BF16)
    w_gate = _pad_cols(w_in[:, FOX_MAIN:], GATE_PAD).astype(BF16)
    proj, gates = norm_proj(x_in, gain, w_main, _fox_col_scale(), w_gate)
    key_bias, query_bias, stats = fox_bias_columns(
        gates, _pad_cols(b_f.reshape(1, -1), GATE_PAD), proj, batch, seq)
    first_tiles = fox_first_tiles(stats, batch, seq)
    attn = fox_attention(proj, key_bias, query_bias, first_tiles, batch, seq)
    return out_proj_norm([attn], [w_out.astype(BF16)], h, ffn_gain)


def kernel(x, norm_mix, norm_ffn, norm_final, w_in_even, w_out_even, swa_sinks, mlstm_b_i, mlstm_b_f, mlstm_head_gain, w_in_odd, fox_b_f, w_out_odd, w_gate, w_up, w_down):
    batch, seq, d = x.shape
    depth = norm_mix.shape[0]
    h = x.reshape(batch * seq, d)
    x_in = h
    w_gate, w_up, w_down = cast_bf16(w_gate), cast_bf16(w_up), cast_bf16(w_down)
    for layer in range(depth):
        j = layer // 2
        if layer % 2 == 0:
            h, xn = _even_mixer(h, x_in, norm_mix[layer], w_in_even[j], w_out_even[j],
                                swa_sinks[j], mlstm_b_i[j], mlstm_b_f[j], mlstm_head_gain[j],
                                norm_ffn[layer], batch, seq)
        else:
            h, xn = _fox_mixer(h, x_in, norm_mix[layer], w_in_odd[j], fox_b_f[j], w_out_odd[j],
                               norm_ffn[layer], batch, seq)
        act = ffn_up(xn, w_gate, w_up, layer)
        if layer + 1 < depth:
            h, x_in = down_proj_norm(act, w_down, layer, h, norm_mix[layer + 1], last=False)
        else:
            (out,) = down_proj_norm(act, w_down, layer, h, norm_final, last=True)
    return out.reshape(batch, seq, d)
```

```python
import functools

import jax
import jax.numpy as jnp
import numpy as np
from jax import lax
from jax.experimental import pallas as pl
from jax.experimental.pallas import tpu as pltpu

F32 = jnp.float32
BF16 = jnp.bfloat16

D_MODEL = 2048
RMS_EPS = 1e-5
LOG2E = 1.4426950408889634

HEAD_DIM = 64
SWA_Q_HEADS = 16
SWA_KV_HEADS = 4
SWA_GROUP = SWA_Q_HEADS // SWA_KV_HEADS
WINDOW = 128
SWA_BLOCK = 128

MLSTM_HEADS = 4
MLSTM_V_DIM = 256
MLSTM_QK_DIM = 128
MLSTM_CHUNK = 256
GATE_SOFTCAP = 15.0

FOX_HEAD_DIM = 128
FOX_HEADS = 16

FFN_HIDDEN = 5632

EVEN_MAIN = 4608
EVEN_QS, EVEN_VM, EVEN_OM = 0, 1024, 2048
EVEN_QM, EVEN_KM, EVEN_KS, EVEN_VS = 3072, 3584, 4096, 4352
FOX_MAIN = 3 * FOX_HEADS * FOX_HEAD_DIM

LANES = 128
GATE_PAD = LANES
VMEM_LIMIT = 56 * 1024 * 1024


def _cparams(*semantics, flags=None):
    return pltpu.CompilerParams(dimension_semantics=semantics,
                                vmem_limit_bytes=VMEM_LIMIT, flags=flags)


def _rms_norm_rows(x, gain):
    ms = jnp.mean(x * x, axis=-1, keepdims=True)
    return x * lax.rsqrt(ms + RMS_EPS) * gain


def _log_sigmoid(x):
    return jnp.minimum(x, 0.0) - jnp.log(1.0 + jnp.exp(-jnp.abs(x)))


def _sigmoid(x):
    return 1.0 / (1.0 + jnp.exp(-x))


def _split3_bf16(x):
    hi = x.astype(BF16)
    r1 = x - hi.astype(F32)
    mid = r1.astype(BF16)
    lo = (r1 - mid.astype(F32)).astype(BF16)
    return hi, mid, lo


def _dot(a, b):
    return jnp.dot(a, b, preferred_element_type=F32)


def _dot_nt(a, b):
    return lax.dot_general(a, b, (((1,), (1,)), ((), ())), preferred_element_type=F32)


def _norm_proj_kernel(x_ref, g_ref, w_ref, cs_ref, wg_ref, o_ref, og_ref, xn_ref, *, normed):
    lhs_ref = x_ref if normed else xn_ref

    @pl.when(pl.program_id(1) == 0)
    def _():
        if not normed:
            xn_ref[...] = _rms_norm_rows(x_ref[...], g_ref[...]).astype(BF16)
        og_ref[...] = _dot(lhs_ref[...], wg_ref[...])

    o_ref[...] = (_dot(lhs_ref[...], w_ref[...]) * cs_ref[...]).astype(o_ref.dtype)


def norm_proj(x, gain, w, col_scale, w_gate, *, tm=1024, tn=1536):
    m, d = x.shape
    n = w.shape[1]
    tm = min(tm, m)
    return pl.pallas_call(
        functools.partial(_norm_proj_kernel, normed=(x.dtype == BF16)),
        grid=(m // tm, n // tn),
        in_specs=[
            pl.BlockSpec((tm, d), lambda i, j: (i, 0)),
            pl.BlockSpec((1, d), lambda i, j: (0, 0)),
            pl.BlockSpec((d, tn), lambda i, j: (0, j)),
            pl.BlockSpec((1, tn), lambda i, j: (0, j)),
            pl.BlockSpec((d, GATE_PAD), lambda i, j: (0, 0)),
        ],
        out_specs=[
            pl.BlockSpec((tm, tn), lambda i, j: (i, j)),
            pl.BlockSpec((tm, GATE_PAD), lambda i, j: (i, 0)),
        ],
        out_shape=[
            jax.ShapeDtypeStruct((m, n), BF16),
            jax.ShapeDtypeStruct((m, GATE_PAD), F32),
        ],
        scratch_shapes=[pltpu.VMEM((tm, d), BF16)],
        compiler_params=_cparams("parallel", "arbitrary"),
        name="norm_proj",
    )(x, gain.reshape(1, d), w, col_scale, w_gate)


def _out_proj_norm_kernel(*refs, n_lhs):
    a_refs = refs[:n_lhs]
    w_ref, r_ref, g_ref, h_ref, xn_ref = refs[n_lhs:n_lhs + 5]
    acc = r_ref[...]
    row = 0
    for a_ref in a_refs:
        width = a_ref.shape[1]
        acc = acc + _dot(a_ref[...], w_ref[row:row + width, :])
        row += width
    h_ref[...] = acc
    xn_ref[...] = _rms_norm_rows(acc, g_ref[...]).astype(xn_ref.dtype)


def out_proj_norm(lhs_list, w, res, gain, *, tm=512):
    m, n = res.shape
    tm = min(tm, m)
    in_specs = [pl.BlockSpec((tm, a.shape[1]), lambda i: (i, 0)) for a in lhs_list]
    in_specs += [pl.BlockSpec(w.shape, lambda i: (0, 0)),
                 pl.BlockSpec((tm, n), lambda i: (i, 0)), pl.BlockSpec((1, n), lambda i: (0, 0))]
    return pl.pallas_call(
        functools.partial(_out_proj_norm_kernel, n_lhs=len(lhs_list)),
        grid=(m // tm,),
        in_specs=in_specs,
        out_specs=[pl.BlockSpec((tm, n), lambda i: (i, 0)), pl.BlockSpec((tm, n), lambda i: (i, 0))],
        out_shape=[jax.ShapeDtypeStruct((m, n), F32), jax.ShapeDtypeStruct((m, n), BF16)],
        compiler_params=_cparams("parallel"),
        name="out_proj_norm",
    )(*lhs_list, w, res, gain.reshape(1, n))


def _down_proj_norm_kernel(a_ref, w_ref, r_ref, g_ref, *out_refs, last):
    acc = r_ref[...] + _dot(a_ref[...], w_ref[...])
    normed = _rms_norm_rows(acc, g_ref[...])
    if last:
        out_refs[0][...] = normed
    else:
        out_refs[0][...] = acc
        out_refs[1][...] = normed.astype(out_refs[1].dtype)


def down_proj_norm(act, w_stack, layer, res, gain, *, last, tm=256):
    m, n = res.shape
    k = act.shape[1]
    tm = min(tm, m)
    rows = lambda width: pl.BlockSpec((tm, width), lambda i: (i, 0))
    out_shape = [jax.ShapeDtypeStruct((m, n), F32)]
    if not last:
        out_shape.append(jax.ShapeDtypeStruct((m, n), BF16))
    return pl.pallas_call(
        functools.partial(_down_proj_norm_kernel, last=last),
        grid=(m // tm,),
        in_specs=[rows(k),
                  pl.BlockSpec((None, k, n), lambda i: (layer, 0, 0),
                               pipeline_mode=pl.Buffered(1)),
                  rows(n),
                  pl.BlockSpec((1, n), lambda i: (0, 0))],
        out_specs=[rows(n)] * len(out_shape),
        out_shape=out_shape,
        compiler_params=_cparams("parallel"),
        name="down_proj_norm",
    )(act, w_stack, res, gain.reshape(1, n))


def _ffn_up_kernel(xn_ref, wg_ref, wu_ref, o_ref):
    xn = xn_ref[...]
    gate = _dot(xn, wg_ref[...])
    up = _dot(xn, wu_ref[...])
    o_ref[...] = (gate * _sigmoid(gate) * up).astype(o_ref.dtype)


def ffn_up(xn, w_gate, w_up, layer, *, tm=1024, tn=512):
    m, d = xn.shape
    n = w_gate.shape[2]
    tm = min(tm, m)
    return pl.pallas_call(
        _ffn_up_kernel,
        grid=(m // tm, n // tn),
        in_specs=[
            pl.BlockSpec((tm, d), lambda i, j: (i, 0)),
            pl.BlockSpec((None, d, tn), lambda i, j: (layer, 0, j)),
            pl.BlockSpec((None, d, tn), lambda i, j: (layer, 0, j)),
        ],
        out_specs=pl.BlockSpec((tm, tn), lambda i, j: (i, j)),
        out_shape=jax.ShapeDtypeStruct((m, n), BF16),
        compiler_params=_cparams("parallel", "arbitrary"),
        name="ffn_up",
    )(xn, w_gate, w_up)


def _cast_kernel(x_ref, o_ref):
    o_ref[...] = x_ref[...].astype(o_ref.dtype)


def cast_bf16(w, *, rows=512):
    lead, r, c = w.shape
    flat = w.reshape(lead * r, c)
    out = pl.pallas_call(
        _cast_kernel,
        grid=(lead * r // rows,),
        in_specs=[pl.BlockSpec((rows, c), lambda i: (i, 0))],
        out_specs=pl.BlockSpec((rows, c), lambda i: (i, 0)),
        out_shape=jax.ShapeDtypeStruct(flat.shape, BF16),
        compiler_params=_cparams("parallel"),
        name="cast_bf16",
    )(flat)
    return out.reshape(lead, r, c)


def _swa_kernel(sinks_ref, q_ref, kc_ref, kp_ref, vc_ref, vp_ref, o_ref, bias_ref):
    blk, dh = SWA_BLOCK, HEAD_DIM
    nkv = SWA_KV_HEADS
    keys = 2 * blk

    @pl.when((pl.program_id(0) == 0) & (pl.program_id(1) == 0))
    def _():
        kj = lax.broadcasted_iota(jnp.int32, (keys, blk), 0)
        qi = lax.broadcasted_iota(jnp.int32, (keys, blk), 1) + blk
        dist = qi - kj
        in_window = (dist >= 0) & (dist < WINDOW)
        distf = dist.astype(F32)
        for g in range(nkv):
            for first, valid in ((0, in_window & (kj >= blk)), (1, in_window)):
                quad = [jnp.where(valid, -(2.0 ** (-8.0 * (4 * g + hh + 1) / SWA_Q_HEADS) * LOG2E)
                                  * distf, -jnp.inf) for hh in range(SWA_GROUP)]
                bias_ref[first, g] = jnp.concatenate(
                    [jnp.concatenate([quad[0], quad[2]], axis=1),
                     jnp.concatenate([quad[1], quad[3]], axis=1)], axis=0)

    which = jnp.where(pl.program_id(1) > 0, 1, 0)
    k = jnp.concatenate([kp_ref[...], kc_ref[...]], axis=0).astype(F32)
    v_t = jnp.concatenate([vp_ref[...], vc_ref[...]], axis=0).astype(F32).T
    lane = lax.broadcasted_iota(jnp.int32, (keys, LANES), 1)
    low = lane < dh

    def key_operand(g):
        kblk = k[:, (g // 2) * LANES:(g // 2 + 1) * LANES]
        swapped = pltpu.roll(kblk, dh, axis=1)
        k_low = jnp.where(low, kblk if g % 2 == 0 else swapped, 0.0)
        k_high = jnp.where(low, 0.0, swapped if g % 2 == 0 else kblk)
        return jnp.concatenate([k_low, k_high], axis=0).astype(BF16)

    def value_operand(g):
        vh_t = v_t[g * dh:(g + 1) * dh, :]
        zero = jnp.zeros_like(vh_t)
        return jnp.concatenate([jnp.concatenate([vh_t, zero], axis=1),
                                jnp.concatenate([zero, vh_t], axis=1)], axis=0).astype(BF16)

    scores = []
    for g in range(nkv):
        q2 = jnp.concatenate([q_ref[:, (2 * g) * LANES:(2 * g + 1) * LANES],
                              q_ref[:, (2 * g + 1) * LANES:(2 * g + 2) * LANES]], axis=0)
        scores.append(_dot_nt(key_operand(g), q2))
    probs, scales = [], []
    for g in range(nkv):
        s = scores[g] + bias_ref[which, g]
        halves, inv = [], []
        for half in range(2):
            sh = s[half * keys:(half + 1) * keys, :]
            sink = jnp.concatenate(
                [jnp.full((1, blk), sinks_ref[4 * g + 2 * pair + half] * LOG2E, F32)
                 for pair in range(2)], axis=1)
            m = jnp.maximum(jnp.max(sh, axis=0, keepdims=True), sink)
            p = jnp.exp2(sh - m)
            den = jnp.sum(p, axis=0, keepdims=True) + jnp.exp2(sink - m)
            halves.append(p.astype(BF16))
            inv.append(jnp.broadcast_to(1.0 / den, (dh, 2 * blk)))
        probs.append(jnp.concatenate(halves, axis=0))
        scales.append(jnp.concatenate(inv, axis=0))
    for g in range(nkv):
        o_t = _dot(value_operand(g), probs[g]) * scales[g]
        for pair in range(2):
            o_ref[:, (2 * g + pair) * LANES:(2 * g + pair + 1) * LANES] = (
                o_t[:, pair * blk:(pair + 1) * blk].T.astype(o_ref.dtype))


def swa_attention(proj, sinks, batch, seq):
    blk = SWA_BLOCK
    nb = seq // blk
    m = batch * seq
    qw = SWA_Q_HEADS * HEAD_DIM
    kw = SWA_KV_HEADS * HEAD_DIM
    cur = lambda b, n: b * nb + n
    prev = lambda b, n: b * nb + jnp.maximum(n - 1, 0)
    return pl.pallas_call(
        _swa_kernel,
        grid=(batch, nb),
        in_specs=[
            pl.BlockSpec(memory_space=pltpu.SMEM),
            pl.BlockSpec((blk, qw), lambda b, n: (cur(b, n), EVEN_QS // qw)),
            pl.BlockSpec((blk, kw), lambda b, n: (cur(b, n), EVEN_KS // kw)),
            pl.BlockSpec((blk, kw), lambda b, n: (prev(b, n), EVEN_KS // kw)),
            pl.BlockSpec((blk, kw), lambda b, n: (cur(b, n), EVEN_VS // kw)),
            pl.BlockSpec((blk, kw), lambda b, n: (prev(b, n), EVEN_VS // kw)),
        ],
        out_specs=pl.BlockSpec((blk, qw), lambda b, n: (cur(b, n), 0)),
        out_shape=jax.ShapeDtypeStruct((m, qw), BF16),
        scratch_shapes=[pltpu.VMEM((2, SWA_KV_HEADS, 4 * blk, 2 * blk), F32)],
        compiler_params=_cparams("arbitrary", "arbitrary"),
        name="swa_attention",
    )(sinks, proj, proj, proj, proj, proj)


def _mlstm_kernel(g_ref, bias_ref, gain_ref, q_ref, k_ref, v_ref, og_ref, o_ref,
                  c_ref, m_ref, qk_ref, qc_ref):
    L = MLSTM_CHUNK
    dk, dv = MLSTM_QK_DIM, MLSTM_V_DIM
    nh = MLSTM_HEADS

    @pl.when(pl.program_id(1) == 0)
    def _():
        c_ref[...] = jnp.zeros_like(c_ref)
        m_ref[...] = jnp.zeros_like(m_ref)

    a_all = GATE_SOFTCAP * jnp.tanh((g_ref[...] + bias_ref[...]) / GATE_SOFTCAP)
    lf_all = _log_sigmoid(a_all)
    row = lax.broadcasted_iota(jnp.int32, (L, L), 0)
    col = lax.broadcasted_iota(jnp.int32, (L, L), 1)
    causal = row >= col
    tri = jnp.where(causal, 1.0, 0.0).astype(BF16)
    hi, mid, lo = _split3_bf16(lf_all)
    b_all = _dot(tri, hi) + _dot(tri, mid) + _dot(tri, lo)
    a_all_t = a_all.T
    b_all_t = b_all.T
    ones = jnp.ones((L, LANES), BF16)

    for h in range(nh):
        q = q_ref[:, h * dk:(h + 1) * dk]
        qk_ref[h] = _dot_nt(q, k_ref[:, h * dk:(h + 1) * dk])
        qc_ref[h] = _dot(q, c_ref[h].astype(BF16))

    for h in range(nh):
        li_r = a_all_t[h:h + 1, :]
        b_c = b_all[:, nh + h:nh + h + 1]
        b_r = b_all_t[nh + h:nh + h + 1, :]
        b_last = b_r[:, L - 1:L]
        m_prev = m_ref[h:h + 1, 0:1]

        a_r = b_last - b_r + li_r
        a_max = jnp.max(a_r, axis=-1, keepdims=True)
        w_r = jnp.exp(a_r - a_max)

        log_d = jnp.where(causal, b_c - b_r + li_r, -jnp.inf)
        g_c = b_c + m_prev
        m_t = jnp.maximum(g_c, jnp.max(log_d, axis=-1, keepdims=True))
        d_mat = jnp.exp(log_d - m_t)
        inter = jnp.exp(g_c - m_t)

        k = k_ref[:, h * dk:(h + 1) * dk]
        v_aug = jnp.concatenate([v_ref[:, h * dv:(h + 1) * dv], ones], axis=1)
        p = (qk_ref[h] * d_mat).astype(BF16)
        c_prev = c_ref[h]
        num = _dot(p, v_aug) + inter * qc_ref[h]
        den = num[:, dv:dv + 1]
        hh = num[:, :dv] / jnp.maximum(jnp.abs(den), jnp.exp(-m_t))
        hn = _rms_norm_rows(hh, gain_ref[:, h * dv:(h + 1) * dv])
        gate = _sigmoid(og_ref[:, h * dv:(h + 1) * dv].astype(F32))
        o_ref[:, h * dv:(h + 1) * dv] = (gate * hn).astype(o_ref.dtype)

        m_new = jnp.maximum(b_last + m_prev, a_max)
        decay = jnp.exp(b_last + m_prev - m_new)
        scl = jnp.exp(a_max - m_new)
        kw_t = (k.astype(F32).T * w_r).astype(BF16)
        c_ref[h] = decay * c_prev + scl * _dot(kw_t, v_aug)
        m_ref[h:h + 1, :] = jnp.broadcast_to(m_new, (1, LANES))


def mlstm(proj, gates, bias_row, head_gain, batch, seq):
    L = MLSTM_CHUNK
    nc = seq // L
    m = batch * seq
    nh, dk, dv = MLSTM_HEADS, MLSTM_QK_DIM, MLSTM_V_DIM
    rows = lambda b, c: b * nc + c
    return pl.pallas_call(
        _mlstm_kernel,
        grid=(batch, nc),
        in_specs=[
            pl.BlockSpec((L, GATE_PAD), lambda b, c: (rows(b, c), 0)),
            pl.BlockSpec((1, GATE_PAD), lambda b, c: (0, 0)),
            pl.BlockSpec((1, nh * dv), lambda b, c: (0, 0)),
            pl.BlockSpec((L, nh * dk), lambda b, c: (rows(b, c), EVEN_QM // (nh * dk))),
            pl.BlockSpec((L, nh * dk), lambda b, c: (rows(b, c), EVEN_KM // (nh * dk))),
            pl.BlockSpec((L, nh * dv), lambda b, c: (rows(b, c), EVEN_VM // (nh * dv))),
            pl.BlockSpec((L, nh * dv), lambda b, c: (rows(b, c), EVEN_OM // (nh * dv))),
        ],
        out_specs=pl.BlockSpec((L, nh * dv), lambda b, c: (rows(b, c), 0)),
        out_shape=jax.ShapeDtypeStruct((m, nh * dv), BF16),
        scratch_shapes=[pltpu.VMEM((nh, dk, dv + LANES), F32),
                        pltpu.VMEM((8, LANES), F32),
                        pltpu.VMEM((nh, L, L), F32),
                        pltpu.VMEM((nh, L, dv + LANES), F32)],
        compiler_params=_cparams("parallel", "arbitrary"),
        name="mlstm",
    )(gates, bias_row, head_gain.reshape(1, nh * dv), proj, proj, proj, proj)


BIAS_COLS = 3


def _fox_selectors():
    width = FOX_HEADS * LANES
    sel_k = np.zeros((GATE_PAD, width), np.float32)
    sel_q = np.zeros((GATE_PAD, width), np.float32)
    const_k = np.zeros((1, width), np.float32)
    const_q = np.zeros((1, width), np.float32)
    for h in range(FOX_HEADS):
        for part in range(BIAS_COLS):
            sel_k[part * FOX_HEADS + h, h * LANES + part] = 1.0
            sel_q[part * FOX_HEADS + h, h * LANES + BIAS_COLS + part] = 1.0
            const_k[0, h * LANES + BIAS_COLS + part] = 1.0
            const_q[0, h * LANES + part] = -1.0
    return (jnp.asarray(sel_k, BF16), jnp.asarray(sel_q, BF16),
            jnp.asarray(const_k), jnp.asarray(const_q))


FOX_BLOCK = 512
STAT_ROWS = 8


def _head_norms_sq(x_ref, gsum):
    return jnp.max(_dot(jnp.square(x_ref[...]), gsum), axis=0, keepdims=True)


def _fox_bias_kernel(g_ref, bias_ref, selk_ref, selq_ref, ck_ref, cq_ref, gsum_ref, q_ref, k_ref,
                     kb_ref, qb_ref, stat_ref, carry_ref, *, ts):
    @pl.when(pl.program_id(1) == 0)
    def _():
        carry_ref[...] = jnp.zeros_like(carry_ref)

    lf = _log_sigmoid(g_ref[...] + bias_ref[...])
    row = lax.broadcasted_iota(jnp.int32, (ts, ts), 0)
    col = lax.broadcasted_iota(jnp.int32, (ts, ts), 1)
    tri = jnp.where(row >= col, 1.0, 0.0).astype(BF16)
    hi, mid, lo = _split3_bf16(lf)
    c = _dot(tri, hi) + _dot(tri, mid) + _dot(tri, lo) + carry_ref[0:1, :]
    carry_ref[...] = jnp.broadcast_to(c[ts - 1:ts, :], carry_ref.shape)
    c2 = c * LOG2E
    p_hi, p_mid, p_lo = (t.astype(F32) for t in _split3_bf16(c2))
    lane = lax.broadcasted_iota(jnp.int32, (ts, GATE_PAD), 1)
    parts = jnp.where(lane < FOX_HEADS, p_hi,
                      jnp.where(lane < 2 * FOX_HEADS, pltpu.roll(p_mid, FOX_HEADS, axis=1),
                                pltpu.roll(p_lo, 2 * FOX_HEADS, axis=1))).astype(BF16)
    kb_ref[...] = (_dot(parts, selk_ref[...]) + ck_ref[...]).astype(BF16)
    qb_ref[...] = (_dot(parts, selq_ref[...]) + cq_ref[...]).astype(BF16)
    gsum = gsum_ref[...]
    stat_ref[0] = jnp.concatenate(
        [_head_norms_sq(q_ref, gsum), _head_norms_sq(k_ref, gsum), c2[0:1, :], c2[ts - 1:ts, :],
         jnp.zeros((STAT_ROWS - 4, GATE_PAD), F32)], axis=0)


def fox_bias_columns(gates, bias_row, proj, batch, seq):
    ts = FOX_BLOCK
    ns = seq // ts
    m = batch * seq
    width = FOX_HEADS * LANES
    sel_k, sel_q, const_k, const_q = _fox_selectors()
    gsum = np.zeros((width, GATE_PAD), np.float32)
    for h in range(FOX_HEADS):
        gsum[h * FOX_HEAD_DIM:(h + 1) * FOX_HEAD_DIM, h] = 1.0
    gsum = jnp.asarray(gsum, BF16)
    whole = lambda shape: pl.BlockSpec(shape, lambda b, s: (0, 0))
    rows = lambda b, s: b * ns + s
    return pl.pallas_call(
        functools.partial(_fox_bias_kernel, ts=ts),
        grid=(batch, ns),
        in_specs=[pl.BlockSpec((ts, GATE_PAD), lambda b, s: (rows(b, s), 0)),
                  whole((1, GATE_PAD)),
                  whole(sel_k.shape), whole(sel_q.shape),
                  whole((1, width)), whole((1, width)), whole(gsum.shape),
                  pl.BlockSpec((ts, width), lambda b, s: (rows(b, s), 0)),
                  pl.BlockSpec((ts, width), lambda b, s: (rows(b, s), 1))],
        out_specs=[pl.BlockSpec((ts, width), lambda b, s: (rows(b, s), 0)),
                   pl.BlockSpec((ts, width), lambda b, s: (rows(b, s), 0)),
                   pl.BlockSpec((1, STAT_ROWS, GATE_PAD), lambda b, s: (rows(b, s), 0, 0))],
        out_shape=[jax.ShapeDtypeStruct((m, width), BF16),
                   jax.ShapeDtypeStruct((m, width), BF16),
                   jax.ShapeDtypeStruct((batch * ns, STAT_ROWS, GATE_PAD), F32)],
        scratch_shapes=[pltpu.VMEM((8, GATE_PAD), F32)],
        compiler_params=_cparams("parallel", "arbitrary"),
        name="fox_bias_columns",
    )(gates, bias_row, sel_k, sel_q, const_k, const_q, gsum, proj, proj)


FOX_SKIP_BITS = 170.0
FOX_NORM_MARGIN = 1.02


def fox_first_tiles(stats, batch, seq):
    nq = seq // FOX_BLOCK
    st = stats.reshape(batch, nq, STAT_ROWS, GATE_PAD)[..., :FOX_HEADS]
    qn = jnp.sqrt(st[:, :, 0, :]) * FOX_NORM_MARGIN
    kn = jnp.sqrt(st[:, :, 1, :])
    c_first, c_last = st[:, :, 2, :], st[:, :, 3, :]
    upper = (qn[:, :, None, :] * kn[:, None, :, :]
             + c_first[:, :, None, :] - c_last[:, None, :, :])
    row_max_lower = -(qn * kn)[:, :, None, :]
    blocks = jnp.arange(nq)
    needed = (upper - row_max_lower >= -FOX_SKIP_BITS) | (blocks[None, None, :, None] >= blocks[None, :, None, None])
    first = jnp.argmax(needed, axis=2).astype(jnp.int32)
    return jnp.transpose(first, (0, 2, 1)).reshape(-1)


FOX_GROUP = 8


def _fox_attn_kernel(first_ref, q_ref, qb_ref, k_ref, kb_ref, v_ref, o_ref,
                     vt_ref, m_ref, l_ref, acc_ref, s_ref, *, blk, nblk):
    qi = pl.program_id(2)
    group = FOX_GROUP

    @pl.when(qi == 0)
    def _():
        for c in range(nblk):
            vt_ref[c] = v_ref[c * blk:(c + 1) * blk, :].astype(F32).T.astype(BF16)

    q_aug_t = jnp.concatenate([q_ref[...].astype(F32).T, qb_ref[...].astype(F32).T],
                              axis=0).astype(BF16)
    m_ref[...] = jnp.full_like(m_ref, -jnp.inf)
    l_ref[...] = jnp.zeros_like(l_ref)
    acc_ref[...] = jnp.zeros_like(acc_ref)

    def tiles(kbs, masks):
        n = len(kbs)
        start = pl.multiple_of(kbs[0] * blk, blk)
        k_aug = jnp.concatenate([k_ref[pl.ds(start, n * blk), :], kb_ref[pl.ds(start, n * blk), :]],
                                axis=1)
        s_all = _dot(k_aug, q_aug_t)
        tile_max = []
        for t, masked in enumerate(masks):
            s = s_all[t * blk:(t + 1) * blk, :]
            if masked:
                key = lax.broadcasted_iota(jnp.int32, (blk, blk), 0)
                qry = lax.broadcasted_iota(jnp.int32, (blk, blk), 1)
                s = jnp.where(key <= qry, s, -jnp.inf)
            s_ref[t] = s
            tile_max.append(jnp.max(s, axis=0, keepdims=True))
        m_run = m_ref[...]
        l_run = l_ref[...]
        acc = acc_ref[...]
        for t, kb in enumerate(kbs):
            m_new = jnp.maximum(m_run, tile_max[t])
            alpha = jnp.exp2(m_run - m_new)
            p = jnp.exp2(s_ref[t] - m_new)
            l_run = alpha * l_run + jnp.sum(p, axis=0, keepdims=True)
            acc = alpha * acc + _dot(vt_ref[kb], p.astype(BF16))
            m_run = m_new
        acc_ref[...] = acc
        l_ref[...] = l_run
        m_ref[...] = m_run

    first = first_ref[(pl.program_id(0) * pl.num_programs(1) + pl.program_id(1)) * nblk + qi]
    n_full = qi - first

    def body(j, carry):
        tiles([first + group * j + t for t in range(group)], [False] * group)
        return carry

    lax.fori_loop(0, n_full // group, body, 0)

    for r in range(group):
        @pl.when(n_full % group == r)
        def _(r=r):
            tiles([qi - r + t for t in range(r + 1)], [False] * r + [True])

    o_ref[...] = (acc_ref[...] / l_ref[...]).T.astype(o_ref.dtype)


def fox_attention(proj, key_bias, query_bias, first_tiles, batch, seq):
    blk = FOX_BLOCK
    nq = seq // blk
    m = batch * seq
    nh, dh = FOX_HEADS, FOX_HEAD_DIM
    return pl.pallas_call(
        functools.partial(_fox_attn_kernel, blk=blk, nblk=nq),
        grid=(batch, nh, nq),
        in_specs=[
            pl.BlockSpec(memory_space=pltpu.SMEM),
            pl.BlockSpec((blk, dh), lambda b, h, i: (b * nq + i, h)),
            pl.BlockSpec((blk, LANES), lambda b, h, i: (b * nq + i, h)),
            pl.BlockSpec((seq, dh), lambda b, h, i: (b, nh + h)),
            pl.BlockSpec((seq, LANES), lambda b, h, i: (b, h)),
            pl.BlockSpec((seq, dh), lambda b, h, i: (b, 2 * nh + h)),
        ],
        out_specs=pl.BlockSpec((blk, dh), lambda b, h, i: (b * nq + i, h)),
        out_shape=jax.ShapeDtypeStruct((m, nh * dh), BF16),
        scratch_shapes=[pltpu.VMEM((nq, dh, blk), BF16),
                        pltpu.VMEM((1, blk), F32), pltpu.VMEM((1, blk), F32),
                        pltpu.VMEM((dh, blk), F32),
                        pltpu.VMEM((FOX_GROUP, blk, blk), F32)],
        compiler_params=_cparams("parallel", "parallel", "arbitrary"),
        name="fox_attention",
    )(first_tiles, proj, query_bias, proj, key_bias, proj)


def _pad_cols(t, width):
    return jnp.pad(t, ((0, 0), (0, width - t.shape[1])))


def _even_col_scale():
    cs = jnp.ones((EVEN_MAIN,), F32)
    cs = cs.at[EVEN_QS:EVEN_QS + SWA_Q_HEADS * HEAD_DIM].set(HEAD_DIM ** -0.5 * LOG2E)
    cs = cs.at[EVEN_QM:EVEN_QM + MLSTM_HEADS * MLSTM_QK_DIM].set(MLSTM_QK_DIM ** -0.5)
    return cs.reshape(1, EVEN_MAIN)


def _fox_col_scale():
    cs = jnp.ones((FOX_MAIN,), F32)
    cs = cs.at[:FOX_HEADS * FOX_HEAD_DIM].set(FOX_HEAD_DIM ** -0.5 * LOG2E)
    return cs.reshape(1, FOX_MAIN)


def _even_mixer(h, x_in, gain, w_in, w_out, sinks, b_i, b_f, head_gain, ffn_gain, batch, seq):
    q_s, k_s, v_s, q_m, k_m, v_m, o_m, gate_cols = jnp.split(
        w_in, [1024, 1280, 1536, 2048, 2560, 3584, 4608], axis=1)
    w_main = jnp.concatenate([q_s, v_m, o_m, q_m, k_m, k_s, v_s], axis=1).astype(BF16)
    w_gate = _pad_cols(gate_cols, GATE_PAD).astype(BF16)
    proj, gates = norm_proj(x_in, gain, w_main, _even_col_scale(), w_gate)
    attn = swa_attention(proj, sinks, batch, seq)
    bias_row = _pad_cols(jnp.concatenate([b_i, b_f]).reshape(1, -1), GATE_PAD)
    mem = mlstm(proj, gates, bias_row, head_gain, batch, seq)
    return out_proj_norm([attn, mem], w_out.astype(BF16), h, ffn_gain)


def _fox_mixer(h, x_in, gain, w_in, b_f, w_out, ffn_gain, batch, seq):
    w_main = w_in[:, :FOX_MAIN].astype(BF16)
    w_gate = _pad_cols(w_in[:, FOX_MAIN:], GATE_PAD).astype(BF16)
    proj, gates = norm_proj(x_in, gain, w_main, _fox_col_scale(), w_gate)
    key_bias, query_bias, stats = fox_bias_columns(
        gates, _pad_cols(b_f.reshape(1, -1), GATE_PAD), proj, batch, seq)
    first_tiles = fox_first_tiles(stats, batch, seq)
    attn = fox_attention(proj, key_bias, query_bias, first_tiles, batch, seq)
    return out_proj_norm([attn], w_out.astype(BF16), h, ffn_gain)


def kernel(x, norm_mix, norm_ffn, norm_final, w_in_even, w_out_even, swa_sinks, mlstm_b_i, mlstm_b_f, mlstm_head_gain, w_in_odd, fox_b_f, w_out_odd, w_gate, w_up, w_down):
    batch, seq, d = x.shape
    depth = norm_mix.shape[0]
    h = x.reshape(batch * seq, d)
    x_in = h
    w_gate, w_up, w_down = cast_bf16(w_gate), cast_bf16(w_up), cast_bf16(w_down)
    for layer in range(depth):
        j = layer // 2
        if layer % 2 == 0:
            h, xn = _even_mixer(h, x_in, norm_mix[layer], w_in_even[j], w_out_even[j],
                                swa_sinks[j], mlstm_b_i[j], mlstm_b_f[j], mlstm_head_gain[j],
                                norm_ffn[layer], batch, seq)
        else:
            h, xn = _fox_mixer(h, x_in, norm_mix[layer], w_in_odd[j], fox_b_f[j], w_out_odd[j],
                               norm_ffn[layer], batch, seq)
        act = ffn_up(xn, w_gate, w_up, layer)
        if layer + 1 < depth:
            h, x_in = down_proj_norm(act, w_down, layer, h, norm_mix[layer + 1], last=False)
        else:
            (out,) = down_proj_norm(act, w_down, layer, h, norm_final, last=True)
    return out.reshape(batch, seq, d)
```

```python
import functools

import jax
import jax.numpy as jnp
import numpy as np
from jax import lax
from jax.experimental import pallas as pl
from jax.experimental.pallas import tpu as pltpu

F32 = jnp.float32
BF16 = jnp.bfloat16

RMS_EPS = 1e-5
LOG2E = 1.4426950408889634

HEAD_DIM = 64
SWA_Q_HEADS = 16
SWA_KV_HEADS = 4
SWA_GROUP = SWA_Q_HEADS // SWA_KV_HEADS
WINDOW = 128
SWA_BLOCK = 128

MLSTM_HEADS = 4
MLSTM_V_DIM = 256
MLSTM_QK_DIM = 128
MLSTM_CHUNK = 256
GATE_SOFTCAP = 15.0

FOX_HEAD_DIM = 128
FOX_HEADS = 16

EVEN_MAIN = 4608
EVEN_QS, EVEN_VM, EVEN_OM = 0, 1024, 2048
EVEN_QM, EVEN_KM, EVEN_KS, EVEN_VS = 3072, 3584, 4096, 4352
FOX_MAIN = 3 * FOX_HEADS * FOX_HEAD_DIM

LANES = 128
SUBLANES = 8
GATE_PAD = LANES
VMEM_LIMIT = 56 * 1024 * 1024

PROJ_TM, PROJ_TN = 1024, 1536
FFN_TM, FFN_TN = 2048, 512
FFN_SUB_ROWS = 1024
OUT_TM = 512
DOWN_TM = 256
CAST_ROWS = 512


def _cparams(*semantics):
    return pltpu.CompilerParams(dimension_semantics=semantics, vmem_limit_bytes=VMEM_LIMIT)


def _rms_norm_rows(x, gain):
    ms = jnp.mean(x * x, axis=-1, keepdims=True)
    return x * lax.rsqrt(ms + RMS_EPS) * gain


def _log_sigmoid(x):
    return jnp.minimum(x, 0.0) - jnp.log(1.0 + jnp.exp(-jnp.abs(x)))


def _sigmoid(x):
    return 1.0 / (1.0 + jnp.exp(-x))


def _split3_bf16(x):
    hi = x.astype(BF16)
    r1 = x - hi.astype(F32)
    mid = r1.astype(BF16)
    lo = (r1 - mid.astype(F32)).astype(BF16)
    return hi, mid, lo


def _dot(a, b):
    return jnp.dot(a, b, preferred_element_type=F32)


def _dot_nt(a, b):
    return lax.dot_general(a, b, (((1,), (1,)), ((), ())), preferred_element_type=F32)


def _norm_proj_kernel(x_ref, g_ref, w_ref, cs_ref, wg_ref, o_ref, og_ref, xn_ref, *, normed):
    lhs_ref = x_ref if normed else xn_ref

    @pl.when(pl.program_id(1) == 0)
    def _():
        if not normed:
            xn_ref[...] = _rms_norm_rows(x_ref[...], g_ref[...]).astype(BF16)
        og_ref[...] = _dot(lhs_ref[...], wg_ref[...])

    o_ref[...] = (_dot(lhs_ref[...], w_ref[...]) * cs_ref[...]).astype(o_ref.dtype)


def norm_proj(x, gain, w, col_scale, w_gate, *, tm=PROJ_TM, tn=PROJ_TN):
    m, d = x.shape
    n = w.shape[1]
    tm = min(tm, m)
    return pl.pallas_call(
        functools.partial(_norm_proj_kernel, normed=(x.dtype == BF16)),
        grid=(m // tm, n // tn),
        in_specs=[
            pl.BlockSpec((tm, d), lambda i, j: (i, 0)),
            pl.BlockSpec((1, d), lambda i, j: (0, 0)),
            pl.BlockSpec((d, tn), lambda i, j: (0, j)),
            pl.BlockSpec((1, tn), lambda i, j: (0, j)),
            pl.BlockSpec((d, GATE_PAD), lambda i, j: (0, 0)),
        ],
        out_specs=[
            pl.BlockSpec((tm, tn), lambda i, j: (i, j)),
            pl.BlockSpec((tm, GATE_PAD), lambda i, j: (i, 0)),
        ],
        out_shape=[
            jax.ShapeDtypeStruct((m, n), BF16),
            jax.ShapeDtypeStruct((m, GATE_PAD), F32),
        ],
        scratch_shapes=[pltpu.VMEM((tm, d), BF16)],
        compiler_params=_cparams("parallel", "arbitrary"),
        name="norm_proj",
    )(x, gain.reshape(1, d), w, col_scale, w_gate)


def _out_proj_norm_kernel(*refs, n_lhs):
    a_refs = refs[:n_lhs]
    w_ref, r_ref, g_ref, h_ref, xn_ref = refs[n_lhs:n_lhs + 5]
    acc = r_ref[...]
    row = 0
    for a_ref in a_refs:
        width = a_ref.shape[1]
        acc = acc + _dot(a_ref[...], w_ref[row:row + width, :])
        row += width
    h_ref[...] = acc
    xn_ref[...] = _rms_norm_rows(acc, g_ref[...]).astype(xn_ref.dtype)


def out_proj_norm(lhs_list, w, res, gain, *, tm=OUT_TM):
    m, n = res.shape
    tm = min(tm, m)
    in_specs = [pl.BlockSpec((tm, a.shape[1]), lambda i: (i, 0)) for a in lhs_list]
    in_specs += [pl.BlockSpec(w.shape, lambda i: (0, 0)),
                 pl.BlockSpec((tm, n), lambda i: (i, 0)), pl.BlockSpec((1, n), lambda i: (0, 0))]
    return pl.pallas_call(
        functools.partial(_out_proj_norm_kernel, n_lhs=len(lhs_list)),
        grid=(m // tm,),
        in_specs=in_specs,
        out_specs=[pl.BlockSpec((tm, n), lambda i: (i, 0)), pl.BlockSpec((tm, n), lambda i: (i, 0))],
        out_shape=[jax.ShapeDtypeStruct((m, n), F32), jax.ShapeDtypeStruct((m, n), BF16)],
        compiler_params=_cparams("parallel"),
        name="out_proj_norm",
    )(*lhs_list, w, res, gain.reshape(1, n))


def _down_proj_norm_kernel(a_ref, w_ref, r_ref, g_ref, *out_refs, last):
    acc = r_ref[...] + _dot(a_ref[...], w_ref[...])
    normed = _rms_norm_rows(acc, g_ref[...])
    if last:
        out_refs[0][...] = normed
    else:
        out_refs[0][...] = acc
        out_refs[1][...] = normed.astype(out_refs[1].dtype)


def down_proj_norm(act, w_stack, layer, res, gain, *, last, tm=DOWN_TM):
    m, n = res.shape
    k = act.shape[1]
    tm = min(tm, m)
    rows = lambda width: pl.BlockSpec((tm, width), lambda i: (i, 0))
    out_shape = [jax.ShapeDtypeStruct((m, n), F32)]
    if not last:
        out_shape.append(jax.ShapeDtypeStruct((m, n), BF16))
    return pl.pallas_call(
        functools.partial(_down_proj_norm_kernel, last=last),
        grid=(m // tm,),
        in_specs=[rows(k),
                  pl.BlockSpec((None, k, n), lambda i: (layer, 0, 0),
                               pipeline_mode=pl.Buffered(1)),
                  rows(n),
                  pl.BlockSpec((1, n), lambda i: (0, 0))],
        out_specs=[rows(n)] * len(out_shape),
        out_shape=out_shape,
        compiler_params=_cparams("parallel"),
        name="down_proj_norm",
    )(act, w_stack, res, gain.reshape(1, n))


def _ffn_up_kernel(xn_ref, wg_ref, wu_ref, o_ref):
    for r in range(0, xn_ref.shape[0], FFN_SUB_ROWS):
        xn = xn_ref[r:r + FFN_SUB_ROWS, :]
        gate = _dot(xn, wg_ref[...])
        up = _dot(xn, wu_ref[...])
        o_ref[r:r + FFN_SUB_ROWS, :] = (gate * _sigmoid(gate) * up).astype(o_ref.dtype)


def ffn_up(xn, w_gate, w_up, layer, *, tm=FFN_TM, tn=FFN_TN):
    m, d = xn.shape
    n = w_gate.shape[2]
    tm = min(tm, m)
    return pl.pallas_call(
        _ffn_up_kernel,
        grid=(m // tm, n // tn),
        in_specs=[
            pl.BlockSpec((tm, d), lambda i, j: (i, 0)),
            pl.BlockSpec((None, d, tn), lambda i, j: (layer, 0, j)),
            pl.BlockSpec((None, d, tn), lambda i, j: (layer, 0, j)),
        ],
        out_specs=pl.BlockSpec((tm, tn), lambda i, j: (i, j)),
        out_shape=jax.ShapeDtypeStruct((m, n), BF16),
        compiler_params=_cparams("parallel", "arbitrary"),
        name="ffn_up",
    )(xn, w_gate, w_up)


def _cast_kernel(x_ref, o_ref):
    o_ref[...] = x_ref[...].astype(o_ref.dtype)


def cast_bf16(w, *, rows=CAST_ROWS):
    lead, r, c = w.shape
    flat = w.reshape(lead * r, c)
    out = pl.pallas_call(
        _cast_kernel,
        grid=(lead * r // rows,),
        in_specs=[pl.BlockSpec((rows, c), lambda i: (i, 0))],
        out_specs=pl.BlockSpec((rows, c), lambda i: (i, 0)),
        out_shape=jax.ShapeDtypeStruct(flat.shape, BF16),
        compiler_params=_cparams("parallel"),
        name="cast_bf16",
    )(flat)
    return out.reshape(lead, r, c)


def _swa_kernel(sinks_ref, q_ref, kc_ref, kp_ref, vc_ref, vp_ref, o_ref, bias_ref):
    blk, dh = SWA_BLOCK, HEAD_DIM
    nkv = SWA_KV_HEADS
    keys = 2 * blk

    @pl.when((pl.program_id(0) == 0) & (pl.program_id(1) == 0))
    def _():
        kj = lax.broadcasted_iota(jnp.int32, (keys, blk), 0)
        qi = lax.broadcasted_iota(jnp.int32, (keys, blk), 1) + blk
        dist = qi - kj
        in_window = (dist >= 0) & (dist < WINDOW)
        distf = dist.astype(F32)
        for g in range(nkv):
            for first, valid in ((0, in_window & (kj >= blk)), (1, in_window)):
                quad = [jnp.where(valid, -(2.0 ** (-8.0 * (4 * g + hh + 1) / SWA_Q_HEADS) * LOG2E)
                                  * distf, -jnp.inf) for hh in range(SWA_GROUP)]
                bias_ref[first, g] = jnp.concatenate(
                    [jnp.concatenate([quad[0], quad[2]], axis=1),
                     jnp.concatenate([quad[1], quad[3]], axis=1)], axis=0)

    which = jnp.where(pl.program_id(1) > 0, 1, 0)
    k = jnp.concatenate([kp_ref[...], kc_ref[...]], axis=0).astype(F32)
    v_t = jnp.concatenate([vp_ref[...], vc_ref[...]], axis=0).astype(F32).T
    lane = lax.broadcasted_iota(jnp.int32, (keys, LANES), 1)
    low = lane < dh

    def key_operand(g):
        kblk = k[:, (g // 2) * LANES:(g // 2 + 1) * LANES]
        swapped = pltpu.roll(kblk, dh, axis=1)
        k_low = jnp.where(low, kblk if g % 2 == 0 else swapped, 0.0)
        k_high = jnp.where(low, 0.0, swapped if g % 2 == 0 else kblk)
        return jnp.concatenate([k_low, k_high], axis=0).astype(BF16)

    def value_operand(g):
        vh_t = v_t[g * dh:(g + 1) * dh, :]
        zero = jnp.zeros_like(vh_t)
        return jnp.concatenate([jnp.concatenate([vh_t, zero], axis=1),
                                jnp.concatenate([zero, vh_t], axis=1)], axis=0).astype(BF16)

    scores = []
    for g in range(nkv):
        q2 = jnp.concatenate([q_ref[:, (2 * g) * LANES:(2 * g + 1) * LANES],
                              q_ref[:, (2 * g + 1) * LANES:(2 * g + 2) * LANES]], axis=0)
        scores.append(_dot_nt(key_operand(g), q2))
    probs, scales = [], []
    for g in range(nkv):
        s = scores[g] + bias_ref[which, g]
        halves, inv = [], []
        for half in range(2):
            sh = s[half * keys:(half + 1) * keys, :]
            sink = jnp.concatenate(
                [jnp.full((1, blk), sinks_ref[4 * g + 2 * pair + half] * LOG2E, F32)
                 for pair in range(2)], axis=1)
            m = jnp.maximum(jnp.max(sh, axis=0, keepdims=True), sink)
            p = jnp.exp2(sh - m)
            den = jnp.sum(p, axis=0, keepdims=True) + jnp.exp2(sink - m)
            halves.append(p.astype(BF16))
            inv.append(jnp.broadcast_to(1.0 / den, (dh, 2 * blk)))
        probs.append(jnp.concatenate(halves, axis=0))
        scales.append(jnp.concatenate(inv, axis=0))
    for g in range(nkv):
        o_t = _dot(value_operand(g), probs[g]) * scales[g]
        for pair in range(2):
            o_ref[:, (2 * g + pair) * LANES:(2 * g + pair + 1) * LANES] = (
                o_t[:, pair * blk:(pair + 1) * blk].T.astype(o_ref.dtype))


def swa_attention(proj, sinks, batch, seq):
    blk = SWA_BLOCK
    nb = seq // blk
    m = batch * seq
    qw = SWA_Q_HEADS * HEAD_DIM
    kw = SWA_KV_HEADS * HEAD_DIM
    cur = lambda b, n: b * nb + n
    prev = lambda b, n: b * nb + jnp.maximum(n - 1, 0)
    return pl.pallas_call(
        _swa_kernel,
        grid=(batch, nb),
        in_specs=[
            pl.BlockSpec(memory_space=pltpu.SMEM),
            pl.BlockSpec((blk, qw), lambda b, n: (cur(b, n), EVEN_QS // qw)),
            pl.BlockSpec((blk, kw), lambda b, n: (cur(b, n), EVEN_KS // kw)),
            pl.BlockSpec((blk, kw), lambda b, n: (prev(b, n), EVEN_KS // kw)),
            pl.BlockSpec((blk, kw), lambda b, n: (cur(b, n), EVEN_VS // kw)),
            pl.BlockSpec((blk, kw), lambda b, n: (prev(b, n), EVEN_VS // kw)),
        ],
        out_specs=pl.BlockSpec((blk, qw), lambda b, n: (cur(b, n), 0)),
        out_shape=jax.ShapeDtypeStruct((m, qw), BF16),
        scratch_shapes=[pltpu.VMEM((2, SWA_KV_HEADS, 4 * blk, 2 * blk), F32)],
        compiler_params=_cparams("arbitrary", "arbitrary"),
        name="swa_attention",
    )(sinks, proj, proj, proj, proj, proj)


def _mlstm_kernel(g_ref, bias_ref, gain_ref, q_ref, k_ref, v_ref, og_ref, o_ref,
                  c_ref, m_ref, qk_ref, qc_ref):
    L = MLSTM_CHUNK
    dk, dv = MLSTM_QK_DIM, MLSTM_V_DIM
    nh = MLSTM_HEADS

    @pl.when(pl.program_id(1) == 0)
    def _():
        c_ref[...] = jnp.zeros_like(c_ref)
        m_ref[...] = jnp.zeros_like(m_ref)

    a_all = GATE_SOFTCAP * jnp.tanh((g_ref[...] + bias_ref[...]) / GATE_SOFTCAP)
    lf_all = _log_sigmoid(a_all)
    row = lax.broadcasted_iota(jnp.int32, (L, L), 0)
    col = lax.broadcasted_iota(jnp.int32, (L, L), 1)
    causal = row >= col
    tri = jnp.where(causal, 1.0, 0.0).astype(BF16)
    hi, mid, lo = _split3_bf16(lf_all)
    b_all = _dot(tri, hi) + _dot(tri, mid) + _dot(tri, lo)
    a_all_t = a_all.T
    b_all_t = b_all.T
    ones = jnp.ones((L, LANES), BF16)

    for h in range(nh):
        q = q_ref[:, h * dk:(h + 1) * dk]
        qk_ref[h] = _dot_nt(q, k_ref[:, h * dk:(h + 1) * dk])
        qc_ref[h] = _dot(q, c_ref[h].astype(BF16))

    for h in range(nh):
        li_r = a_all_t[h:h + 1, :]
        b_c = b_all[:, nh + h:nh + h + 1]
        b_r = b_all_t[nh + h:nh + h + 1, :]
        b_last = b_r[:, L - 1:L]
        m_prev = m_ref[h:h + 1, 0:1]

        a_r = b_last - b_r + li_r
        a_max = jnp.max(a_r, axis=-1, keepdims=True)
        w_r = jnp.exp(a_r - a_max)

        log_d = jnp.where(causal, b_c - b_r + li_r, -jnp.inf)
        g_c = b_c + m_prev
        m_t = jnp.maximum(g_c, jnp.max(log_d, axis=-1, keepdims=True))
        d_mat = jnp.exp(log_d - m_t)
        inter = jnp.exp(g_c - m_t)

        k = k_ref[:, h * dk:(h + 1) * dk]
        v_aug = jnp.concatenate([v_ref[:, h * dv:(h + 1) * dv], ones], axis=1)
        p = (qk_ref[h] * d_mat).astype(BF16)
        c_prev = c_ref[h]
        num = _dot(p, v_aug) + inter * qc_ref[h]
        den = num[:, dv:dv + 1]
        hh = num[:, :dv] / jnp.maximum(jnp.abs(den), jnp.exp(-m_t))
        hn = _rms_norm_rows(hh, gain_ref[:, h * dv:(h + 1) * dv])
        gate = _sigmoid(og_ref[:, h * dv:(h + 1) * dv].astype(F32))
        o_ref[:, h * dv:(h + 1) * dv] = (gate * hn).astype(o_ref.dtype)

        m_new = jnp.maximum(b_last + m_prev, a_max)
        decay = jnp.exp(b_last + m_prev - m_new)
        scl = jnp.exp(a_max - m_new)
        kw_t = (k.astype(F32).T * w_r).astype(BF16)
        c_ref[h] = decay * c_prev + scl * _dot(kw_t, v_aug)
        m_ref[h:h + 1, :] = jnp.broadcast_to(m_new, (1, LANES))


def mlstm(proj, gates, bias_row, head_gain, batch, seq):
    L = MLSTM_CHUNK
    nc = seq // L
    m = batch * seq
    nh, dk, dv = MLSTM_HEADS, MLSTM_QK_DIM, MLSTM_V_DIM
    rows = lambda b, c: b * nc + c
    return pl.pallas_call(
        _mlstm_kernel,
        grid=(batch, nc),
        in_specs=[
            pl.BlockSpec((L, GATE_PAD), lambda b, c: (rows(b, c), 0)),
            pl.BlockSpec((1, GATE_PAD), lambda b, c: (0, 0)),
            pl.BlockSpec((1, nh * dv), lambda b, c: (0, 0)),
            pl.BlockSpec((L, nh * dk), lambda b, c: (rows(b, c), EVEN_QM // (nh * dk))),
            pl.BlockSpec((L, nh * dk), lambda b, c: (rows(b, c), EVEN_KM // (nh * dk))),
            pl.BlockSpec((L, nh * dv), lambda b, c: (rows(b, c), EVEN_VM // (nh * dv))),
            pl.BlockSpec((L, nh * dv), lambda b, c: (rows(b, c), EVEN_OM // (nh * dv))),
        ],
        out_specs=pl.BlockSpec((L, nh * dv), lambda b, c: (rows(b, c), 0)),
        out_shape=jax.ShapeDtypeStruct((m, nh * dv), BF16),
        scratch_shapes=[pltpu.VMEM((nh, dk, dv + LANES), F32),
                        pltpu.VMEM((SUBLANES, LANES), F32),
                        pltpu.VMEM((nh, L, L), F32),
                        pltpu.VMEM((nh, L, dv + LANES), F32)],
        compiler_params=_cparams("parallel", "arbitrary"),
        name="mlstm",
    )(gates, bias_row, head_gain.reshape(1, nh * dv), proj, proj, proj, proj)


BIAS_COLS = 3


def _fox_selectors():
    width = FOX_HEADS * LANES
    sel_k = np.zeros((GATE_PAD, width), np.float32)
    sel_q = np.zeros((GATE_PAD, width), np.float32)
    const_k = np.zeros((1, width), np.float32)
    const_q = np.zeros((1, width), np.float32)
    for h in range(FOX_HEADS):
        for part in range(BIAS_COLS):
            sel_k[part * FOX_HEADS + h, h * LANES + part] = 1.0
            sel_q[part * FOX_HEADS + h, h * LANES + BIAS_COLS + part] = 1.0
            const_k[0, h * LANES + BIAS_COLS + part] = 1.0
            const_q[0, h * LANES + part] = -1.0
    return (jnp.asarray(sel_k, BF16), jnp.asarray(sel_q, BF16),
            jnp.asarray(const_k), jnp.asarray(const_q))


FOX_BLOCK = 512
STAT_ROWS = SUBLANES


def _head_norms_sq(x_ref, gsum):
    return jnp.max(_dot(jnp.square(x_ref[...]), gsum), axis=0, keepdims=True)


def _fox_bias_kernel(g_ref, bias_ref, selk_ref, selq_ref, ck_ref, cq_ref, gsum_ref, q_ref, k_ref,
                     kb_ref, qb_ref, stat_ref, carry_ref, *, ts):
    @pl.when(pl.program_id(1) == 0)
    def _():
        carry_ref[...] = jnp.zeros_like(carry_ref)

    lf = _log_sigmoid(g_ref[...] + bias_ref[...])
    row = lax.broadcasted_iota(jnp.int32, (ts, ts), 0)
    col = lax.broadcasted_iota(jnp.int32, (ts, ts), 1)
    tri = jnp.where(row >= col, 1.0, 0.0).astype(BF16)
    hi, mid, lo = _split3_bf16(lf)
    c = _dot(tri, hi) + _dot(tri, mid) + _dot(tri, lo) + carry_ref[0:1, :]
    carry_ref[...] = jnp.broadcast_to(c[ts - 1:ts, :], carry_ref.shape)
    c2 = c * LOG2E
    p_hi, p_mid, p_lo = (t.astype(F32) for t in _split3_bf16(c2))
    lane = lax.broadcasted_iota(jnp.int32, (ts, GATE_PAD), 1)
    parts = jnp.where(lane < FOX_HEADS, p_hi,
                      jnp.where(lane < 2 * FOX_HEADS, pltpu.roll(p_mid, FOX_HEADS, axis=1),
                                pltpu.roll(p_lo, 2 * FOX_HEADS, axis=1))).astype(BF16)
    kb_ref[...] = (_dot(parts, selk_ref[...]) + ck_ref[...]).astype(BF16)
    qb_ref[...] = (_dot(parts, selq_ref[...]) + cq_ref[...]).astype(BF16)
    gsum = gsum_ref[...]
    stat_ref[0] = jnp.concatenate(
        [_head_norms_sq(q_ref, gsum), _head_norms_sq(k_ref, gsum), c2[0:1, :], c2[ts - 1:ts, :],
         jnp.zeros((STAT_ROWS - 4, GATE_PAD), F32)], axis=0)


def fox_bias_columns(gates, bias_row, proj, batch, seq):
    ts = FOX_BLOCK
    ns = seq // ts
    m = batch * seq
    width = FOX_HEADS * LANES
    sel_k, sel_q, const_k, const_q = _fox_selectors()
    gsum = np.zeros((width, GATE_PAD), np.float32)
    for h in range(FOX_HEADS):
        gsum[h * FOX_HEAD_DIM:(h + 1) * FOX_HEAD_DIM, h] = 1.0
    gsum = jnp.asarray(gsum, BF16)
    whole = lambda shape: pl.BlockSpec(shape, lambda b, s: (0, 0))
    rows = lambda b, s: b * ns + s
    return pl.pallas_call(
        functools.partial(_fox_bias_kernel, ts=ts),
        grid=(batch, ns),
        in_specs=[pl.BlockSpec((ts, GATE_PAD), lambda b, s: (rows(b, s), 0)),
                  whole((1, GATE_PAD)),
                  whole(sel_k.shape), whole(sel_q.shape),
                  whole((1, width)), whole((1, width)), whole(gsum.shape),
                  pl.BlockSpec((ts, width), lambda b, s: (rows(b, s), 0)),
                  pl.BlockSpec((ts, width), lambda b, s: (rows(b, s), 1))],
        out_specs=[pl.BlockSpec((ts, width), lambda b, s: (rows(b, s), 0)),
                   pl.BlockSpec((ts, width), lambda b, s: (rows(b, s), 0)),
                   pl.BlockSpec((1, STAT_ROWS, GATE_PAD), lambda b, s: (rows(b, s), 0, 0))],
        out_shape=[jax.ShapeDtypeStruct((m, width), BF16),
                   jax.ShapeDtypeStruct((m, width), BF16),
                   jax.ShapeDtypeStruct((batch * ns, STAT_ROWS, GATE_PAD), F32)],
        scratch_shapes=[pltpu.VMEM((SUBLANES, GATE_PAD), F32)],
        compiler_params=_cparams("parallel", "arbitrary"),
        name="fox_bias_columns",
    )(gates, bias_row, sel_k, sel_q, const_k, const_q, gsum, proj, proj)


FOX_SKIP_BITS = 170.0
FOX_NORM_MARGIN = 1.02


def fox_first_tiles(stats, batch, seq):
    nq = seq // FOX_BLOCK
    st = stats.reshape(batch, nq, STAT_ROWS, GATE_PAD)[..., :FOX_HEADS]
    qn = jnp.sqrt(st[:, :, 0, :]) * FOX_NORM_MARGIN
    kn = jnp.sqrt(st[:, :, 1, :])
    c_first, c_last = st[:, :, 2, :], st[:, :, 3, :]
    upper = (qn[:, :, None, :] * kn[:, None, :, :]
             + c_first[:, :, None, :] - c_last[:, None, :, :])
    row_max_lower = -(qn * kn)[:, :, None, :]
    blocks = jnp.arange(nq)
    on_or_after_diagonal = blocks[None, None, :, None] >= blocks[None, :, None, None]
    needed = (upper - row_max_lower >= -FOX_SKIP_BITS) | on_or_after_diagonal
    first = jnp.argmax(needed, axis=2).astype(jnp.int32)
    return jnp.transpose(first, (0, 2, 1)).reshape(-1)


FOX_GROUP = 8


def _fox_attn_kernel(first_ref, q_ref, qb_ref, k_ref, kb_ref, v_ref, o_ref,
                     vt_ref, m_ref, l_ref, acc_ref, s_ref, *, blk, nblk):
    qi = pl.program_id(2)
    group = FOX_GROUP

    @pl.when(qi == 0)
    def _():
        for c in range(nblk):
            vt_ref[c] = v_ref[c * blk:(c + 1) * blk, :].T

    q_aug_t = jnp.concatenate([q_ref[...].T, qb_ref[...].T], axis=0)
    m_ref[...] = jnp.full_like(m_ref, -jnp.inf)
    l_ref[...] = jnp.zeros_like(l_ref)
    acc_ref[...] = jnp.zeros_like(acc_ref)

    def tiles(kbs, masks):
        n = len(kbs)
        start = pl.multiple_of(kbs[0] * blk, blk)
        k_aug = jnp.concatenate([k_ref[pl.ds(start, n * blk), :], kb_ref[pl.ds(start, n * blk), :]],
                                axis=1)
        s_all = _dot(k_aug, q_aug_t)
        tile_max = []
        for t, masked in enumerate(masks):
            s = s_all[t * blk:(t + 1) * blk, :]
            if masked:
                key = lax.broadcasted_iota(jnp.int32, (blk, blk), 0)
                qry = lax.broadcasted_iota(jnp.int32, (blk, blk), 1)
                s = jnp.where(key <= qry, s, -jnp.inf)
            s_ref[t] = s
            tile_max.append(jnp.max(s, axis=0, keepdims=True))
        m_run = m_ref[...]
        l_run = l_ref[...]
        acc = acc_ref[...]
        for t, kb in enumerate(kbs):
            m_new = jnp.maximum(m_run, tile_max[t])
            alpha = jnp.exp2(m_run - m_new)
            p = jnp.exp2(s_ref[t] - m_new)
            l_run = alpha * l_run + jnp.sum(p, axis=0, keepdims=True)
            acc = alpha * acc + _dot(vt_ref[kb], p.astype(BF16))
            m_run = m_new
        acc_ref[...] = acc
        l_ref[...] = l_run
        m_ref[...] = m_run

    first = first_ref[(pl.program_id(0) * pl.num_programs(1) + pl.program_id(1)) * nblk + qi]
    n_full = qi - first

    def body(j, carry):
        tiles([first + group * j + t for t in range(group)], [False] * group)
        return carry

    lax.fori_loop(0, n_full // group, body, 0)

    for r in range(group):
        @pl.when(n_full % group == r)
        def _(r=r):
            tiles([qi - r + t for t in range(r + 1)], [False] * r + [True])

    o_ref[...] = (acc_ref[...] / l_ref[...]).T.astype(o_ref.dtype)


def fox_attention(proj, key_bias, query_bias, first_tiles, batch, seq):
    blk = FOX_BLOCK
    nq = seq // blk
    m = batch * seq
    nh, dh = FOX_HEADS, FOX_HEAD_DIM
    return pl.pallas_call(
        functools.partial(_fox_attn_kernel, blk=blk, nblk=nq),
        grid=(batch, nh, nq),
        in_specs=[
            pl.BlockSpec(memory_space=pltpu.SMEM),
            pl.BlockSpec((blk, dh), lambda b, h, i: (b * nq + i, h)),
            pl.BlockSpec((blk, LANES), lambda b, h, i: (b * nq + i, h)),
            pl.BlockSpec((seq, dh), lambda b, h, i: (b, nh + h)),
            pl.BlockSpec((seq, LANES), lambda b, h, i: (b, h)),
            pl.BlockSpec((seq, dh), lambda b, h, i: (b, 2 * nh + h)),
        ],
        out_specs=pl.BlockSpec((blk, dh), lambda b, h, i: (b * nq + i, h)),
        out_shape=jax.ShapeDtypeStruct((m, nh * dh), BF16),
        scratch_shapes=[pltpu.VMEM((nq, dh, blk), BF16),
                        pltpu.VMEM((1, blk), F32), pltpu.VMEM((1, blk), F32),
                        pltpu.VMEM((dh, blk), F32),
                        pltpu.VMEM((FOX_GROUP, blk, blk), F32)],
        compiler_params=_cparams("parallel", "parallel", "arbitrary"),
        name="fox_attention",
    )(first_tiles, proj, query_bias, proj, key_bias, proj)


def _pad_cols(t, width):
    return jnp.pad(t, ((0, 0), (0, width - t.shape[1])))


def _even_col_scale():
    cs = jnp.ones((EVEN_MAIN,), F32)
    cs = cs.at[EVEN_QS:EVEN_QS + SWA_Q_HEADS * HEAD_DIM].set(HEAD_DIM ** -0.5 * LOG2E)
    cs = cs.at[EVEN_QM:EVEN_QM + MLSTM_HEADS * MLSTM_QK_DIM].set(MLSTM_QK_DIM ** -0.5)
    return cs.reshape(1, EVEN_MAIN)


def _fox_col_scale():
    cs = jnp.ones((FOX_MAIN,), F32)
    cs = cs.at[:FOX_HEADS * FOX_HEAD_DIM].set(FOX_HEAD_DIM ** -0.5 * LOG2E)
    return cs.reshape(1, FOX_MAIN)


def _even_mixer(h, x_in, gain, w_in, w_out, sinks, b_i, b_f, head_gain, ffn_gain, batch, seq):
    q_s, k_s, v_s, q_m, k_m, v_m, o_m, gate_cols = jnp.split(
        w_in, [1024, 1280, 1536, 2048, 2560, 3584, 4608], axis=1)
    w_main = jnp.concatenate([q_s, v_m, o_m, q_m, k_m, k_s, v_s], axis=1).astype(BF16)
    w_gate = _pad_cols(gate_cols, GATE_PAD).astype(BF16)
    proj, gates = norm_proj(x_in, gain, w_main, _even_col_scale(), w_gate)
    attn = swa_attention(proj, sinks, batch, seq)
    bias_row = _pad_cols(jnp.concatenate([b_i, b_f]).reshape(1, -1), GATE_PAD)
    mem = mlstm(proj, gates, bias_row, head_gain, batch, seq)
    return out_proj_norm([attn, mem], w_out.astype(BF16), h, ffn_gain)


def _fox_mixer(h, x_in, gain, w_in, b_f, w_out, ffn_gain, batch, seq):
    w_main = w_in[:, :FOX_MAIN].astype(BF16)
    w_gate = _pad_cols(w_in[:, FOX_MAIN:], GATE_PAD).astype(BF16)
    proj, gates = norm_proj(x_in, gain, w_main, _fox_col_scale(), w_gate)
    key_bias, query_bias, stats = fox_bias_columns(
        gates, _pad_cols(b_f.reshape(1, -1), GATE_PAD), proj, batch, seq)
    first_tiles = fox_first_tiles(stats, batch, seq)
    attn = fox_attention(proj, key_bias, query_bias, first_tiles, batch, seq)
    return out_proj_norm([attn], w_out.astype(BF16), h, ffn_gain)


def kernel(x, norm_mix, norm_ffn, norm_final, w_in_even, w_out_even, swa_sinks, mlstm_b_i, mlstm_b_f, mlstm_head_gain, w_in_odd, fox_b_f, w_out_odd, w_gate, w_up, w_down):
    batch, seq, d = x.shape
    depth = norm_mix.shape[0]
    h = x.reshape(batch * seq, d)
    x_in = h
    w_gate, w_up, w_down = cast_bf16(w_gate), cast_bf16(w_up), cast_bf16(w_down)
    for layer in range(depth):
        j = layer // 2
        if layer % 2 == 0:
            h, xn = _even_mixer(h, x_in, norm_mix[layer], w_in_even[j], w_out_even[j],
                                swa_sinks[j], mlstm_b_i[j], mlstm_b_f[j], mlstm_head_gain[j],
                                norm_ffn[layer], batch, seq)
        else:
            h, xn = _fox_mixer(h, x_in, norm_mix[layer], w_in_odd[j], fox_b_f[j], w_out_odd[j],
                               norm_ffn[layer], batch, seq)
        act = ffn_up(xn, w_gate, w_up, layer)
        if layer + 1 < depth:
            h, x_in = down_proj_norm(act, w_down, layer, h, norm_mix[layer + 1], last=False)
        else:
            (out,) = down_proj_norm(act, w_down, layer, h, norm_final, last=True)
    return out.reshape(batch, seq, d)
```

```python
import functools

import jax
import jax.numpy as jnp
import numpy as np
from jax import lax
from jax.experimental import pallas as pl
from jax.experimental.pallas import tpu as pltpu

F32 = jnp.float32
BF16 = jnp.bfloat16

RMS_EPS = 1e-5
LOG2E = 1.4426950408889634

HEAD_DIM = 64
SWA_Q_HEADS = 16
SWA_KV_HEADS = 4
SWA_GROUP = SWA_Q_HEADS // SWA_KV_HEADS
WINDOW = 128
SWA_BLOCK = 128

MLSTM_HEADS = 4
MLSTM_V_DIM = 256
MLSTM_QK_DIM = 128
MLSTM_CHUNK = 256
GATE_SOFTCAP = 15.0

FOX_HEAD_DIM = 128
FOX_HEADS = 16

EVEN_MAIN = 4608
EVEN_QS, EVEN_VM, EVEN_OM = 0, 1024, 2048
EVEN_QM, EVEN_KM, EVEN_KS, EVEN_VS = 3072, 3584, 4096, 4352
FOX_MAIN = 3 * FOX_HEADS * FOX_HEAD_DIM

LANES = 128
SUBLANES = 8
GATE_PAD = LANES
VMEM_LIMIT = 56 * 1024 * 1024

PROJ_TM, PROJ_TN = 1024, 1536
FFN_TM, FFN_TN = 2048, 512
FFN_SUB_ROWS = 1024
OUT_TM = 512
DOWN_TM = 256
CAST_ROWS = 512


def _cparams(*semantics):
    return pltpu.CompilerParams(dimension_semantics=semantics, vmem_limit_bytes=VMEM_LIMIT)


def _rms_norm_rows(x, gain):
    ms = jnp.mean(x * x, axis=-1, keepdims=True)
    return x * lax.rsqrt(ms + RMS_EPS) * gain


def _log_sigmoid(x):
    return jnp.minimum(x, 0.0) - jnp.log(1.0 + jnp.exp(-jnp.abs(x)))


def _sigmoid(x):
    return 1.0 / (1.0 + jnp.exp(-x))


def _split3_bf16(x):
    hi = x.astype(BF16)
    r1 = x - hi.astype(F32)
    mid = r1.astype(BF16)
    lo = (r1 - mid.astype(F32)).astype(BF16)
    return hi, mid, lo


def _dot(a, b):
    return jnp.dot(a, b, preferred_element_type=F32)


def _dot_nt(a, b):
    return lax.dot_general(a, b, (((1,), (1,)), ((), ())), preferred_element_type=F32)


def _norm_proj_kernel(x_ref, g_ref, w_ref, cs_ref, wg_ref, o_ref, og_ref, xn_ref, *, normed):
    lhs_ref = x_ref if normed else xn_ref

    @pl.when(pl.program_id(1) == 0)
    def _():
        if not normed:
            xn_ref[...] = _rms_norm_rows(x_ref[...], g_ref[...]).astype(BF16)
        og_ref[...] = _dot(lhs_ref[...], wg_ref[...])

    o_ref[...] = (_dot(lhs_ref[...], w_ref[...]) * cs_ref[...]).astype(o_ref.dtype)


def norm_proj(x, gain, w, col_scale, w_gate, *, tm=PROJ_TM, tn=PROJ_TN):
    m, d = x.shape
    n = w.shape[1]
    tm = min(tm, m)
    return pl.pallas_call(
        functools.partial(_norm_proj_kernel, normed=(x.dtype == BF16)),
        grid=(m // tm, n // tn),
        in_specs=[
            pl.BlockSpec((tm, d), lambda i, j: (i, 0)),
            pl.BlockSpec((1, d), lambda i, j: (0, 0)),
            pl.BlockSpec((d, tn), lambda i, j: (0, j)),
            pl.BlockSpec((1, tn), lambda i, j: (0, j)),
            pl.BlockSpec((d, GATE_PAD), lambda i, j: (0, 0)),
        ],
        out_specs=[
            pl.BlockSpec((tm, tn), lambda i, j: (i, j)),
            pl.BlockSpec((tm, GATE_PAD), lambda i, j: (i, 0)),
        ],
        out_shape=[
            jax.ShapeDtypeStruct((m, n), BF16),
            jax.ShapeDtypeStruct((m, GATE_PAD), F32),
        ],
        scratch_shapes=[pltpu.VMEM((tm, d), BF16)],
        compiler_params=_cparams("parallel", "arbitrary"),
        name="norm_proj",
    )(x, gain.reshape(1, d), w, col_scale, w_gate)


def _out_proj_norm_kernel(*refs, n_lhs):
    a_refs = refs[:n_lhs]
    w_ref, r_ref, g_ref, h_ref, xn_ref = refs[n_lhs:n_lhs + 5]
    acc = r_ref[...]
    row = 0
    for a_ref in a_refs:
        width = a_ref.shape[1]
        acc = acc + _dot(a_ref[...], w_ref[row:row + width, :])
        row += width
    h_ref[...] = acc
    xn_ref[...] = _rms_norm_rows(acc, g_ref[...]).astype(xn_ref.dtype)


def out_proj_norm(lhs_list, w, res, gain, *, tm=OUT_TM):
    m, n = res.shape
    tm = min(tm, m)
    in_specs = [pl.BlockSpec((tm, a.shape[1]), lambda i: (i, 0)) for a in lhs_list]
    in_specs += [pl.BlockSpec(w.shape, lambda i: (0, 0)),
                 pl.BlockSpec((tm, n), lambda i: (i, 0)), pl.BlockSpec((1, n), lambda i: (0, 0))]
    return pl.pallas_call(
        functools.partial(_out_proj_norm_kernel, n_lhs=len(lhs_list)),
        grid=(m // tm,),
        in_specs=in_specs,
        out_specs=[pl.BlockSpec((tm, n), lambda i: (i, 0)), pl.BlockSpec((tm, n), lambda i: (i, 0))],
        out_shape=[jax.ShapeDtypeStruct((m, n), F32), jax.ShapeDtypeStruct((m, n), BF16)],
        compiler_params=_cparams("parallel"),
        name="out_proj_norm",
    )(*lhs_list, w, res, gain.reshape(1, n))


def _down_proj_norm_kernel(a_ref, w_ref, r_ref, g_ref, *out_refs, last):
    acc = r_ref[...] + _dot(a_ref[...], w_ref[...])
    normed = _rms_norm_rows(acc, g_ref[...])
    if last:
        out_refs[0][...] = normed
    else:
        out_refs[0][...] = acc
        out_refs[1][...] = normed.astype(out_refs[1].dtype)


def down_proj_norm(act, w_stack, layer, res, gain, *, last, tm=DOWN_TM):
    m, n = res.shape
    k = act.shape[1]
    tm = min(tm, m)
    rows = lambda width: pl.BlockSpec((tm, width), lambda i: (i, 0))
    out_shape = [jax.ShapeDtypeStruct((m, n), F32)]
    if not last:
        out_shape.append(jax.ShapeDtypeStruct((m, n), BF16))
    return pl.pallas_call(
        functools.partial(_down_proj_norm_kernel, last=last),
        grid=(m // tm,),
        in_specs=[rows(k),
                  pl.BlockSpec((None, k, n), lambda i: (layer, 0, 0),
                               pipeline_mode=pl.Buffered(1)),
                  rows(n),
                  pl.BlockSpec((1, n), lambda i: (0, 0))],
        out_specs=[rows(n)] * len(out_shape),
        out_shape=out_shape,
        compiler_params=_cparams("parallel"),
        name="down_proj_norm",
    )(act, w_stack, res, gain.reshape(1, n))


def _ffn_up_kernel(xn_ref, wg_ref, wu_ref, o_ref):
    for r in range(0, xn_ref.shape[0], FFN_SUB_ROWS):
        xn = xn_ref[r:r + FFN_SUB_ROWS, :]
        gate = _dot(xn, wg_ref[...])
        up = _dot(xn, wu_ref[...])
        o_ref[r:r + FFN_SUB_ROWS, :] = (gate * _sigmoid(gate) * up).astype(o_ref.dtype)


def ffn_up(xn, w_gate, w_up, layer, *, tm=FFN_TM, tn=FFN_TN):
    m, d = xn.shape
    n = w_gate.shape[2]
    tm = min(tm, m)
    return pl.pallas_call(
        _ffn_up_kernel,
        grid=(m // tm, n // tn),
        in_specs=[
            pl.BlockSpec((tm, d), lambda i, j: (i, 0)),
            pl.BlockSpec((None, d, tn), lambda i, j: (layer, 0, j)),
            pl.BlockSpec((None, d, tn), lambda i, j: (layer, 0, j)),
        ],
        out_specs=pl.BlockSpec((tm, tn), lambda i, j: (i, j)),
        out_shape=jax.ShapeDtypeStruct((m, n), BF16),
        compiler_params=_cparams("parallel", "arbitrary"),
        name="ffn_up",
    )(xn, w_gate, w_up)


def _cast_kernel(x_ref, o_ref):
    o_ref[...] = x_ref[...].astype(o_ref.dtype)


def cast_bf16(w, *, rows=CAST_ROWS):
    lead, r, c = w.shape
    flat = w.reshape(lead * r, c)
    out = pl.pallas_call(
        _cast_kernel,
        grid=(lead * r // rows,),
        in_specs=[pl.BlockSpec((rows, c), lambda i: (i, 0))],
        out_specs=pl.BlockSpec((rows, c), lambda i: (i, 0)),
        out_shape=jax.ShapeDtypeStruct(flat.shape, BF16),
        compiler_params=_cparams("parallel"),
        name="cast_bf16",
    )(flat)
    return out.reshape(lead, r, c)


def _swa_kernel(sinks_ref, q_ref, kc_ref, kp_ref, vc_ref, vp_ref, o_ref, bias_ref):
    blk, dh = SWA_BLOCK, HEAD_DIM
    nkv = SWA_KV_HEADS
    keys = 2 * blk

    @pl.when((pl.program_id(0) == 0) & (pl.program_id(1) == 0))
    def _():
        kj = lax.broadcasted_iota(jnp.int32, (keys, blk), 0)
        qi = lax.broadcasted_iota(jnp.int32, (keys, blk), 1) + blk
        dist = qi - kj
        in_window = (dist >= 0) & (dist < WINDOW)
        distf = dist.astype(F32)
        for g in range(nkv):
            for first, valid in ((0, in_window & (kj >= blk)), (1, in_window)):
                quad = [jnp.where(valid, -(2.0 ** (-8.0 * (4 * g + hh + 1) / SWA_Q_HEADS) * LOG2E)
                                  * distf, -jnp.inf) for hh in range(SWA_GROUP)]
                bias_ref[first, g] = jnp.concatenate(
                    [jnp.concatenate([quad[0], quad[2]], axis=1),
                     jnp.concatenate([quad[1], quad[3]], axis=1)], axis=0)

    which = jnp.where(pl.program_id(1) > 0, 1, 0)
    k = jnp.concatenate([kp_ref[...], kc_ref[...]], axis=0).astype(F32)
    v_t = jnp.concatenate([vp_ref[...], vc_ref[...]], axis=0).astype(F32).T
    lane = lax.broadcasted_iota(jnp.int32, (keys, LANES), 1)
    low = lane < dh

    def key_operand(g):
        kblk = k[:, (g // 2) * LANES:(g // 2 + 1) * LANES]
        swapped = pltpu.roll(kblk, dh, axis=1)
        k_low = jnp.where(low, kblk if g % 2 == 0 else swapped, 0.0)
        k_high = jnp.where(low, 0.0, swapped if g % 2 == 0 else kblk)
        return jnp.concatenate([k_low, k_high], axis=0).astype(BF16)

    def value_operand(g):
        vh_t = v_t[g * dh:(g + 1) * dh, :]
        zero = jnp.zeros_like(vh_t)
        return jnp.concatenate([jnp.concatenate([vh_t, zero], axis=1),
                                jnp.concatenate([zero, vh_t], axis=1)], axis=0).astype(BF16)

    scores = []
    for g in range(nkv):
        q2 = jnp.concatenate([q_ref[:, (2 * g) * LANES:(2 * g + 1) * LANES],
                              q_ref[:, (2 * g + 1) * LANES:(2 * g + 2) * LANES]], axis=0)
        scores.append(_dot_nt(key_operand(g), q2))
    probs, scales = [], []
    for g in range(nkv):
        s = scores[g] + bias_ref[which, g]
        halves, inv = [], []
        for half in range(2):
            sh = s[half * keys:(half + 1) * keys, :]
            sink = jnp.concatenate(
                [jnp.full((1, blk), sinks_ref[4 * g + 2 * pair + half] * LOG2E, F32)
                 for pair in range(2)], axis=1)
            m = jnp.maximum(jnp.max(sh, axis=0, keepdims=True), sink)
            p = jnp.exp2(sh - m)
            den = jnp.sum(p, axis=0, keepdims=True) + jnp.exp2(sink - m)
            halves.append(p.astype(BF16))
            inv.append(jnp.broadcast_to(1.0 / den, (dh, 2 * blk)))
        probs.append(jnp.concatenate(halves, axis=0))
        scales.append(jnp.concatenate(inv, axis=0))
    for g in range(nkv):
        o_t = _dot(value_operand(g), probs[g]) * scales[g]
        for pair in range(2):
            o_ref[:, (2 * g + pair) * LANES:(2 * g + pair + 1) * LANES] = (
                o_t[:, pair * blk:(pair + 1) * blk].T.astype(o_ref.dtype))


def swa_attention(proj, sinks, batch, seq):
    blk = SWA_BLOCK
    nb = seq // blk
    m = batch * seq
    qw = SWA_Q_HEADS * HEAD_DIM
    kw = SWA_KV_HEADS * HEAD_DIM
    cur = lambda b, n: b * nb + n
    prev = lambda b, n: b * nb + jnp.maximum(n - 1, 0)
    return pl.pallas_call(
        _swa_kernel,
        grid=(batch, nb),
        in_specs=[
            pl.BlockSpec(memory_space=pltpu.SMEM),
            pl.BlockSpec((blk, qw), lambda b, n: (cur(b, n), EVEN_QS // qw)),
            pl.BlockSpec((blk, kw), lambda b, n: (cur(b, n), EVEN_KS // kw)),
            pl.BlockSpec((blk, kw), lambda b, n: (prev(b, n), EVEN_KS // kw)),
            pl.BlockSpec((blk, kw), lambda b, n: (cur(b, n), EVEN_VS // kw)),
            pl.BlockSpec((blk, kw), lambda b, n: (prev(b, n), EVEN_VS // kw)),
        ],
        out_specs=pl.BlockSpec((blk, qw), lambda b, n: (cur(b, n), 0)),
        out_shape=jax.ShapeDtypeStruct((m, qw), BF16),
        scratch_shapes=[pltpu.VMEM((2, SWA_KV_HEADS, 4 * blk, 2 * blk), F32)],
        compiler_params=_cparams("arbitrary", "arbitrary"),
        name="swa_attention",
    )(sinks, proj, proj, proj, proj, proj)


def _mlstm_kernel(g_ref, bias_ref, gain_ref, q_ref, k_ref, v_ref, og_ref, o_ref,
                  c_ref, m_ref, qk_ref, qc_ref):
    L = MLSTM_CHUNK
    dk, dv = MLSTM_QK_DIM, MLSTM_V_DIM
    nh = MLSTM_HEADS

    @pl.when(pl.program_id(1) == 0)
    def _():
        c_ref[...] = jnp.zeros_like(c_ref)
        m_ref[...] = jnp.zeros_like(m_ref)

    a_all = GATE_SOFTCAP * jnp.tanh((g_ref[...] + bias_ref[...]) / GATE_SOFTCAP)
    lf_all = _log_sigmoid(a_all)
    row = lax.broadcasted_iota(jnp.int32, (L, L), 0)
    col = lax.broadcasted_iota(jnp.int32, (L, L), 1)
    causal = row >= col
    tri = jnp.where(causal, 1.0, 0.0).astype(BF16)
    hi, mid, lo = _split3_bf16(lf_all)
    b_all = _dot(tri, hi) + _dot(tri, mid) + _dot(tri, lo)
    a_all_t = a_all.T
    b_all_t = b_all.T
    ones = jnp.ones((L, LANES), BF16)

    for h in range(nh):
        q = q_ref[:, h * dk:(h + 1) * dk]
        qk_ref[h] = _dot_nt(q, k_ref[:, h * dk:(h + 1) * dk])
        qc_ref[h] = _dot(q, c_ref[h].astype(BF16))

    for h in range(nh):
        li_r = a_all_t[h:h + 1, :]
        b_c = b_all[:, nh + h:nh + h + 1]
        b_r = b_all_t[nh + h:nh + h + 1, :]
        b_last = b_r[:, L - 1:L]
        m_prev = m_ref[h:h + 1, 0:1]

        a_r = b_last - b_r + li_r
        a_max = jnp.max(a_r, axis=-1, keepdims=True)
        w_r = jnp.exp(a_r - a_max)

        log_d = jnp.where(causal, b_c - b_r + li_r, -jnp.inf)
        g_c = b_c + m_prev
        m_t = jnp.maximum(g_c, jnp.max(log_d, axis=-1, keepdims=True))
        d_mat = jnp.exp(log_d - m_t)
        inter = jnp.exp(g_c - m_t)

        k = k_ref[:, h * dk:(h + 1) * dk]
        v_aug = jnp.concatenate([v_ref[:, h * dv:(h + 1) * dv], ones], axis=1)
        p = (qk_ref[h] * d_mat).astype(BF16)
        c_prev = c_ref[h]
        num = _dot(p, v_aug) + inter * qc_ref[h]
        den = num[:, dv:dv + 1]
        hh = num[:, :dv] / jnp.maximum(jnp.abs(den), jnp.exp(-m_t))
        hn = _rms_norm_rows(hh, gain_ref[:, h * dv:(h + 1) * dv])
        gate = _sigmoid(og_ref[:, h * dv:(h + 1) * dv].astype(F32))
        o_ref[:, h * dv:(h + 1) * dv] = (gate * hn).astype(o_ref.dtype)

        m_new = jnp.maximum(b_last + m_prev, a_max)
        decay = jnp.exp(b_last + m_prev - m_new)
        scl = jnp.exp(a_max - m_new)
        kw_t = (k.astype(F32).T * w_r).astype(BF16)
        c_ref[h] = decay * c_prev + scl * _dot(kw_t, v_aug)
        m_ref[h:h + 1, :] = jnp.broadcast_to(m_new, (1, LANES))


def mlstm(proj, gates, bias_row, head_gain, batch, seq):
    L = MLSTM_CHUNK
    nc = seq // L
    m = batch * seq
    nh, dk, dv = MLSTM_HEADS, MLSTM_QK_DIM, MLSTM_V_DIM
    rows = lambda b, c: b * nc + c
    return pl.pallas_call(
        _mlstm_kernel,
        grid=(batch, nc),
        in_specs=[
            pl.BlockSpec((L, GATE_PAD), lambda b, c: (rows(b, c), 0)),
            pl.BlockSpec((1, GATE_PAD), lambda b, c: (0, 0)),
            pl.BlockSpec((1, nh * dv), lambda b, c: (0, 0)),
            pl.BlockSpec((L, nh * dk), lambda b, c: (rows(b, c), EVEN_QM // (nh * dk))),
            pl.BlockSpec((L, nh * dk), lambda b, c: (rows(b, c), EVEN_KM // (nh * dk))),
            pl.BlockSpec((L, nh * dv), lambda b, c: (rows(b, c), EVEN_VM // (nh * dv))),
            pl.BlockSpec((L, nh * dv), lambda b, c: (rows(b, c), EVEN_OM // (nh * dv))),
        ],
        out_specs=pl.BlockSpec((L, nh * dv), lambda b, c: (rows(b, c), 0)),
        out_shape=jax.ShapeDtypeStruct((m, nh * dv), BF16),
        scratch_shapes=[pltpu.VMEM((nh, dk, dv + LANES), F32),
                        pltpu.VMEM((SUBLANES, LANES), F32),
                        pltpu.VMEM((nh, L, L), F32),
                        pltpu.VMEM((nh, L, dv + LANES), F32)],
        compiler_params=_cparams("parallel", "arbitrary"),
        name="mlstm",
    )(gates, bias_row, head_gain.reshape(1, nh * dv), proj, proj, proj, proj)


BIAS_COLS = 3


def _fox_selectors():
    width = FOX_HEADS * LANES
    sel_k = np.zeros((GATE_PAD, width), np.float32)
    sel_q = np.zeros((GATE_PAD, width), np.float32)
    const_k = np.zeros((1, width), np.float32)
    const_q = np.zeros((1, width), np.float32)
    for h in range(FOX_HEADS):
        for part in range(BIAS_COLS):
            sel_k[part * FOX_HEADS + h, h * LANES + part] = 1.0
            sel_q[part * FOX_HEADS + h, h * LANES + BIAS_COLS + part] = 1.0
            const_k[0, h * LANES + BIAS_COLS + part] = 1.0
            const_q[0, h * LANES + part] = -1.0
    return (jnp.asarray(sel_k, BF16), jnp.asarray(sel_q, BF16),
            jnp.asarray(const_k), jnp.asarray(const_q))


FOX_BLOCK = 512
STAT_ROWS = SUBLANES


def _head_norms_sq(x_ref, gsum):
    return jnp.max(_dot(jnp.square(x_ref[...]), gsum), axis=0, keepdims=True)


def _fox_bias_kernel(g_ref, bias_ref, selk_ref, selq_ref, ck_ref, cq_ref, gsum_ref, q_ref, k_ref,
                     kb_ref, qb_ref, stat_ref, carry_ref, *, ts):
    @pl.when(pl.program_id(1) == 0)
    def _():
        carry_ref[...] = jnp.zeros_like(carry_ref)

    lf = _log_sigmoid(g_ref[...] + bias_ref[...])
    row = lax.broadcasted_iota(jnp.int32, (ts, ts), 0)
    col = lax.broadcasted_iota(jnp.int32, (ts, ts), 1)
    tri = jnp.where(row >= col, 1.0, 0.0).astype(BF16)
    hi, mid, lo = _split3_bf16(lf)
    c = _dot(tri, hi) + _dot(tri, mid) + _dot(tri, lo) + carry_ref[0:1, :]
    carry_ref[...] = jnp.broadcast_to(c[ts - 1:ts, :], carry_ref.shape)
    c2 = c * LOG2E
    p_hi, p_mid, p_lo = (t.astype(F32) for t in _split3_bf16(c2))
    lane = lax.broadcasted_iota(jnp.int32, (ts, GATE_PAD), 1)
    parts = jnp.where(lane < FOX_HEADS, p_hi,
                      jnp.where(lane < 2 * FOX_HEADS, pltpu.roll(p_mid, FOX_HEADS, axis=1),
                                pltpu.roll(p_lo, 2 * FOX_HEADS, axis=1))).astype(BF16)
    kb_ref[...] = (_dot(parts, selk_ref[...]) + ck_ref[...]).astype(BF16)
    qb_ref[...] = (_dot(parts, selq_ref[...]) + cq_ref[...]).astype(BF16)
    gsum = gsum_ref[...]
    stat_ref[0] = jnp.concatenate(
        [_head_norms_sq(q_ref, gsum), _head_norms_sq(k_ref, gsum), c2[0:1, :], c2[ts - 1:ts, :],
         jnp.zeros((STAT_ROWS - 4, GATE_PAD), F32)], axis=0)


def fox_bias_columns(gates, bias_row, proj, batch, seq):
    ts = FOX_BLOCK
    ns = seq // ts
    m = batch * seq
    width = FOX_HEADS * LANES
    sel_k, sel_q, const_k, const_q = _fox_selectors()
    gsum = np.zeros((width, GATE_PAD), np.float32)
    for h in range(FOX_HEADS):
        gsum[h * FOX_HEAD_DIM:(h + 1) * FOX_HEAD_DIM, h] = 1.0
    gsum = jnp.asarray(gsum, BF16)
    whole = lambda shape: pl.BlockSpec(shape, lambda b, s: (0, 0))
    rows = lambda b, s: b * ns + s
    return pl.pallas_call(
        functools.partial(_fox_bias_kernel, ts=ts),
        grid=(batch, ns),
        in_specs=[pl.BlockSpec((ts, GATE_PAD), lambda b, s: (rows(b, s), 0)),
                  whole((1, GATE_PAD)),
                  whole(sel_k.shape), whole(sel_q.shape),
                  whole((1, width)), whole((1, width)), whole(gsum.shape),
                  pl.BlockSpec((ts, width), lambda b, s: (rows(b, s), 0)),
                  pl.BlockSpec((ts, width), lambda b, s: (rows(b, s), 1))],
        out_specs=[pl.BlockSpec((ts, width), lambda b, s: (rows(b, s), 0)),
                   pl.BlockSpec((ts, width), lambda b, s: (rows(b, s), 0)),
                   pl.BlockSpec((1, STAT_ROWS, GATE_PAD), lambda b, s: (rows(b, s), 0, 0))],
        out_shape=[jax.ShapeDtypeStruct((m, width), BF16),
                   jax.ShapeDtypeStruct((m, width), BF16),
                   jax.ShapeDtypeStruct((batch * ns, STAT_ROWS, GATE_PAD), F32)],
        scratch_shapes=[pltpu.VMEM((SUBLANES, GATE_PAD), F32)],
        compiler_params=_cparams("parallel", "arbitrary"),
        name="fox_bias_columns",
    )(gates, bias_row, sel_k, sel_q, const_k, const_q, gsum, proj, proj)


FOX_SKIP_BITS = 170.0
FOX_NORM_MARGIN = 1.02


def fox_first_tiles(stats, batch, seq):
    nq = seq // FOX_BLOCK
    st = stats.reshape(batch, nq, STAT_ROWS, GATE_PAD)[..., :FOX_HEADS]
    qn = jnp.sqrt(st[:, :, 0, :]) * FOX_NORM_MARGIN
    kn = jnp.sqrt(st[:, :, 1, :])
    c_first, c_last = st[:, :, 2, :], st[:, :, 3, :]
    upper = (qn[:, :, None, :] * kn[:, None, :, :]
             + c_first[:, :, None, :] - c_last[:, None, :, :])
    row_max_lower = -(qn * kn)[:, :, None, :]
    blocks = jnp.arange(nq)
    on_or_after_diagonal = blocks[None, None, :, None] >= blocks[None, :, None, None]
    needed = (upper - row_max_lower >= -FOX_SKIP_BITS) | on_or_after_diagonal
    first = jnp.argmax(needed, axis=2).astype(jnp.int32)
    return jnp.transpose(first, (0, 2, 1)).reshape(-1)


FOX_GROUP = 8


def _fox_attn_kernel(first_ref, q_ref, qb_ref, k_ref, kb_ref, v_ref, o_ref,
                     vt_ref, m_ref, l_ref, acc_ref, s_ref, *, blk, nblk):
    for c in range(nblk):
        vt_ref[c] = v_ref[c * blk:(c + 1) * blk, :].T
    table_base = (pl.program_id(0) * pl.num_programs(1) + pl.program_id(1)) * nblk

    def query_block(qi, carry):
        _fox_query_block(qi, first_ref[table_base + qi], q_ref, qb_ref, k_ref, kb_ref, o_ref,
                         vt_ref, m_ref, l_ref, acc_ref, s_ref, blk)
        return carry

    lax.fori_loop(0, nblk, query_block, 0)


def _fox_query_block(qi, first, q_ref, qb_ref, k_ref, kb_ref, o_ref,
                     vt_ref, m_ref, l_ref, acc_ref, s_ref, blk):
    group = FOX_GROUP
    q_rows = pl.ds(pl.multiple_of(qi * blk, blk), blk)

    q_aug_t = jnp.concatenate([q_ref[q_rows, :].T, qb_ref[q_rows, :].T], axis=0)
    m_ref[...] = jnp.full_like(m_ref, -jnp.inf)
    l_ref[...] = jnp.zeros_like(l_ref)
    acc_ref[...] = jnp.zeros_like(acc_ref)

    def tiles(kbs, masks):
        n = len(kbs)
        start = pl.multiple_of(kbs[0] * blk, blk)
        k_aug = jnp.concatenate([k_ref[pl.ds(start, n * blk), :], kb_ref[pl.ds(start, n * blk), :]],
                                axis=1)
        s_all = _dot(k_aug, q_aug_t)
        tile_max = []
        for t, masked in enumerate(masks):
            s = s_all[t * blk:(t + 1) * blk, :]
            if masked:
                key = lax.broadcasted_iota(jnp.int32, (blk, blk), 0)
                qry = lax.broadcasted_iota(jnp.int32, (blk, blk), 1)
                s = jnp.where(key <= qry, s, -jnp.inf)
            s_ref[t] = s
            tile_max.append(jnp.max(s, axis=0, keepdims=True))
        m_run = m_ref[...]
        l_run = l_ref[...]
        acc = acc_ref[...]
        for t, kb in enumerate(kbs):
            m_new = jnp.maximum(m_run, tile_max[t])
            alpha = jnp.exp2(m_run - m_new)
            p = jnp.exp2(s_ref[t] - m_new)
            l_run = alpha * l_run + jnp.sum(p, axis=0, keepdims=True)
            acc = alpha * acc + _dot(vt_ref[kb], p.astype(BF16))
            m_run = m_new
        acc_ref[...] = acc
        l_ref[...] = l_run
        m_ref[...] = m_run

    n_full = qi - first

    def body(j, carry):
        tiles([first + group * j + t for t in range(group)], [False] * group)
        return carry

    lax.fori_loop(0, n_full // group, body, 0)

    for r in range(group):
        @pl.when(n_full % group == r)
        def _(r=r):
            tiles([qi - r + t for t in range(r + 1)], [False] * r + [True])

    o_ref[q_rows, :] = (acc_ref[...] / l_ref[...]).T.astype(o_ref.dtype)


def fox_attention(proj, key_bias, query_bias, first_tiles, batch, seq):
    blk = FOX_BLOCK
    nq = seq // blk
    m = batch * seq
    nh, dh = FOX_HEADS, FOX_HEAD_DIM
    return pl.pallas_call(
        functools.partial(_fox_attn_kernel, blk=blk, nblk=nq),
        grid=(batch, nh),
        in_specs=[
            pl.BlockSpec(memory_space=pltpu.SMEM),
            pl.BlockSpec((seq, dh), lambda b, h: (b, h)),
            pl.BlockSpec((seq, LANES), lambda b, h: (b, h)),
            pl.BlockSpec((seq, dh), lambda b, h: (b, nh + h)),
            pl.BlockSpec((seq, LANES), lambda b, h: (b, h)),
            pl.BlockSpec((seq, dh), lambda b, h: (b, 2 * nh + h)),
        ],
        out_specs=pl.BlockSpec((seq, dh), lambda b, h: (b, h)),
        out_shape=jax.ShapeDtypeStruct((m, nh * dh), BF16),
        scratch_shapes=[pltpu.VMEM((nq, dh, blk), BF16),
                        pltpu.VMEM((1, blk), F32), pltpu.VMEM((1, blk), F32),
                        pltpu.VMEM((dh, blk), F32),
                        pltpu.VMEM((FOX_GROUP, blk, blk), F32)],
        compiler_params=_cparams("parallel", "parallel"),
        name="fox_attention",
    )(first_tiles, proj, query_bias, proj, key_bias, proj)


def _pad_cols(t, width):
    return jnp.pad(t, ((0, 0), (0, width - t.shape[1])))


def _even_col_scale():
    cs = jnp.ones((EVEN_MAIN,), F32)
    cs = cs.at[EVEN_QS:EVEN_QS + SWA_Q_HEADS * HEAD_DIM].set(HEAD_DIM ** -0.5 * LOG2E)
    cs = cs.at[EVEN_QM:EVEN_QM + MLSTM_HEADS * MLSTM_QK_DIM].set(MLSTM_QK_DIM ** -0.5)
    return cs.reshape(1, EVEN_MAIN)


def _fox_col_scale():
    cs = jnp.ones((FOX_MAIN,), F32)
    cs = cs.at[:FOX_HEADS * FOX_HEAD_DIM].set(FOX_HEAD_DIM ** -0.5 * LOG2E)
    return cs.reshape(1, FOX_MAIN)


def _even_mixer(h, x_in, gain, w_in, w_out, sinks, b_i, b_f, head_gain, ffn_gain, batch, seq):
    q_s, k_s, v_s, q_m, k_m, v_m, o_m, gate_cols = jnp.split(
        w_in, [1024, 1280, 1536, 2048, 2560, 3584, 4608], axis=1)
    w_main = jnp.concatenate([q_s, v_m, o_m, q_m, k_m, k_s, v_s], axis=1).astype(BF16)
    w_gate = _pad_cols(gate_cols, GATE_PAD).astype(BF16)
    proj, gates = norm_proj(x_in, gain, w_main, _even_col_scale(), w_gate)
    attn = swa_attention(proj, sinks, batch, seq)
    bias_row = _pad_cols(jnp.concatenate([b_i, b_f]).reshape(1, -1), GATE_PAD)
    mem = mlstm(proj, gates, bias_row, head_gain, batch, seq)
    return out_proj_norm([attn, mem], w_out.astype(BF16), h, ffn_gain)


def _fox_mixer(h, x_in, gain, w_in, b_f, w_out, ffn_gain, batch, seq):
    w_main = w_in[:, :FOX_MAIN].astype(BF16)
    w_gate = _pad_cols(w_in[:, FOX_MAIN:], GATE_PAD).astype(BF16)
    proj, gates = norm_proj(x_in, gain, w_main, _fox_col_scale(), w_gate)
    key_bias, query_bias, stats = fox_bias_columns(
        gates, _pad_cols(b_f.reshape(1, -1), GATE_PAD), proj, batch, seq)
    first_tiles = fox_first_tiles(stats, batch, seq)
    attn = fox_attention(proj, key_bias, query_bias, first_tiles, batch, seq)
    return out_proj_norm([attn], w_out.astype(BF16), h, ffn_gain)


def kernel(x, norm_mix, norm_ffn, norm_final, w_in_even, w_out_even, swa_sinks, mlstm_b_i, mlstm_b_f, mlstm_head_gain, w_in_odd, fox_b_f, w_out_odd, w_gate, w_up, w_down):
    batch, seq, d = x.shape
    depth = norm_mix.shape[0]
    h = x.reshape(batch * seq, d)
    x_in = h
    w_gate, w_up, w_down = cast_bf16(w_gate), cast_bf16(w_up), cast_bf16(w_down)
    for layer in range(depth):
        j = layer // 2
        if layer % 2 == 0:
            h, xn = _even_mixer(h, x_in, norm_mix[layer], w_in_even[j], w_out_even[j],
                                swa_sinks[j], mlstm_b_i[j], mlstm_b_f[j], mlstm_head_gain[j],
                                norm_ffn[layer], batch, seq)
        else:
            h, xn = _fox_mixer(h, x_in, norm_mix[layer], w_in_odd[j], fox_b_f[j], w_out_odd[j],
                               norm_ffn[layer], batch, seq)
        act = ffn_up(xn, w_gate, w_up, layer)
        if layer + 1 < depth:
            h, x_in = down_proj_norm(act, w_down, layer, h, norm_mix[layer + 1], last=False)
        else:
            (out,) = down_proj_norm(act, w_down, layer, h, norm_final, last=True)
    return out.reshape(batch, seq, d)
```

```python
import functools

import jax
import jax.numpy as jnp
import numpy as np
from jax import lax
from jax.experimental import pallas as pl
from jax.experimental.pallas import tpu as pltpu

F32 = jnp.float32
BF16 = jnp.bfloat16

RMS_EPS = 1e-5
LOG2E = 1.4426950408889634

HEAD_DIM = 64
SWA_Q_HEADS = 16
SWA_KV_HEADS = 4
SWA_GROUP = SWA_Q_HEADS // SWA_KV_HEADS
WINDOW = 128
SWA_BLOCK = 128

MLSTM_HEADS = 4
MLSTM_V_DIM = 256
MLSTM_QK_DIM = 128
MLSTM_CHUNK = 256
GATE_SOFTCAP = 15.0

FOX_HEAD_DIM = 128
FOX_HEADS = 16

EVEN_MAIN = 4608
EVEN_QS, EVEN_VM, EVEN_OM = 0, 1024, 2048
EVEN_QM, EVEN_KM, EVEN_KS, EVEN_VS = 3072, 3584, 4096, 4352
FOX_MAIN = 3 * FOX_HEADS * FOX_HEAD_DIM

LANES = 128
SUBLANES = 8
GATE_PAD = LANES
VMEM_LIMIT = 56 * 1024 * 1024

PROJ_TM, PROJ_TN = 1024, 1536
FFN_TM, FFN_TN = 2048, 512
FFN_SUB_ROWS = 1024
OUT_TM = 512
DOWN_TM = 256
CAST_ROWS = 512


def _cparams(*semantics):
    return pltpu.CompilerParams(dimension_semantics=semantics, vmem_limit_bytes=VMEM_LIMIT)


def _rms_norm_rows(x, gain):
    ms = jnp.mean(x * x, axis=-1, keepdims=True)
    return x * lax.rsqrt(ms + RMS_EPS) * gain


def _log_sigmoid(x):
    return jnp.minimum(x, 0.0) - jnp.log(1.0 + jnp.exp(-jnp.abs(x)))


def _sigmoid(x):
    return 1.0 / (1.0 + jnp.exp(-x))


def _split3_bf16(x):
    hi = x.astype(BF16)
    r1 = x - hi.astype(F32)
    mid = r1.astype(BF16)
    lo = (r1 - mid.astype(F32)).astype(BF16)
    return hi, mid, lo


def _dot(a, b):
    return jnp.dot(a, b, preferred_element_type=F32)


def _dot_nt(a, b):
    return lax.dot_general(a, b, (((1,), (1,)), ((), ())), preferred_element_type=F32)


def _norm_proj_kernel(x_ref, g_ref, w_ref, cs_ref, wg_ref, o_ref, og_ref, xn_ref, *, normed):
    lhs_ref = x_ref if normed else xn_ref

    @pl.when(pl.program_id(1) == 0)
    def _():
        if not normed:
            xn_ref[...] = _rms_norm_rows(x_ref[...], g_ref[...]).astype(BF16)
        og_ref[...] = _dot(lhs_ref[...], wg_ref[...])

    for r in range(0, o_ref.shape[0], PROJ_TM):
        rows = slice(r, r + PROJ_TM)
        o_ref[rows, :] = (_dot(lhs_ref[rows, :], w_ref[...]) * cs_ref[...]).astype(o_ref.dtype)


def norm_proj(x, gain, w, col_scale, w_gate, *, tn=PROJ_TN):
    m, d = x.shape
    n = w.shape[1]
    normed = x.dtype == BF16
    tm = min(2 * PROJ_TM if normed else PROJ_TM, m)
    return pl.pallas_call(
        functools.partial(_norm_proj_kernel, normed=normed),
        grid=(m // tm, n // tn),
        in_specs=[
            pl.BlockSpec((tm, d), lambda i, j: (i, 0)),
            pl.BlockSpec((1, d), lambda i, j: (0, 0)),
            pl.BlockSpec((d, tn), lambda i, j: (0, j)),
            pl.BlockSpec((1, tn), lambda i, j: (0, j)),
            pl.BlockSpec((d, GATE_PAD), lambda i, j: (0, 0)),
        ],
        out_specs=[
            pl.BlockSpec((tm, tn), lambda i, j: (i, j)),
            pl.BlockSpec((tm, GATE_PAD), lambda i, j: (i, 0)),
        ],
        out_shape=[
            jax.ShapeDtypeStruct((m, n), BF16),
            jax.ShapeDtypeStruct((m, GATE_PAD), F32),
        ],
        scratch_shapes=[pltpu.VMEM((SUBLANES, LANES) if normed else (tm, d), BF16)],
        compiler_params=_cparams("parallel", "arbitrary"),
        name="norm_proj",
    )(x, gain.reshape(1, d), w, col_scale, w_gate)


def _out_proj_norm_kernel(*refs, n_lhs):
    a_refs = refs[:n_lhs]
    w_ref, r_ref, g_ref, h_ref, xn_ref = refs[n_lhs:n_lhs + 5]
    acc = r_ref[...]
    row = 0
    for a_ref in a_refs:
        width = a_ref.shape[1]
        acc = acc + _dot(a_ref[...], w_ref[row:row + width, :])
        row += width
    h_ref[...] = acc
    xn_ref[...] = _rms_norm_rows(acc, g_ref[...]).astype(xn_ref.dtype)


def out_proj_norm(lhs_list, w, res, gain, *, tm=OUT_TM):
    m, n = res.shape
    tm = min(tm, m)
    in_specs = [pl.BlockSpec((tm, a.shape[1]), lambda i: (i, 0)) for a in lhs_list]
    in_specs += [pl.BlockSpec(w.shape, lambda i: (0, 0)),
                 pl.BlockSpec((tm, n), lambda i: (i, 0)), pl.BlockSpec((1, n), lambda i: (0, 0))]
    return pl.pallas_call(
        functools.partial(_out_proj_norm_kernel, n_lhs=len(lhs_list)),
        grid=(m // tm,),
        in_specs=in_specs,
        out_specs=[pl.BlockSpec((tm, n), lambda i: (i, 0)), pl.BlockSpec((tm, n), lambda i: (i, 0))],
        out_shape=[jax.ShapeDtypeStruct((m, n), F32), jax.ShapeDtypeStruct((m, n), BF16)],
        compiler_params=_cparams("parallel"),
        name="out_proj_norm",
    )(*lhs_list, w, res, gain.reshape(1, n))


def _down_proj_norm_kernel(a_ref, w_ref, r_ref, g_ref, *out_refs, last):
    acc = r_ref[...] + _dot(a_ref[...], w_ref[...])
    normed = _rms_norm_rows(acc, g_ref[...])
    if last:
        out_refs[0][...] = normed
    else:
        out_refs[0][...] = acc
        out_refs[1][...] = normed.astype(out_refs[1].dtype)


def down_proj_norm(act, w_stack, layer, res, gain, *, last, tm=DOWN_TM):
    m, n = res.shape
    k = act.shape[1]
    tm = min(tm, m)
    rows = lambda width: pl.BlockSpec((tm, width), lambda i: (i, 0))
    out_shape = [jax.ShapeDtypeStruct((m, n), F32)]
    if not last:
        out_shape.append(jax.ShapeDtypeStruct((m, n), BF16))
    return pl.pallas_call(
        functools.partial(_down_proj_norm_kernel, last=last),
        grid=(m // tm,),
        in_specs=[rows(k),
                  pl.BlockSpec((None, k, n), lambda i: (layer, 0, 0),
                               pipeline_mode=pl.Buffered(1)),
                  rows(n),
                  pl.BlockSpec((1, n), lambda i: (0, 0))],
        out_specs=[rows(n)] * len(out_shape),
        out_shape=out_shape,
        compiler_params=_cparams("parallel"),
        name="down_proj_norm",
    )(act, w_stack, res, gain.reshape(1, n))


def _ffn_up_kernel(xn_ref, wg_ref, wu_ref, o_ref):
    for r in range(0, xn_ref.shape[0], FFN_SUB_ROWS):
        xn = xn_ref[r:r + FFN_SUB_ROWS, :]
        gate = _dot(xn, wg_ref[...])
        up = _dot(xn, wu_ref[...])
        o_ref[r:r + FFN_SUB_ROWS, :] = (gate * _sigmoid(gate) * up).astype(o_ref.dtype)


def ffn_up(xn, w_gate, w_up, layer, *, tm=FFN_TM, tn=FFN_TN):
    m, d = xn.shape
    depth, _, n = w_gate.shape
    tm = min(tm, m)
    weight_tile = pl.BlockSpec((d, tn), lambda i, j: (layer, j), pipeline_mode=pl.Buffered(3))

    def stream(xn_hbm, wg_hbm, wu_hbm, o_hbm):
        pltpu.emit_pipeline(
            _ffn_up_kernel,
            grid=(m // tm, n // tn),
            in_specs=[pl.BlockSpec((tm, d), lambda i, j: (i, 0)), weight_tile, weight_tile],
            out_specs=[pl.BlockSpec((tm, tn), lambda i, j: (i, j))],
        )(xn_hbm, wg_hbm, wu_hbm, o_hbm)

    anywhere = pl.BlockSpec(memory_space=pl.ANY)
    return pl.pallas_call(
        stream,
        in_specs=[anywhere, anywhere, anywhere],
        out_specs=anywhere,
        out_shape=jax.ShapeDtypeStruct((m, n), BF16),
        compiler_params=pltpu.CompilerParams(vmem_limit_bytes=VMEM_LIMIT),
        name="ffn_up",
    )(xn, w_gate.reshape(depth * d, n), w_up.reshape(depth * d, n))


def _cast_kernel(x_ref, o_ref):
    o_ref[...] = x_ref[...].astype(o_ref.dtype)


def cast_bf16(w, *, rows=CAST_ROWS):
    lead, r, c = w.shape
    flat = w.reshape(lead * r, c)
    out = pl.pallas_call(
        _cast_kernel,
        grid=(lead * r // rows,),
        in_specs=[pl.BlockSpec((rows, c), lambda i: (i, 0))],
        out_specs=pl.BlockSpec((rows, c), lambda i: (i, 0)),
        out_shape=jax.ShapeDtypeStruct(flat.shape, BF16),
        compiler_params=_cparams("parallel"),
        name="cast_bf16",
    )(flat)
    return out.reshape(lead, r, c)


def _swa_kernel(sinks_ref, q_ref, kc_ref, kp_ref, vc_ref, vp_ref, o_ref, bias_ref):
    blk, dh = SWA_BLOCK, HEAD_DIM
    nkv = SWA_KV_HEADS
    keys = 2 * blk

    @pl.when((pl.program_id(0) == 0) & (pl.program_id(1) == 0))
    def _():
        kj = lax.broadcasted_iota(jnp.int32, (keys, blk), 0)
        qi = lax.broadcasted_iota(jnp.int32, (keys, blk), 1) + blk
        dist = qi - kj
        in_window = (dist >= 0) & (dist < WINDOW)
        distf = dist.astype(F32)
        for g in range(nkv):
            for first, valid in ((0, in_window & (kj >= blk)), (1, in_window)):
                quad = [jnp.where(valid, -(2.0 ** (-8.0 * (4 * g + hh + 1) / SWA_Q_HEADS) * LOG2E)
                                  * distf, -jnp.inf) for hh in range(SWA_GROUP)]
                bias_ref[first, g] = jnp.concatenate(
                    [jnp.concatenate([quad[0], quad[2]], axis=1),
                     jnp.concatenate([quad[1], quad[3]], axis=1)], axis=0)

    which = jnp.where(pl.program_id(1) > 0, 1, 0)
    k = jnp.concatenate([kp_ref[...], kc_ref[...]], axis=0).astype(F32)
    v_t = jnp.concatenate([vp_ref[...], vc_ref[...]], axis=0).astype(F32).T
    lane = lax.broadcasted_iota(jnp.int32, (keys, LANES), 1)
    low = lane < dh

    def key_operand(g):
        kblk = k[:, (g // 2) * LANES:(g // 2 + 1) * LANES]
        swapped = pltpu.roll(kblk, dh, axis=1)
        k_low = jnp.where(low, kblk if g % 2 == 0 else swapped, 0.0)
        k_high = jnp.where(low, 0.0, swapped if g % 2 == 0 else kblk)
        return jnp.concatenate([k_low, k_high], axis=0).astype(BF16)

    def value_operand(g):
        vh_t = v_t[g * dh:(g + 1) * dh, :]
        zero = jnp.zeros_like(vh_t)
        return jnp.concatenate([jnp.concatenate([vh_t, zero], axis=1),
                                jnp.concatenate([zero, vh_t], axis=1)], axis=0).astype(BF16)

    scores = []
    for g in range(nkv):
        q2 = jnp.concatenate([q_ref[:, (2 * g) * LANES:(2 * g + 1) * LANES],
                              q_ref[:, (2 * g + 1) * LANES:(2 * g + 2) * LANES]], axis=0)
        scores.append(_dot_nt(key_operand(g), q2))
    probs, scales = [], []
    for g in range(nkv):
        s = scores[g] + bias_ref[which, g]
        halves, inv = [], []
        for half in range(2):
            sh = s[half * keys:(half + 1) * keys, :]
            sink = jnp.concatenate(
                [jnp.full((1, blk), sinks_ref[4 * g + 2 * pair + half] * LOG2E, F32)
                 for pair in range(2)], axis=1)
            m = jnp.maximum(jnp.max(sh, axis=0, keepdims=True), sink)
            p = jnp.exp2(sh - m)
            den = jnp.sum(p, axis=0, keepdims=True) + jnp.exp2(sink - m)
            halves.append(p.astype(BF16))
            inv.append(jnp.broadcast_to(1.0 / den, (dh, 2 * blk)))
        probs.append(jnp.concatenate(halves, axis=0))
        scales.append(jnp.concatenate(inv, axis=0))
    for g in range(nkv):
        o_t = _dot(value_operand(g), probs[g]) * scales[g]
        for pair in range(2):
            o_ref[:, (2 * g + pair) * LANES:(2 * g + pair + 1) * LANES] = (
                o_t[:, pair * blk:(pair + 1) * blk].T.astype(o_ref.dtype))


def swa_attention(proj, sinks, batch, seq):
    blk = SWA_BLOCK
    nb = seq // blk
    m = batch * seq
    qw = SWA_Q_HEADS * HEAD_DIM
    kw = SWA_KV_HEADS * HEAD_DIM
    cur = lambda b, n: b * nb + n
    prev = lambda b, n: b * nb + jnp.maximum(n - 1, 0)
    return pl.pallas_call(
        _swa_kernel,
        grid=(batch, nb),
        in_specs=[
            pl.BlockSpec(memory_space=pltpu.SMEM),
            pl.BlockSpec((blk, qw), lambda b, n: (cur(b, n), EVEN_QS // qw)),
            pl.BlockSpec((blk, kw), lambda b, n: (cur(b, n), EVEN_KS // kw)),
            pl.BlockSpec((blk, kw), lambda b, n: (prev(b, n), EVEN_KS // kw)),
            pl.BlockSpec((blk, kw), lambda b, n: (cur(b, n), EVEN_VS // kw)),
            pl.BlockSpec((blk, kw), lambda b, n: (prev(b, n), EVEN_VS // kw)),
        ],
        out_specs=pl.BlockSpec((blk, qw), lambda b, n: (cur(b, n), 0)),
        out_shape=jax.ShapeDtypeStruct((m, qw), BF16),
        scratch_shapes=[pltpu.VMEM((2, SWA_KV_HEADS, 4 * blk, 2 * blk), F32)],
        compiler_params=_cparams("arbitrary", "arbitrary"),
        name="swa_attention",
    )(sinks, proj, proj, proj, proj, proj)


def _mlstm_kernel(g_ref, bias_ref, gain_ref, q_ref, k_ref, v_ref, og_ref, o_ref,
                  c_ref, m_ref, qk_ref, qc_ref):
    L = MLSTM_CHUNK
    dk, dv = MLSTM_QK_DIM, MLSTM_V_DIM
    nh = MLSTM_HEADS

    @pl.when(pl.program_id(1) == 0)
    def _():
        c_ref[...] = jnp.zeros_like(c_ref)
        m_ref[...] = jnp.zeros_like(m_ref)

    a_all = GATE_SOFTCAP * jnp.tanh((g_ref[...] + bias_ref[...]) / GATE_SOFTCAP)
    lf_all = _log_sigmoid(a_all)
    row = lax.broadcasted_iota(jnp.int32, (L, L), 0)
    col = lax.broadcasted_iota(jnp.int32, (L, L), 1)
    causal = row >= col
    tri = jnp.where(causal, 1.0, 0.0).astype(BF16)
    hi, mid, lo = _split3_bf16(lf_all)
    b_all = _dot(tri, hi) + _dot(tri, mid) + _dot(tri, lo)
    a_all_t = a_all.T
    b_all_t = b_all.T
    ones = jnp.ones((L, LANES), BF16)

    for h in range(nh):
        q = q_ref[:, h * dk:(h + 1) * dk]
        qk_ref[h] = _dot_nt(q, k_ref[:, h * dk:(h + 1) * dk])
        qc_ref[h] = _dot(q, c_ref[h].astype(BF16))

    for h in range(nh):
        li_r = a_all_t[h:h + 1, :]
        b_c = b_all[:, nh + h:nh + h + 1]
        b_r = b_all_t[nh + h:nh + h + 1, :]
        b_last = b_r[:, L - 1:L]
        m_prev = m_ref[h:h + 1, 0:1]

        a_r = b_last - b_r + li_r
        a_max = jnp.max(a_r, axis=-1, keepdims=True)
        w_r = jnp.exp(a_r - a_max)

        log_d = jnp.where(causal, b_c - b_r + li_r, -jnp.inf)
        g_c = b_c + m_prev
        m_t = jnp.maximum(g_c, jnp.max(log_d, axis=-1, keepdims=True))
        d_mat = jnp.exp(log_d - m_t)
        inter = jnp.exp(g_c - m_t)

        k = k_ref[:, h * dk:(h + 1) * dk]
        v_aug = jnp.concatenate([v_ref[:, h * dv:(h + 1) * dv], ones], axis=1)
        p = (qk_ref[h] * d_mat).astype(BF16)
        c_prev = c_ref[h]
        num = _dot(p, v_aug) + inter * qc_ref[h]
        den = num[:, dv:dv + 1]
        hh = num[:, :dv] / jnp.maximum(jnp.abs(den), jnp.exp(-m_t))
        hn = _rms_norm_rows(hh, gain_ref[:, h * dv:(h + 1) * dv])
        gate = _sigmoid(og_ref[:, h * dv:(h + 1) * dv].astype(F32))
        o_ref[:, h * dv:(h + 1) * dv] = (gate * hn).astype(o_ref.dtype)

        m_new = jnp.maximum(b_last + m_prev, a_max)
        decay = jnp.exp(b_last + m_prev - m_new)
        scl = jnp.exp(a_max - m_new)
        kw_t = (k.astype(F32).T * w_r).astype(BF16)
        c_ref[h] = decay * c_prev + scl * _dot(kw_t, v_aug)
        m_ref[h:h + 1, :] = jnp.broadcast_to(m_new, (1, LANES))


def mlstm(proj, gates, bias_row, head_gain, batch, seq):
    L = MLSTM_CHUNK
    nc = seq // L
    m = batch * seq
    nh, dk, dv = MLSTM_HEADS, MLSTM_QK_DIM, MLSTM_V_DIM
    rows = lambda b, c: b * nc + c
    return pl.pallas_call(
        _mlstm_kernel,
        grid=(batch, nc),
        in_specs=[
            pl.BlockSpec((L, GATE_PAD), lambda b, c: (rows(b, c), 0)),
            pl.BlockSpec((1, GATE_PAD), lambda b, c: (0, 0)),
            pl.BlockSpec((1, nh * dv), lambda b, c: (0, 0)),
            pl.BlockSpec((L, nh * dk), lambda b, c: (rows(b, c), EVEN_QM // (nh * dk))),
            pl.BlockSpec((L, nh * dk), lambda b, c: (rows(b, c), EVEN_KM // (nh * dk))),
            pl.BlockSpec((L, nh * dv), lambda b, c: (rows(b, c), EVEN_VM // (nh * dv))),
            pl.BlockSpec((L, nh * dv), lambda b, c: (rows(b, c), EVEN_OM // (nh * dv))),
        ],
        out_specs=pl.BlockSpec((L, nh * dv), lambda b, c: (rows(b, c), 0)),
        out_shape=jax.ShapeDtypeStruct((m, nh * dv), BF16),
        scratch_shapes=[pltpu.VMEM((nh, dk, dv + LANES), F32),
                        pltpu.VMEM((SUBLANES, LANES), F32),
                        pltpu.VMEM((nh, L, L), F32),
                        pltpu.VMEM((nh, L, dv + LANES), F32)],
        compiler_params=_cparams("parallel", "arbitrary"),
        name="mlstm",
    )(gates, bias_row, head_gain.reshape(1, nh * dv), proj, proj, proj, proj)


BIAS_COLS = 3


def _fox_selectors():
    width = FOX_HEADS * LANES
    sel_k = np.zeros((GATE_PAD, width), np.float32)
    sel_q = np.zeros((GATE_PAD, width), np.float32)
    const_k = np.zeros((1, width), np.float32)
    const_q = np.zeros((1, width), np.float32)
    for h in range(FOX_HEADS):
        for part in range(BIAS_COLS):
            sel_k[part * FOX_HEADS + h, h * LANES + part] = 1.0
            sel_q[part * FOX_HEADS + h, h * LANES + BIAS_COLS + part] = 1.0
            const_k[0, h * LANES + BIAS_COLS + part] = 1.0
            const_q[0, h * LANES + part] = -1.0
    return (jnp.asarray(sel_k, BF16), jnp.asarray(sel_q, BF16),
            jnp.asarray(const_k), jnp.asarray(const_q))


FOX_BLOCK = 512
STAT_ROWS = SUBLANES


def _head_norms_sq(x_ref, gsum):
    return jnp.max(_dot(jnp.square(x_ref[...]), gsum), axis=0, keepdims=True)


def _fox_bias_kernel(g_ref, bias_ref, selk_ref, selq_ref, ck_ref, cq_ref, gsum_ref, q_ref, k_ref,
                     kb_ref, qb_ref, stat_ref, carry_ref, *, ts):
    @pl.when(pl.program_id(1) == 0)
    def _():
        carry_ref[...] = jnp.zeros_like(carry_ref)

    lf = _log_sigmoid(g_ref[...] + bias_ref[...])
    row = lax.broadcasted_iota(jnp.int32, (ts, ts), 0)
    col = lax.broadcasted_iota(jnp.int32, (ts, ts), 1)
    tri = jnp.where(row >= col, 1.0, 0.0).astype(BF16)
    hi, mid, lo = _split3_bf16(lf)
    c = _dot(tri, hi) + _dot(tri, mid) + _dot(tri, lo) + carry_ref[0:1, :]
    carry_ref[...] = jnp.broadcast_to(c[ts - 1:ts, :], carry_ref.shape)
    c2 = c * LOG2E
    p_hi, p_mid, p_lo = (t.astype(F32) for t in _split3_bf16(c2))
    lane = lax.broadcasted_iota(jnp.int32, (ts, GATE_PAD), 1)
    parts = jnp.where(lane < FOX_HEADS, p_hi,
                      jnp.where(lane < 2 * FOX_HEADS, pltpu.roll(p_mid, FOX_HEADS, axis=1),
                                pltpu.roll(p_lo, 2 * FOX_HEADS, axis=1))).astype(BF16)
    kb_ref[...] = (_dot(parts, selk_ref[...]) + ck_ref[...]).astype(BF16)
    qb_ref[...] = (_dot(parts, selq_ref[...]) + cq_ref[...]).astype(BF16)
    gsum = gsum_ref[...]
    stat_ref[0] = jnp.concatenate(
        [_head_norms_sq(q_ref, gsum), _head_norms_sq(k_ref, gsum), c2[0:1, :], c2[ts - 1:ts, :],
         jnp.zeros((STAT_ROWS - 4, GATE_PAD), F32)], axis=0)


def fox_bias_columns(gates, bias_row, proj, batch, seq):
    ts = FOX_BLOCK
    ns = seq // ts
    m = batch * seq
    width = FOX_HEADS * LANES
    sel_k, sel_q, const_k, const_q = _fox_selectors()
    gsum = np.zeros((width, GATE_PAD), np.float32)
    for h in range(FOX_HEADS):
        gsum[h * FOX_HEAD_DIM:(h + 1) * FOX_HEAD_DIM, h] = 1.0
    gsum = jnp.asarray(gsum, BF16)
    whole = lambda shape: pl.BlockSpec(shape, lambda b, s: (0, 0))
    rows = lambda b, s: b * ns + s
    return pl.pallas_call(
        functools.partial(_fox_bias_kernel, ts=ts),
        grid=(batch, ns),
        in_specs=[pl.BlockSpec((ts, GATE_PAD), lambda b, s: (rows(b, s), 0)),
                  whole((1, GATE_PAD)),
                  whole(sel_k.shape), whole(sel_q.shape),
                  whole((1, width)), whole((1, width)), whole(gsum.shape),
                  pl.BlockSpec((ts, width), lambda b, s: (rows(b, s), 0)),
                  pl.BlockSpec((ts, width), lambda b, s: (rows(b, s), 1))],
        out_specs=[pl.BlockSpec((ts, width), lambda b, s: (rows(b, s), 0)),
                   pl.BlockSpec((ts, width), lambda b, s: (rows(b, s), 0)),
                   pl.BlockSpec((1, STAT_ROWS, GATE_PAD), lambda b, s: (rows(b, s), 0, 0))],
        out_shape=[jax.ShapeDtypeStruct((m, width), BF16),
                   jax.ShapeDtypeStruct((m, width), BF16),
                   jax.ShapeDtypeStruct((batch * ns, STAT_ROWS, GATE_PAD), F32)],
        scratch_shapes=[pltpu.VMEM((SUBLANES, GATE_PAD), F32)],
        compiler_params=_cparams("parallel", "arbitrary"),
        name="fox_bias_columns",
    )(gates, bias_row, sel_k, sel_q, const_k, const_q, gsum, proj, proj)


FOX_SKIP_BITS = 170.0
FOX_NORM_MARGIN = 1.02


def fox_first_tiles(stats, batch, seq):
    nq = seq // FOX_BLOCK
    st = stats.reshape(batch, nq, STAT_ROWS, GATE_PAD)[..., :FOX_HEADS]
    qn = jnp.sqrt(st[:, :, 0, :]) * FOX_NORM_MARGIN
    kn = jnp.sqrt(st[:, :, 1, :])
    c_first, c_last = st[:, :, 2, :], st[:, :, 3, :]
    upper = (qn[:, :, None, :] * kn[:, None, :, :]
             + c_first[:, :, None, :] - c_last[:, None, :, :])
    row_max_lower = -(qn * kn)[:, :, None, :]
    blocks = jnp.arange(nq)
    on_or_after_diagonal = blocks[None, None, :, None] >= blocks[None, :, None, None]
    needed = (upper - row_max_lower >= -FOX_SKIP_BITS) | on_or_after_diagonal
    first = jnp.argmax(needed, axis=2).astype(jnp.int32)
    return jnp.transpose(first, (0, 2, 1)).reshape(-1)


FOX_GROUP = 8


def _fox_attn_kernel(first_ref, q_ref, qb_ref, k_ref, kb_ref, v_ref, o_ref,
                     vt_ref, m_ref, l_ref, acc_ref, s_ref, *, blk, nblk):
    for c in range(nblk):
        vt_ref[c] = v_ref[c * blk:(c + 1) * blk, :].T
    table_base = (pl.program_id(0) * pl.num_programs(1) + pl.program_id(1)) * nblk

    def query_block(qi, carry):
        _fox_query_block(qi, first_ref[table_base + qi], q_ref, qb_ref, k_ref, kb_ref, o_ref,
                         vt_ref, m_ref, l_ref, acc_ref, s_ref, blk)
        return carry

    lax.fori_loop(0, nblk, query_block, 0)


def _fox_query_block(qi, first, q_ref, qb_ref, k_ref, kb_ref, o_ref,
                     vt_ref, m_ref, l_ref, acc_ref, s_ref, blk):
    group = FOX_GROUP
    q_rows = pl.ds(pl.multiple_of(qi * blk, blk), blk)

    q_aug_t = jnp.concatenate([q_ref[q_rows, :].T, qb_ref[q_rows, :].T], axis=0)
    m_ref[...] = jnp.full_like(m_ref, -jnp.inf)
    l_ref[...] = jnp.zeros_like(l_ref)
    acc_ref[...] = jnp.zeros_like(acc_ref)

    def tiles(kbs, masks):
        n = len(kbs)
        start = pl.multiple_of(kbs[0] * blk, blk)
        k_aug = jnp.concatenate([k_ref[pl.ds(start, n * blk), :], kb_ref[pl.ds(start, n * blk), :]],
                                axis=1)
        s_all = _dot(k_aug, q_aug_t)
        tile_max = []
        for t, masked in enumerate(masks):
            s = s_all[t * blk:(t + 1) * blk, :]
            if masked:
                key = lax.broadcasted_iota(jnp.int32, (blk, blk), 0)
                qry = lax.broadcasted_iota(jnp.int32, (blk, blk), 1)
                s = jnp.where(key <= qry, s, -jnp.inf)
            s_ref[t] = s
            tile_max.append(jnp.max(s, axis=0, keepdims=True))
        m_run = m_ref[...]
        l_run = l_ref[...]
        acc = acc_ref[...]
        for t, kb in enumerate(kbs):
            m_new = jnp.maximum(m_run, tile_max[t])
            alpha = jnp.exp2(m_run - m_new)
            p = jnp.exp2(s_ref[t] - m_new)
            l_run = alpha * l_run + jnp.sum(p, axis=0, keepdims=True)
            acc = alpha * acc + _dot(vt_ref[kb], p.astype(BF16))
            m_run = m_new
        acc_ref[...] = acc
        l_ref[...] = l_run
        m_ref[...] = m_run

    n_full = qi - first

    def body(j, carry):
        tiles([first + group * j + t for t in range(group)], [False] * group)
        return carry

    lax.fori_loop(0, n_full // group, body, 0)

    for r in range(group):
        @pl.when(n_full % group == r)
        def _(r=r):
            tiles([jnp.maximum(qi - r, 0) + t for t in range(r + 1)], [False] * r + [True])

    o_ref[q_rows, :] = (acc_ref[...] / l_ref[...]).T.astype(o_ref.dtype)


def fox_attention(proj, key_bias, query_bias, first_tiles, batch, seq):
    blk = FOX_BLOCK
    nq = seq // blk
    m = batch * seq
    nh, dh = FOX_HEADS, FOX_HEAD_DIM
    return pl.pallas_call(
        functools.partial(_fox_attn_kernel, blk=blk, nblk=nq),
        grid=(batch, nh),
        in_specs=[
            pl.BlockSpec(memory_space=pltpu.SMEM),
            pl.BlockSpec((seq, dh), lambda b, h: (b, h)),
            pl.BlockSpec((seq, LANES), lambda b, h: (b, h)),
            pl.BlockSpec((seq, dh), lambda b, h: (b, nh + h)),
            pl.BlockSpec((seq, LANES), lambda b, h: (b, h)),
            pl.BlockSpec((seq, dh), lambda b, h: (b, 2 * nh + h)),
        ],
        out_specs=pl.BlockSpec((seq, dh), lambda b, h: (b, h)),
        out_shape=jax.ShapeDtypeStruct((m, nh * dh), BF16),
        scratch_shapes=[pltpu.VMEM((nq, dh, blk), BF16),
                        pltpu.VMEM((1, blk), F32), pltpu.VMEM((1, blk), F32),
                        pltpu.VMEM((dh, blk), F32),
                        pltpu.VMEM((FOX_GROUP, blk, blk), F32)],
        compiler_params=_cparams("parallel", "parallel"),
        name="fox_attention",
    )(first_tiles, proj, query_bias, proj, key_bias, proj)


def _pad_cols(t, width):
    return jnp.pad(t, ((0, 0), (0, width - t.shape[1])))


def _even_col_scale():
    cs = jnp.ones((EVEN_MAIN,), F32)
    cs = cs.at[EVEN_QS:EVEN_QS + SWA_Q_HEADS * HEAD_DIM].set(HEAD_DIM ** -0.5 * LOG2E)
    cs = cs.at[EVEN_QM:EVEN_QM + MLSTM_HEADS * MLSTM_QK_DIM].set(MLSTM_QK_DIM ** -0.5)
    return cs.reshape(1, EVEN_MAIN)


def _fox_col_scale():
    cs = jnp.ones((FOX_MAIN,), F32)
    cs = cs.at[:FOX_HEADS * FOX_HEAD_DIM].set(FOX_HEAD_DIM ** -0.5 * LOG2E)
    return cs.reshape(1, FOX_MAIN)


def _even_mixer(h, x_in, gain, w_in, w_out, sinks, b_i, b_f, head_gain, ffn_gain, batch, seq):
    q_s, k_s, v_s, q_m, k_m, v_m, o_m, gate_cols = jnp.split(
        w_in, [1024, 1280, 1536, 2048, 2560, 3584, 4608], axis=1)
    w_main = jnp.concatenate([q_s, v_m, o_m, q_m, k_m, k_s, v_s], axis=1).astype(BF16)
    w_gate = _pad_cols(gate_cols, GATE_PAD).astype(BF16)
    proj, gates = norm_proj(x_in, gain, w_main, _even_col_scale(), w_gate)
    attn = swa_attention(proj, sinks, batch, seq)
    bias_row = _pad_cols(jnp.concatenate([b_i, b_f]).reshape(1, -1), GATE_PAD)
    mem = mlstm(proj, gates, bias_row, head_gain, batch, seq)
    return out_proj_norm([attn, mem], w_out.astype(BF16), h, ffn_gain)


def _fox_mixer(h, x_in, gain, w_in, b_f, w_out, ffn_gain, batch, seq):
    w_main = w_in[:, :FOX_MAIN].astype(BF16)
    w_gate = _pad_cols(w_in[:, FOX_MAIN:], GATE_PAD).astype(BF16)
    proj, gates = norm_proj(x_in, gain, w_main, _fox_col_scale(), w_gate)
    key_bias, query_bias, stats = fox_bias_columns(
        gates, _pad_cols(b_f.reshape(1, -1), GATE_PAD), proj, batch, seq)
    first_tiles = fox_first_tiles(stats, batch, seq)
    attn = fox_attention(proj, key_bias, query_bias, first_tiles, batch, seq)
    return out_proj_norm([attn], w_out.astype(BF16), h, ffn_gain)


def kernel(x, norm_mix, norm_ffn, norm_final, w_in_even, w_out_even, swa_sinks, mlstm_b_i, mlstm_b_f, mlstm_head_gain, w_in_odd, fox_b_f, w_out_odd, w_gate, w_up, w_down):
    batch, seq, d = x.shape
    depth = norm_mix.shape[0]
    h = x.reshape(batch * seq, d)
    x_in = h
    w_gate, w_up, w_down = cast_bf16(w_gate), cast_bf16(w_up), cast_bf16(w_down)
    for layer in range(depth):
        j = layer // 2
        if layer % 2 == 0:
            h, xn = _even_mixer(h, x_in, norm_mix[layer], w_in_even[j], w_out_even[j],
                                swa_sinks[j], mlstm_b_i[j], mlstm_b_f[j], mlstm_head_gain[j],
                                norm_ffn[layer], batch, seq)
        else:
            h, xn = _fox_mixer(h, x_in, norm_mix[layer], w_in_odd[j], fox_b_f[j], w_out_odd[j],
                               norm_ffn[layer], batch, seq)
        act = ffn_up(xn, w_gate, w_up, layer)
        if layer + 1 < depth:
            h, x_in = down_proj_norm(act, w_down, layer, h, norm_mix[layer + 1], last=False)
        else:
            (out,) = down_proj_norm(act, w_down, layer, h, norm_final, last=True)
    return out.reshape(batch, seq, d)
```
